```python
import math
import jax, jax.numpy as jnp
from jax import lax
import numpy as np

D_MODEL = 1024
BATCH = 4
SEQ = 4096
DEPTH = 1

DA_HEADS = 4
DA_QK_DIM = 64
DA_V_DIM = 2 * DA_QK_DIM
SB_HEADS = 4
SB_HEAD_DIM = 128
D_FF = 2816
Q_BLOCK = 128
NORM_EPS = 1e-5
N_BRANCHES = 2
DA_QK_WIDTH = DA_HEADS * 2 * DA_QK_DIM
DA_V_WIDTH = DA_HEADS * DA_V_DIM
SB_WIDTH = SB_HEADS * SB_HEAD_DIM
IN_SPLITS = (DA_QK_WIDTH, DA_QK_WIDTH, DA_V_WIDTH, SB_WIDTH, SB_WIDTH, SB_WIDTH, N_BRANCHES * D_MODEL)
D_IN = sum(IN_SPLITS)
IN_SPLIT_POINTS = [int(v) for v in np.cumsum(IN_SPLITS)[:-1]]

kernel_name = "hybrid_gated_diffattn_stickbreak_macaron"


def rmsnorm(x, g):
    xf = x.astype(jnp.float32)
    y = xf * lax.rsqrt(jnp.mean(xf * xf, axis=-1, keepdims=True) + NORM_EPS)
    return (y * g.astype(jnp.float32)).astype(x.dtype)


def swiglu(x, w_gate, w_up, w_down):
    return (jax.nn.silu(x @ w_gate) * (x @ w_up)) @ w_down


def alibi_slopes(n_heads):
    return jnp.exp2(-8.0 * jnp.arange(1, n_heads + 1, dtype=jnp.float32) / n_heads)


def lambda_init_fn(layer_idx):
    return 0.8 - 0.6 * math.exp(-0.3 * layer_idx)


def diff_attention(q, k, v, lam):
    B, H, _, S, _ = q.shape
    dv = v.shape[-1]
    n_blocks = S // Q_BLOCK
    scale = DA_QK_DIM ** -0.5
    slopes = alibi_slopes(H)[:, None, None, None]
    k_pos = jnp.arange(S)

    def block(i):
        start = i * Q_BLOCK
        qb = lax.dynamic_slice_in_dim(q, start, Q_BLOCK, axis=3)
        q_pos = start + jnp.arange(Q_BLOCK)
        dist = q_pos[:, None] - k_pos[None, :]
        causal = dist >= 0
        s = jnp.einsum('bhmqd,bhmkd->bhmqk', qb, k).astype(jnp.float32) * scale
        s = s - slopes * dist.astype(jnp.float32)
        s = jnp.where(causal, s, -jnp.inf)
        p = jax.nn.softmax(s, axis=-1)
        w = p[:, :, 0] - lam * p[:, :, 1]
        return jnp.einsum('bhqk,bhkd->bhqd', w.astype(v.dtype), v)

    out = lax.map(block, jnp.arange(n_blocks))
    return jnp.moveaxis(out, 0, 2).reshape(B, H, S, dv)


def stick_breaking_attention(q, k, v):
    B, H, S, d = q.shape
    n_blocks = S // Q_BLOCK
    scale = d ** -0.5
    k_pos = jnp.arange(S)

    def block(i):
        start = i * Q_BLOCK
        qb = lax.dynamic_slice_in_dim(q, start, Q_BLOCK, axis=2)
        q_pos = start + jnp.arange(Q_BLOCK)
        strict = k_pos[None, :] < q_pos[:, None]
        z = jnp.einsum('bhqd,bhkd->bhqk', qb, k).astype(jnp.float32) * scale
        u = jnp.where(strict, jax.nn.softplus(z), 0.0)
        tail = lax.cumsum(u, axis=3, reverse=True) - u
        log_a = jax.nn.log_sigmoid(z) - tail
        a = jnp.where(strict, jnp.exp(log_a), 0.0)
        return jnp.einsum('bhqk,bhkd->bhqd', a.astype(v.dtype), v)

    out = lax.map(block, jnp.arange(n_blocks))
    return jnp.moveaxis(out, 0, 2).reshape(B, H, S, d)


def gated_mixer(xn, w_in, b_gate, lambda_q1, lambda_k1, lambda_q2, lambda_k2,
                diff_subln, w_branch_diff, w_branch_sb, w_out, lam_init):
    B, S, _ = xn.shape
    proj = xn @ w_in
    da_q, da_k, da_v, sb_q, sb_k, sb_v, gate_logits = jnp.split(proj, IN_SPLIT_POINTS, axis=-1)

    da_q = da_q.reshape(B, S, DA_HEADS, 2, DA_QK_DIM).transpose(0, 2, 3, 1, 4)
    da_k = da_k.reshape(B, S, DA_HEADS, 2, DA_QK_DIM).transpose(0, 2, 3, 1, 4)
    da_v = da_v.reshape(B, S, DA_HEADS, DA_V_DIM).transpose(0, 2, 1, 3)
    lam = (jnp.exp(jnp.sum(lambda_q1.astype(jnp.float32) * lambda_k1.astype(jnp.float32)))
           - jnp.exp(jnp.sum(lambda_q2.astype(jnp.float32) * lambda_k2.astype(jnp.float32)))
           + lam_init)
    a = diff_attention(da_q, da_k, da_v, lam)
    a = rmsnorm(a, diff_subln) * (1.0 - lam_init)
    a = a.transpose(0, 2, 1, 3).reshape(B, S, DA_V_WIDTH)

    sb_q = sb_q.reshape(B, S, SB_HEADS, SB_HEAD_DIM).transpose(0, 2, 1, 3)
    sb_k = sb_k.reshape(B, S, SB_HEADS, SB_HEAD_DIM).transpose(0, 2, 1, 3)
    sb_v = sb_v.reshape(B, S, SB_HEADS, SB_HEAD_DIM).transpose(0, 2, 1, 3)
    b = stick_breaking_attention(sb_q, sb_k, sb_v)
    b = b.transpose(0, 2, 1, 3).reshape(B, S, SB_WIDTH)

    gates = jax.nn.sigmoid(gate_logits + b_gate).reshape(B, S, N_BRANCHES, D_MODEL)
    y = gates[:, :, 0] * (a @ w_branch_diff) + gates[:, :, 1] * (b @ w_branch_sb)
    return y @ w_out


def setup_inputs(seed: int = 0) -> dict:
    key = jax.random.key(seed)
    ks = jax.random.split(key, 24)
    f32 = jnp.float32
    L = DEPTH

    def w(k, shape, fan_in):
        return jax.random.normal(k, shape, f32) * fan_in ** -0.5

    def gain(k, shape):
        return 1.0 + 0.01 * jax.random.normal(k, shape, f32)

    return {
        "x": jax.random.normal(ks[0], (BATCH, SEQ, D_MODEL), f32),
        "ffn1_norm": gain(ks[1], (L, D_MODEL)),
        "ffn1_w_gate": w(ks[2], (L, D_MODEL, D_FF), D_MODEL),
        "ffn1_w_up": w(ks[3], (L, D_MODEL, D_FF), D_MODEL),
        "ffn1_w_down": w(ks[4], (L, D_FF, D_MODEL), D_FF),
        "mix_norm": gain(ks[5], (L, D_MODEL)),
        "w_in": w(ks[6], (L, D_MODEL, D_IN), D_MODEL),
        "b_gate": 0.01 * jax.random.normal(ks[7], (L, N_BRANCHES * D_MODEL), f32),
        "lambda_q1": 0.1 * jax.random.normal(ks[8], (L, DA_QK_DIM), f32),
        "lambda_k1": 0.1 * jax.random.normal(ks[9], (L, DA_QK_DIM), f32),
        "lambda_q2": 0.1 * jax.random.normal(ks[10], (L, DA_QK_DIM), f32),
        "lambda_k2": 0.1 * jax.random.normal(ks[11], (L, DA_QK_DIM), f32),
        "diff_subln": gain(ks[12], (L, DA_V_DIM)),
        "w_branch_diff": w(ks[13], (L, DA_V_WIDTH, D_MODEL), DA_V_WIDTH),
        "w_branch_sb": w(ks[14], (L, SB_WIDTH, D_MODEL), SB_WIDTH),
        "w_out": w(ks[15], (L, D_MODEL, D_MODEL), D_MODEL),
        "ffn2_norm": gain(ks[16], (L, D_MODEL)),
        "ffn2_w_gate": w(ks[17], (L, D_MODEL, D_FF), D_MODEL),
        "ffn2_w_up": w(ks[18], (L, D_MODEL, D_FF), D_MODEL),
        "ffn2_w_down": w(ks[19], (L, D_FF, D_MODEL), D_FF),
        "final_norm": gain(ks[20], (D_MODEL,)),
    }


def reference(x, ffn1_norm, ffn1_w_gate, ffn1_w_up, ffn1_w_down, mix_norm, w_in, b_gate,
              lambda_q1, lambda_k1, lambda_q2, lambda_k2, diff_subln, w_branch_diff,
              w_branch_sb, w_out, ffn2_norm, ffn2_w_gate, ffn2_w_up, ffn2_w_down, final_norm):
    h = x
    for l in range(DEPTH):
        lam_init = lambda_init_fn(l)
        h = h + 0.5 * swiglu(rmsnorm(h, ffn1_norm[l]), ffn1_w_gate[l], ffn1_w_up[l], ffn1_w_down[l])
        h = h + gated_mixer(rmsnorm(h, mix_norm[l]), w_in[l], b_gate[l],
                            lambda_q1[l], lambda_k1[l], lambda_q2[l], lambda_k2[l],
                            diff_subln[l], w_branch_diff[l], w_branch_sb[l], w_out[l], lam_init)
        h = h + 0.5 * swiglu(rmsnorm(h, ffn2_norm[l]), ffn2_w_gate[l], ffn2_w_up[l], ffn2_w_down[l])
    return rmsnorm(h, final_norm)
```

```python
import functools
import math

import jax
import jax.numpy as jnp
from jax import lax
from jax.experimental import pallas as pl
from jax.experimental.pallas import tpu as pltpu

D_MODEL = 1024
D_FF = 2816
N_HEADS = 4
HEAD_W = 128
DA_QK_DIM = 64
QKV_W = 6 * N_HEADS * HEAD_W
GATE_W = 2 * D_MODEL
NORM_EPS = 1e-5
LAMBDA_INIT = 0.8 - 0.6 * math.exp(-0.3 * 0)

VMEM_LIMIT_BYTES = 56 * 1024 * 1024

FFN_TM = 256
PROJ_TM = 512
ATT_TQ = 256
ATT_TK = 256

_NT = (((1,), (1,)), ((), ()))


def _rms(x, g):
    ms = jnp.mean(x * x, axis=-1, keepdims=True)
    return x * lax.rsqrt(ms + NORM_EPS) * g


def _swiglu_half_step(x, norm_g, wg_ref, wu_ref, wd_ref):
    xn = _rms(x, norm_g).astype(jnp.bfloat16)
    g = jnp.dot(xn, wg_ref[...], preferred_element_type=jnp.float32)
    u = jnp.dot(xn, wu_ref[...], preferred_element_type=jnp.float32)
    hact = (g * jax.nn.sigmoid(g) * u).astype(jnp.bfloat16)
    return x + 0.5 * jnp.dot(hact, wd_ref[...], preferred_element_type=jnp.float32)


def _ffn1_kernel(x_ref, norm_ref, wg_ref, wu_ref, wd_ref, o_ref):
    o_ref[...] = _swiglu_half_step(x_ref[...], norm_ref[...], wg_ref, wu_ref, wd_ref)


def _in_proj_kernel(h_ref, norm_ref, win_ref, colscale_ref, bgate_ref, qkv_ref, gate_ref):
    n = _rms(h_ref[...], norm_ref[...]).astype(jnp.bfloat16)
    proj = jnp.dot(n, win_ref[...], preferred_element_type=jnp.float32)
    qkv_ref[...] = (proj[:, :QKV_W] * colscale_ref[...]).astype(jnp.bfloat16)
    gate_ref[...] = jax.nn.sigmoid(proj[:, QKV_W:] + bgate_ref[...]).astype(jnp.bfloat16)


def _diff_attn_kernel(slope_ref, lam_ref, q_ref, k_ref, v_ref, subln_ref, o_ref,
                      m_ref, l_ref, acc_ref):
    tq, tk = ATT_TQ, ATT_TK
    h = pl.program_id(1)
    i = pl.program_id(2)
    slope = slope_ref[h]
    lam = lam_ref[0]

    q = q_ref[0]
    lane = lax.broadcasted_iota(jnp.int32, (tq, HEAD_W), 1)
    zero = jnp.zeros_like(q)
    qq = jnp.concatenate([jnp.where(lane < DA_QK_DIM, q, zero),
                          jnp.where(lane >= DA_QK_DIM, q, zero)], axis=0)

    row = lax.broadcasted_iota(jnp.int32, (2 * tq, tk), 0)
    col = lax.broadcasted_iota(jnp.int32, (2 * tq, tk), 1)
    rel = jnp.where(row >= tq, row - tq, row) - col
    rel_bias = -slope * rel.astype(jnp.float32)

    m_ref[...] = jnp.full_like(m_ref, -jnp.inf)
    l_ref[...] = jnp.zeros_like(l_ref)
    acc_ref[...] = jnp.zeros_like(acc_ref)

    def step(j, masked):
        kb = k_ref[0, pl.ds(pl.multiple_of(j * tk, tk), tk), :]
        vb = v_ref[0, pl.ds(pl.multiple_of(j * tk, tk), tk), :]
        s = lax.dot_general(qq, kb, _NT, preferred_element_type=jnp.float32)
        shift = -slope * ((i - j) * tq).astype(jnp.float32)
        s = s + (rel_bias + shift)
        if masked:
            s = jnp.where(rel >= 0, s, -jnp.inf)
        m_prev = m_ref[...]
        m_new = jnp.maximum(m_prev, jnp.max(s, axis=-1, keepdims=True))
        alpha = jnp.exp(m_prev - m_new)
        p = jnp.exp(s - m_new)
        l_ref[...] = alpha * l_ref[...] + jnp.sum(p, axis=-1, keepdims=True)
        acc_ref[...] = alpha * acc_ref[...] + jnp.dot(
            p.astype(jnp.bfloat16), vb, preferred_element_type=jnp.float32)
        m_ref[...] = m_new

    def body(j, carry):
        step(j, masked=False)
        return carry

    lax.fori_loop(0, i, body, 0)
    step(i, masked=True)

    o = acc_ref[...] / l_ref[...]
    a = o[:tq] - lam * o[tq:]
    a = _rms(a, subln_ref[...]) * (1.0 - LAMBDA_INIT)
    o_ref[0] = a.astype(o_ref.dtype)


def _sb_attn_kernel(q_ref, k_ref, v_ref, o_ref, c_ref, acc_ref):
    tq, tk = ATT_TQ, ATT_TK
    i = pl.program_id(2)
    q = q_ref[0]

    row = lax.broadcasted_iota(jnp.int32, (tq, tk), 0)
    col = lax.broadcasted_iota(jnp.int32, (tq, tk), 1)
    strict = col < row
    r2 = lax.broadcasted_iota(jnp.int32, (tk, tk), 0)
    c2 = lax.broadcasted_iota(jnp.int32, (tk, tk), 1)
    upper = jnp.where(r2 > c2, 1.0, 0.0).astype(jnp.bfloat16)

    c_ref[...] = jnp.zeros_like(c_ref)
    acc_ref[...] = jnp.zeros_like(acc_ref)

    def step(j, masked):
        kb = k_ref[0, pl.ds(pl.multiple_of(j * tk, tk), tk), :]
        vb = v_ref[0, pl.ds(pl.multiple_of(j * tk, tk), tk), :]
        z = lax.dot_general(q, kb, _NT, preferred_element_type=jnp.float32)
        t = jnp.log(1.0 + jnp.exp(-jnp.abs(z)))
        u = jnp.maximum(z, 0.0) + t
        log_sig = jnp.minimum(z, 0.0) - t
        if masked:
            u = jnp.where(strict, u, 0.0)
        tail = jnp.dot(u.astype(jnp.bfloat16), upper, preferred_element_type=jnp.float32)
        a = jnp.exp(log_sig - tail - c_ref[...])
        if masked:
            a = jnp.where(strict, a, 0.0)
        acc_ref[...] += jnp.dot(a.astype(jnp.bfloat16), vb, preferred_element_type=jnp.float32)
        c_ref[...] += jnp.sum(u, axis=-1, keepdims=True)

    step(i, masked=True)

    def body(jj, carry):
        step(i - 1 - jj, masked=False)
        return carry

    lax.fori_loop(0, i, body, 0)
    o_ref[0] = acc_ref[...].astype(o_ref.dtype)


def _mix_ffn2_kernel(h_ref, a_ref, b_ref, gate_ref, wa_ref, wb_ref, wout_ref,
                     norm2_ref, wg_ref, wu_ref, wd_ref, normf_ref, o_ref):
    ya = jnp.dot(a_ref[...], wa_ref[...], preferred_element_type=jnp.float32)
    yb = jnp.dot(b_ref[...], wb_ref[...], preferred_element_type=jnp.float32)
    gate = gate_ref[...].astype(jnp.float32)
    y = (gate[:, :D_MODEL] * ya + gate[:, D_MODEL:] * yb).astype(jnp.bfloat16)
    h2 = h_ref[...] + jnp.dot(y, wout_ref[...], preferred_element_type=jnp.float32)
    h3 = _swiglu_half_step(h2, norm2_ref[...], wg_ref, wu_ref, wd_ref)
    o_ref[...] = _rms(h3, normf_ref[...])


def _const_spec(shape):
    return pl.BlockSpec(shape, lambda *_: (0,) * len(shape))


def _tc_params(n_axes):
    return pltpu.CompilerParams(dimension_semantics=("arbitrary",) * n_axes,
                                vmem_limit_bytes=VMEM_LIMIT_BYTES)


def kernel(x, ffn1_norm, ffn1_w_gate, ffn1_w_up, ffn1_w_down, mix_norm, w_in, b_gate, lambda_q1, lambda_k1, lambda_q2, lambda_k2, diff_subln, w_branch_diff, w_branch_sb, w_out, ffn2_norm, ffn2_w_gate, ffn2_w_up, ffn2_w_down, final_norm):
    B, S, D = x.shape
    T = B * S
    f32, bf16 = jnp.float32, jnp.bfloat16
    xt = x.reshape(T, D)
    row = lambda v: v.reshape(1, -1).astype(f32)

    tok_spec = pl.BlockSpec((FFN_TM, D), lambda t: (t, 0))
    h1 = pl.pallas_call(
        _ffn1_kernel,
        grid=(T // FFN_TM,),
        in_specs=[tok_spec, _const_spec((1, D)), _const_spec((D, D_FF)),
                  _const_spec((D, D_FF)), _const_spec((D_FF, D))],
        out_specs=tok_spec,
        out_shape=jax.ShapeDtypeStruct((T, D), f32),
        compiler_params=_tc_params(1),
        name="ffn1",
    )(xt, row(ffn1_norm[0]), ffn1_w_gate[0].astype(bf16), ffn1_w_up[0].astype(bf16),
      ffn1_w_down[0].astype(bf16))

    colscale = jnp.ones((QKV_W,), f32)
    colscale = colscale.at[0:512].set(DA_QK_DIM ** -0.5)
    colscale = colscale.at[1536:2048].set(HEAD_W ** -0.5)
    qkv, gates = pl.pallas_call(
        _in_proj_kernel,
        grid=(T // PROJ_TM,),
        in_specs=[pl.BlockSpec((PROJ_TM, D), lambda t: (t, 0)), _const_spec((1, D)),
                  _const_spec((D, QKV_W + GATE_W)), _const_spec((1, QKV_W)),
                  _const_spec((1, GATE_W))],
        out_specs=[pl.BlockSpec((PROJ_TM, QKV_W), lambda t: (t, 0)),
                   pl.BlockSpec((PROJ_TM, GATE_W), lambda t: (t, 0))],
        out_shape=[jax.ShapeDtypeStruct((T, QKV_W), bf16),
                   jax.ShapeDtypeStruct((T, GATE_W), bf16)],
        compiler_params=_tc_params(1),
        name="in_proj",
    )(h1, row(mix_norm[0]), w_in[0].astype(bf16), colscale.reshape(1, -1), row(b_gate[0]))
    qkv3 = qkv.reshape(B, S, QKV_W)

    nq = S // ATT_TQ

    def q_spec(slab):
        return pl.BlockSpec((1, ATT_TQ, HEAD_W), lambda b, h, i: (b, i, slab * N_HEADS + h))

    def kv_spec(slab):
        return pl.BlockSpec((1, S, HEAD_W), lambda b, h, i: (b, 0, slab * N_HEADS + h))

    att_out_spec = pl.BlockSpec((1, ATT_TQ, HEAD_W), lambda b, h, i: (b, i, h))
    att_out_shape = jax.ShapeDtypeStruct((B, S, N_HEADS * HEAD_W), bf16)
    smem_spec = pl.BlockSpec(memory_space=pltpu.SMEM)

    lam = (jnp.exp(jnp.sum(lambda_q1[0].astype(f32) * lambda_k1[0].astype(f32)))
           - jnp.exp(jnp.sum(lambda_q2[0].astype(f32) * lambda_k2[0].astype(f32)))
           + LAMBDA_INIT).reshape(1)
    slopes = jnp.exp2(-8.0 * jnp.arange(1, N_HEADS + 1, dtype=f32) / N_HEADS)

    a = pl.pallas_call(
        _diff_attn_kernel,
        grid=(B, N_HEADS, nq),
        in_specs=[smem_spec, smem_spec, q_spec(0), kv_spec(1), kv_spec(2),
                  _const_spec((1, HEAD_W))],
        out_specs=att_out_spec,
        out_shape=att_out_shape,
        scratch_shapes=[pltpu.VMEM((2 * ATT_TQ, 1), f32), pltpu.VMEM((2 * ATT_TQ, 1), f32),
                        pltpu.VMEM((2 * ATT_TQ, HEAD_W), f32)],
        compiler_params=_tc_params(3),
        name="diff_attn",
    )(slopes, lam, qkv3, qkv3, qkv3, row(diff_subln[0]))

    b = pl.pallas_call(
        _sb_attn_kernel,
        grid=(B, N_HEADS, nq),
        in_specs=[q_spec(3), kv_spec(4), kv_spec(5)],
        out_specs=att_out_spec,
        out_shape=att_out_shape,
        scratch_shapes=[pltpu.VMEM((ATT_TQ, 1), f32), pltpu.VMEM((ATT_TQ, HEAD_W), f32)],
        compiler_params=_tc_params(3),
        name="sb_attn",
    )(qkv3, qkv3, qkv3)

    out = pl.pallas_call(
        _mix_ffn2_kernel,
        grid=(T // FFN_TM,),
        in_specs=[tok_spec,
                  pl.BlockSpec((FFN_TM, N_HEADS * HEAD_W), lambda t: (t, 0)),
                  pl.BlockSpec((FFN_TM, N_HEADS * HEAD_W), lambda t: (t, 0)),
                  pl.BlockSpec((FFN_TM, GATE_W), lambda t: (t, 0)),
                  _const_spec((N_HEADS * HEAD_W, D)), _const_spec((N_HEADS * HEAD_W, D)),
                  _const_spec((D, D)), _const_spec((1, D)), _const_spec((D, D_FF)),
                  _const_spec((D, D_FF)), _const_spec((D_FF, D)), _const_spec((1, D))],
        out_specs=tok_spec,
        out_shape=jax.ShapeDtypeStruct((T, D), f32),
        compiler_params=_tc_params(1),
        name="mix_ffn2",
    )(h1, a.reshape(T, -1), b.reshape(T, -1), gates,
      w_branch_diff[0].astype(bf16), w_branch_sb[0].astype(bf16), w_out[0].astype(bf16),
      row(ffn2_norm[0]), ffn2_w_gate[0].astype(bf16), ffn2_w_up[0].astype(bf16),
      ffn2_w_down[0].astype(bf16), row(final_norm))
    return out.reshape(B, S, D)
```

```python
import math

import jax
import jax.numpy as jnp
from jax import lax
from jax.experimental import pallas as pl
from jax.experimental.pallas import tpu as pltpu

D_MODEL = 1024
D_FF = 2816
N_HEADS = 4
HEAD_W = 128
DA_QK_DIM = 64
ATT_W = N_HEADS * HEAD_W
K_W = 2 * ATT_W
QVT_W = 4 * ATT_W
GATE_W = 2 * D_MODEL
NORM_EPS = 1e-5
LAMBDA_INIT = 0.8 - 0.6 * math.exp(-0.3 * 0)
LOG2E = 1.0 / math.log(2.0)

VMEM_LIMIT_BYTES = 56 * 1024 * 1024

FFN_TM = 256
PROJ_TM = 512
ATT_TQ = 512
ATT_TK = 512
CHAIN_W = 256

_NT = (((1,), (1,)), ((), ()))


def _rms(x, g):
    ms = jnp.mean(x * x, axis=-1, keepdims=True)
    return x * lax.rsqrt(ms + NORM_EPS) * g


def _swiglu_half_step(x, norm_g, wg_ref, wu_ref, wd_ref):
    xn = _rms(x, norm_g).astype(jnp.bfloat16)
    g = jnp.dot(xn, wg_ref[...], preferred_element_type=jnp.float32)
    u = jnp.dot(xn, wu_ref[...], preferred_element_type=jnp.float32)
    hact = (g * jax.nn.sigmoid(g) * u).astype(jnp.bfloat16)
    return x + 0.5 * jnp.dot(hact, wd_ref[...], preferred_element_type=jnp.float32)


def _ffn1_kernel(x_ref, norm_ref, wg_ref, wu_ref, wd_ref, o_ref):
    o_ref[...] = _swiglu_half_step(x_ref[...], norm_ref[...], wg_ref, wu_ref, wd_ref)


def _in_proj_kernel(h_ref, norm_ref, wk_ref, wqvt_ref, wgate_ref, rowscale_ref, bgate_ref,
                    k_ref, qvt_ref, gate_ref):
    n = _rms(h_ref[...], norm_ref[...]).astype(jnp.bfloat16)
    k_ref[...] = jnp.dot(n, wk_ref[...], preferred_element_type=jnp.float32).astype(jnp.bfloat16)
    qvt = lax.dot_general(wqvt_ref[...], n, _NT, preferred_element_type=jnp.float32)
    qvt_ref[...] = (qvt * rowscale_ref[...]).astype(jnp.bfloat16)
    g = jnp.dot(n, wgate_ref[...], preferred_element_type=jnp.float32)
    gate_ref[...] = jax.nn.sigmoid(g + bgate_ref[...]).astype(jnp.bfloat16)


def _emit_pipelined(stages, n):
    state = [dict() for _ in range(n)]
    for step in range(n + len(stages) - 1):
        for s, stage in enumerate(stages):
            t = step - s
            if 0 <= t < n:
                stage(t, state[t], state[t + 1] if t + 1 < n else None)


def _ordered_after(x, token):
    zero = lax.shift_right_logical(
        lax.shift_right_logical(pltpu.bitcast(token, jnp.uint32), jnp.uint32(16)), jnp.uint32(16))
    return pltpu.bitcast(pltpu.bitcast(x, jnp.uint32) + zero, jnp.float32)


def _diff_attn_kernel(slope_ref, lam_ref, q_ref, k_ref, vt_ref, subln_ref, o_ref,
                      bias_ref, bias_diag_ref, m_ref, l_ref, acc_ref):
    tq, tk, cw = ATT_TQ, ATT_TK, CHAIN_W
    per_map = tq // cw
    n_chains = 2 * per_map
    h = pl.program_id(1)
    i = pl.program_id(2)
    slope = slope_ref[h] * LOG2E
    lam = lam_ref[0]

    qt = q_ref[...]
    chan = lax.broadcasted_iota(jnp.int32, (HEAD_W, tq), 0)
    zero = jnp.zeros_like(qt)
    q_maps = (jnp.where(chan < DA_QK_DIM, qt, zero), jnp.where(chan >= DA_QK_DIM, qt, zero))
    q_chain = [q_maps[c // per_map][:, (c % per_map) * cw:(c % per_map + 1) * cw]
               for c in range(n_chains)]

    krow = lax.broadcasted_iota(jnp.int32, (tk, tq), 0)
    qcol = lax.broadcasted_iota(jnp.int32, (tk, tq), 1)
    rel = qcol - krow
    bias = -slope * rel.astype(jnp.float32)
    bias_ref[...] = bias
    bias_diag_ref[...] = jnp.where(rel >= 0, bias, -jnp.inf)

    m_ref[...] = jnp.full_like(m_ref, -jnp.inf)
    l_ref[...] = jnp.zeros_like(l_ref)
    acc_ref[...] = jnp.zeros_like(acc_ref)

    def block(j, b_ref):
        kb = k_ref[0, pl.ds(pl.multiple_of(j * tk, tk), tk), :]
        vtb = vt_ref[:, pl.ds(pl.multiple_of(j * tk, tk), tk)]
        shift = -slope * ((i - j) * tq).astype(jnp.float32)

        def scores(c, st, nxt):
            st["s"] = jnp.dot(kb, q_chain[c], preferred_element_type=jnp.float32)

        def column_max(c, st, nxt):
            qp = c % per_map
            st["sb"] = st.pop("s") + b_ref[:, qp * cw:(qp + 1) * cw]
            st["cmax"] = jnp.max(st["sb"], axis=0, keepdims=True) + shift

        def softmax_pv(c, st, nxt):
            lanes = slice(c * cw, (c + 1) * cw)
            m_prev = m_ref[:, lanes]
            m_new = jnp.maximum(m_prev, st.pop("cmax"))
            if nxt is not None:
                m_new = _ordered_after(m_new, nxt["cmax"])
            st["alpha"] = jnp.exp2(m_prev - m_new)
            p = jnp.exp2(st.pop("sb") - (m_new - shift))
            l_ref[:, lanes] = st["alpha"] * l_ref[:, lanes] + jnp.sum(p, axis=0, keepdims=True)
            m_ref[:, lanes] = m_new
            st["pv"] = jnp.dot(vtb, p.astype(jnp.bfloat16),
                               preferred_element_type=jnp.float32)

        def accumulate(c, st, nxt):
            lanes = slice(c * cw, (c + 1) * cw)
            acc_ref[:, lanes] = st.pop("alpha") * acc_ref[:, lanes] + st.pop("pv")

        _emit_pipelined((scores, column_max, softmax_pv, accumulate), n_chains)

    def body(j, carry):
        block(j, bias_ref)
        return carry

    lax.fori_loop(0, i, body, 0)
    block(i, bias_diag_ref)

    o = acc_ref[...] / l_ref[...]
    a = o[:, :tq] - lam * o[:, tq:]
    ms = jnp.mean(a * a, axis=0, keepdims=True)
    a = a * lax.rsqrt(ms + NORM_EPS) * subln_ref[...] * (1.0 - LAMBDA_INIT)
    o_ref[0] = a.T.astype(o_ref.dtype)


def _sb_attn_kernel(q_ref, k_ref, vt_ref, o_ref, c_ref, acc_ref):
    tq, tk, cw = ATT_TQ, ATT_TK, CHAIN_W
    n_chains = tq // cw
    n_sub = tk // cw
    i = pl.program_id(2)
    qt = q_ref[...]
    q_chain = [qt[:, c * cw:(c + 1) * cw] for c in range(n_chains)]

    krow = lax.broadcasted_iota(jnp.int32, (cw, cw), 0)
    qcol = lax.broadcasted_iota(jnp.int32, (cw, cw), 1)
    strict = krow < qcol
    later = jnp.where(qcol > krow, 1.0, 0.0).astype(jnp.bfloat16)

    c_ref[...] = jnp.zeros_like(c_ref)
    acc_ref[...] = jnp.zeros_like(acc_ref)

    def block(j, diagonal):
        pieces = [(sub, c) for sub in reversed(range(n_sub)) for c in range(n_chains)
                  if not (diagonal and sub > c)]
        kbs, vtbs = {}, {}
        for sub in range(n_sub):
            start = pl.multiple_of(j * tk + sub * cw, cw)
            kbs[sub] = k_ref[0, pl.ds(start, cw), :]
            vtbs[sub] = vt_ref[:, pl.ds(start, cw)]

        def scores(t, st, nxt):
            sub, c = pieces[t]
            st["z"] = jnp.dot(kbs[sub], q_chain[c],
                              preferred_element_type=jnp.float32)

        def suffix(t, st, nxt):
            sub, c = pieces[t]
            masked = diagonal and sub == c
            z = st.pop("z")
            sp = jnp.log2(1.0 + jnp.exp2(-jnp.abs(z)))
            u = jnp.maximum(z, 0.0) + sp
            st["log_sig"] = z - u
            if masked:
                u = jnp.where(strict, u, 0.0)
            st["tail"] = jnp.dot(later, u.astype(jnp.bfloat16),
                                 preferred_element_type=jnp.float32)
            st["usum"] = jnp.sum(u, axis=0, keepdims=True)

        def weights_pv(t, st, nxt):
            sub, c = pieces[t]
            masked = diagonal and sub == c
            lanes = slice(c * cw, (c + 1) * cw)
            carry = c_ref[:, lanes]
            if nxt is not None:
                carry = _ordered_after(carry, nxt["usum"])
            a = jnp.exp2(st.pop("log_sig") - st.pop("tail") - carry)
            if masked:
                a = jnp.where(strict, a, 0.0)
            c_ref[:, lanes] += st.pop("usum")
            st["pv"] = jnp.dot(vtbs[sub], a.astype(jnp.bfloat16),
                               preferred_element_type=jnp.float32)

        def accumulate(t, st, nxt):
            sub, c = pieces[t]
            lanes = slice(c * cw, (c + 1) * cw)
            acc_ref[:, lanes] += st.pop("pv")

        _emit_pipelined((scores, suffix, weights_pv, accumulate), len(pieces))

    block(i, diagonal=True)

    def body(jj, carry):
        block(i - 1 - jj, diagonal=False)
        return carry

    lax.fori_loop(0, i, body, 0)
    o_ref[0] = acc_ref[...].T.astype(o_ref.dtype)


def _mix_ffn2_kernel(h_ref, a_ref, b_ref, gate_ref, wa_ref, wb_ref, wout_ref,
                     norm2_ref, wg_ref, wu_ref, wd_ref, normf_ref, o_ref):
    ya = jnp.dot(a_ref[...], wa_ref[...], preferred_element_type=jnp.float32)
    yb = jnp.dot(b_ref[...], wb_ref[...], preferred_element_type=jnp.float32)
    gate = gate_ref[...].astype(jnp.float32)
    y = (gate[:, :D_MODEL] * ya + gate[:, D_MODEL:] * yb).astype(jnp.bfloat16)
    h2 = h_ref[...] + jnp.dot(y, wout_ref[...], preferred_element_type=jnp.float32)
    h3 = _swiglu_half_step(h2, norm2_ref[...], wg_ref, wu_ref, wd_ref)
    o_ref[...] = _rms(h3, normf_ref[...])


def _const_spec(shape):
    return pl.BlockSpec(shape, lambda *_: (0,) * len(shape))


ATTN_FLAGS = None


def _tc_params(n_axes, flags=None):
    return pltpu.CompilerParams(dimension_semantics=("arbitrary",) * n_axes,
                                vmem_limit_bytes=VMEM_LIMIT_BYTES, flags=flags)


def kernel(x, ffn1_norm, ffn1_w_gate, ffn1_w_up, ffn1_w_down, mix_norm, w_in, b_gate, lambda_q1, lambda_k1, lambda_q2, lambda_k2, diff_subln, w_branch_diff, w_branch_sb, w_out, ffn2_norm, ffn2_w_gate, ffn2_w_up, ffn2_w_down, final_norm):
    B, S, D = x.shape
    T = B * S
    f32, bf16 = jnp.float32, jnp.bfloat16
    xt = x.reshape(T, D)
    row = lambda v: v.reshape(1, -1).astype(f32)

    tok_spec = pl.BlockSpec((FFN_TM, D), lambda t: (t, 0))
    h1 = pl.pallas_call(
        _ffn1_kernel,
        grid=(T // FFN_TM,),
        in_specs=[tok_spec, _const_spec((1, D)), _const_spec((D, D_FF)),
                  _const_spec((D, D_FF)), _const_spec((D_FF, D))],
        out_specs=tok_spec,
        out_shape=jax.ShapeDtypeStruct((T, D), f32),
        compiler_params=_tc_params(1),
        name="ffn1",
    )(xt, row(ffn1_norm[0]), ffn1_w_gate[0].astype(bf16), ffn1_w_up[0].astype(bf16),
      ffn1_w_down[0].astype(bf16))

    w = w_in[0]
    w_k = jnp.concatenate([w[:, ATT_W:2 * ATT_W], w[:, 4 * ATT_W:5 * ATT_W]], axis=1).astype(bf16)
    w_qvt = jnp.concatenate([w[:, 0:ATT_W], w[:, 3 * ATT_W:4 * ATT_W],
                             w[:, 2 * ATT_W:3 * ATT_W], w[:, 5 * ATT_W:6 * ATT_W]],
                            axis=1).T.astype(bf16)
    w_gate = w[:, 6 * ATT_W:].astype(bf16)
    rowscale = jnp.ones((QVT_W,), f32)
    rowscale = rowscale.at[0:ATT_W].set(DA_QK_DIM ** -0.5 * LOG2E)
    rowscale = rowscale.at[ATT_W:2 * ATT_W].set(HEAD_W ** -0.5 * LOG2E)
    k, qvt, gates = pl.pallas_call(
        _in_proj_kernel,
        grid=(T // PROJ_TM,),
        in_specs=[pl.BlockSpec((PROJ_TM, D), lambda t: (t, 0)), _const_spec((1, D)),
                  _const_spec((D, K_W)), _const_spec((QVT_W, D)), _const_spec((D, GATE_W)),
                  _const_spec((QVT_W, 1)), _const_spec((1, GATE_W))],
        out_specs=[pl.BlockSpec((PROJ_TM, K_W), lambda t: (t, 0)),
                   pl.BlockSpec((QVT_W, PROJ_TM), lambda t: (0, t)),
                   pl.BlockSpec((PROJ_TM, GATE_W), lambda t: (t, 0))],
        out_shape=[jax.ShapeDtypeStruct((T, K_W), bf16),
                   jax.ShapeDtypeStruct((QVT_W, T), bf16),
                   jax.ShapeDtypeStruct((T, GATE_W), bf16)],
        compiler_params=_tc_params(1),
        name="in_proj",
    )(h1, row(mix_norm[0]), w_k, w_qvt, w_gate, rowscale.reshape(-1, 1), row(b_gate[0]))
    k3 = k.reshape(B, S, K_W)

    nq = S // ATT_TQ

    def q_spec(slab):
        return pl.BlockSpec((HEAD_W, ATT_TQ), lambda b, h, i: (slab * N_HEADS + h, b * nq + i))

    def k_spec(slab):
        return pl.BlockSpec((1, S, HEAD_W), lambda b, h, i: (b, 0, slab * N_HEADS + h))

    def vt_spec(slab):
        return pl.BlockSpec((HEAD_W, S), lambda b, h, i: (slab * N_HEADS + h, b))

    att_out_spec = pl.BlockSpec((1, ATT_TQ, HEAD_W), lambda b, h, i: (b, i, h))
    att_out_shape = jax.ShapeDtypeStruct((B, S, ATT_W), bf16)
    smem_spec = pl.BlockSpec(memory_space=pltpu.SMEM)

    lam = (jnp.exp(jnp.sum(lambda_q1[0].astype(f32) * lambda_k1[0].astype(f32)))
           - jnp.exp(jnp.sum(lambda_q2[0].astype(f32) * lambda_k2[0].astype(f32)))
           + LAMBDA_INIT).reshape(1)
    slopes = jnp.exp2(-8.0 * jnp.arange(1, N_HEADS + 1, dtype=f32) / N_HEADS)

    a = pl.pallas_call(
        _diff_attn_kernel,
        grid=(B, N_HEADS, nq),
        in_specs=[smem_spec, smem_spec, q_spec(0), k_spec(0), vt_spec(2),
                  _const_spec((HEAD_W, 1))],
        out_specs=att_out_spec,
        out_shape=att_out_shape,
        scratch_shapes=[pltpu.VMEM((ATT_TK, ATT_TQ), f32), pltpu.VMEM((ATT_TK, ATT_TQ), f32),
                        pltpu.VMEM((1, 2 * ATT_TQ), f32), pltpu.VMEM((1, 2 * ATT_TQ), f32),
                        pltpu.VMEM((HEAD_W, 2 * ATT_TQ), f32)],
        compiler_params=_tc_params(3, ATTN_FLAGS),
        name="diff_attn",
    )(slopes, lam, qvt, k3, qvt, diff_subln[0].reshape(-1, 1).astype(f32))

    b = pl.pallas_call(
        _sb_attn_kernel,
        grid=(B, N_HEADS, nq),
        in_specs=[q_spec(1), k_spec(1), vt_spec(3)],
        out_specs=att_out_spec,
        out_shape=att_out_shape,
        scratch_shapes=[pltpu.VMEM((1, ATT_TQ), f32), pltpu.VMEM((HEAD_W, ATT_TQ), f32)],
        compiler_params=_tc_params(3, ATTN_FLAGS),
        name="sb_attn",
    )(qvt, k3, qvt)

    out = pl.pallas_call(
        _mix_ffn2_kernel,
        grid=(T // FFN_TM,),
        in_specs=[tok_spec,
                  pl.BlockSpec((FFN_TM, ATT_W), lambda t: (t, 0)),
                  pl.BlockSpec((FFN_TM, ATT_W), lambda t: (t, 0)),
                  pl.BlockSpec((FFN_TM, GATE_W), lambda t: (t, 0)),
                  _const_spec((ATT_W, D)), _const_spec((ATT_W, D)),
                  _const_spec((D, D)), _const_spec((1, D)), _const_spec((D, D_FF)),
                  _const_spec((D, D_FF)), _const_spec((D_FF, D)), _const_spec((1, D))],
        out_specs=tok_spec,
        out_shape=jax.ShapeDtypeStruct((T, D), f32),
        compiler_params=_tc_params(1),
        name="mix_ffn2",
    )(h1, a.reshape(T, -1), b.reshape(T, -1), gates,
      w_branch_diff[0].astype(bf16), w_branch_sb[0].astype(bf16), w_out[0].astype(bf16),
      row(ffn2_norm[0]), ffn2_w_gate[0].astype(bf16), ffn2_w_up[0].astype(bf16),
      ffn2_w_down[0].astype(bf16), row(final_norm))
    return out.reshape(B, S, D)
```

```python
import math

import jax
import jax.numpy as jnp
from jax import lax
from jax.experimental import pallas as pl
from jax.experimental.pallas import tpu as pltpu

D_MODEL = 1024
D_FF = 2816
N_HEADS = 4
HEAD_W = 128
DA_QK_DIM = 64
ATT_W = N_HEADS * HEAD_W
K_W = 2 * ATT_W
QVT_W = 4 * ATT_W
GATE_W = 2 * D_MODEL
NORM_EPS = 1e-5
LAMBDA_INIT = 0.8 - 0.6 * math.exp(-0.3 * 0)
LOG2E = 1.0 / math.log(2.0)

VMEM_LIMIT_BYTES = 56 * 1024 * 1024

FFN_TM = 256
PROJ_TM = 512
ATT_TQ = 512
ATT_TK = 512
CHAIN_W = 256
SB_DONE_LOG2 = 160.0

_NT = (((1,), (1,)), ((), ()))


def _rms(x, g):
    ms = jnp.mean(x * x, axis=-1, keepdims=True)
    return x * lax.rsqrt(ms + NORM_EPS) * g


def _swiglu_half_step(x, norm_g, wg_ref, wu_ref, wd_ref):
    xn = _rms(x, norm_g).astype(jnp.bfloat16)
    g = jnp.dot(xn, wg_ref[...], preferred_element_type=jnp.float32)
    u = jnp.dot(xn, wu_ref[...], preferred_element_type=jnp.float32)
    hact = (g * jax.nn.sigmoid(g) * u).astype(jnp.bfloat16)
    return x + 0.5 * jnp.dot(hact, wd_ref[...], preferred_element_type=jnp.float32)


def _ffn1_kernel(x_ref, norm_ref, wg_ref, wu_ref, wd_ref, o_ref):
    o_ref[...] = _swiglu_half_step(x_ref[...], norm_ref[...], wg_ref, wu_ref, wd_ref)


def _in_proj_kernel(h_ref, norm_ref, wk_ref, wqvt_ref, wgate_ref, rowscale_ref, bgate_ref,
                    k_ref, qvt_ref, gate_ref):
    n = _rms(h_ref[...], norm_ref[...]).astype(jnp.bfloat16)
    k_ref[...] = jnp.dot(n, wk_ref[...], preferred_element_type=jnp.float32).astype(jnp.bfloat16)
    qvt = lax.dot_general(wqvt_ref[...], n, _NT, preferred_element_type=jnp.float32)
    qvt_ref[...] = (qvt * rowscale_ref[...]).astype(jnp.bfloat16)
    g = jnp.dot(n, wgate_ref[...], preferred_element_type=jnp.float32)
    gate_ref[...] = jax.nn.sigmoid(g + bgate_ref[...]).astype(jnp.bfloat16)


def _emit_pipelined(stages, n):
    state = [dict() for _ in range(n)]
    for step in range(n + len(stages) - 1):
        for s, stage in enumerate(stages):
            t = step - s
            if 0 <= t < n:
                stage(t, state[t], state[t + 1] if t + 1 < n else None)


def _ordered_after(x, token):
    zero = lax.shift_right_logical(
        lax.shift_right_logical(pltpu.bitcast(token, jnp.uint32), jnp.uint32(16)), jnp.uint32(16))
    return pltpu.bitcast(pltpu.bitcast(x, jnp.uint32) + zero, jnp.float32)


def _diff_attn_kernel(slope_ref, lam_ref, q_ref, k_ref, vt_ref, subln_ref, o_ref,
                      bias_ref, bias_diag_ref, m_ref, l_ref, acc_ref):
    tq, tk, cw = ATT_TQ, ATT_TK, CHAIN_W
    per_map = tq // cw
    n_chains = 2 * per_map
    h = pl.program_id(1)
    i = pl.program_id(2)
    slope = slope_ref[h] * LOG2E
    lam = lam_ref[0]

    qt = q_ref[...]
    chan = lax.broadcasted_iota(jnp.int32, (HEAD_W, tq), 0)
    zero = jnp.zeros_like(qt)
    q_maps = (jnp.where(chan < DA_QK_DIM, qt, zero), jnp.where(chan >= DA_QK_DIM, qt, zero))
    q_chain = [q_maps[c // per_map][:, (c % per_map) * cw:(c % per_map + 1) * cw]
               for c in range(n_chains)]

    krow = lax.broadcasted_iota(jnp.int32, (tk, tq), 0)
    qcol = lax.broadcasted_iota(jnp.int32, (tk, tq), 1)
    rel = qcol - krow
    bias = -slope * rel.astype(jnp.float32)
    bias_ref[...] = bias
    bias_diag_ref[...] = jnp.where(rel >= 0, bias, -jnp.inf)

    m_ref[...] = jnp.full_like(m_ref, -jnp.inf)
    l_ref[...] = jnp.zeros_like(l_ref)
    acc_ref[...] = jnp.zeros_like(acc_ref)

    def block(j, b_ref):
        kb = k_ref[0, pl.ds(pl.multiple_of(j * tk, tk), tk), :]
        vtb = vt_ref[:, pl.ds(pl.multiple_of(j * tk, tk), tk)]
        shift = -slope * ((i - j) * tq).astype(jnp.float32)

        def scores(c, st, nxt):
            st["s"] = jnp.dot(kb, q_chain[c], preferred_element_type=jnp.float32)

        def column_max(c, st, nxt):
            qp = c % per_map
            st["sb"] = st.pop("s") + b_ref[:, qp * cw:(qp + 1) * cw]
            st["cmax"] = jnp.max(st["sb"], axis=0, keepdims=True) + shift

        def softmax_pv(c, st, nxt):
            lanes = slice(c * cw, (c + 1) * cw)
            m_prev = m_ref[:, lanes]
            m_new = jnp.maximum(m_prev, st.pop("cmax"))
            if nxt is not None:
                m_new = _ordered_after(m_new, nxt["cmax"])
            st["alpha"] = jnp.exp2(m_prev - m_new)
            p = jnp.exp2(st.pop("sb") - (m_new - shift))
            l_ref[:, lanes] = st["alpha"] * l_ref[:, lanes] + jnp.sum(p, axis=0, keepdims=True)
            m_ref[:, lanes] = m_new
            st["pv"] = jnp.dot(vtb, p.astype(jnp.bfloat16),
                               preferred_element_type=jnp.float32)

        def accumulate(c, st, nxt):
            lanes = slice(c * cw, (c + 1) * cw)
            acc_ref[:, lanes] = st.pop("alpha") * acc_ref[:, lanes] + st.pop("pv")

        _emit_pipelined((scores, column_max, softmax_pv, accumulate), n_chains)

    def body(j, carry):
        block(j, bias_ref)
        return carry

    lax.fori_loop(0, i, body, 0)
    block(i, bias_diag_ref)

    o = acc_ref[...] / l_ref[...]
    a = o[:, :tq] - lam * o[:, tq:]
    ms = jnp.mean(a * a, axis=0, keepdims=True)
    a = a * lax.rsqrt(ms + NORM_EPS) * subln_ref[...] * (1.0 - LAMBDA_INIT)
    o_ref[0] = a.T.astype(o_ref.dtype)


def _sb_attn_kernel(q_ref, k_ref, vt_ref, o_ref, c_ref, acc_ref):
    tq, tk, cw = ATT_TQ, ATT_TK, CHAIN_W
    n_chains = tq // cw
    n_sub = tk // cw
    i = pl.program_id(2)
    qt = q_ref[...]
    q_chain = [qt[:, c * cw:(c + 1) * cw] for c in range(n_chains)]

    krow = lax.broadcasted_iota(jnp.int32, (cw, cw), 0)
    qcol = lax.broadcasted_iota(jnp.int32, (cw, cw), 1)
    strict = krow < qcol
    later = jnp.where(qcol > krow, 1.0, 0.0).astype(jnp.bfloat16)

    c_ref[...] = jnp.zeros_like(c_ref)
    acc_ref[...] = jnp.zeros_like(acc_ref)

    def block(j, diagonal):
        pieces = [(sub, c) for sub in reversed(range(n_sub)) for c in range(n_chains)
                  if not (diagonal and sub > c)]
        kbs, vtbs = {}, {}
        for sub in range(n_sub):
            start = pl.multiple_of(j * tk + sub * cw, cw)
            kbs[sub] = k_ref[0, pl.ds(start, cw), :]
            vtbs[sub] = vt_ref[:, pl.ds(start, cw)]

        def scores(t, st, nxt):
            sub, c = pieces[t]
            st["z"] = jnp.dot(kbs[sub], q_chain[c],
                              preferred_element_type=jnp.float32)

        def suffix(t, st, nxt):
            sub, c = pieces[t]
            masked = diagonal and sub == c
            z = st.pop("z")
            sp = jnp.log2(1.0 + jnp.exp2(-jnp.abs(z)))
            u = jnp.maximum(z, 0.0) + sp
            st["log_sig"] = z - u
            if masked:
                u = jnp.where(strict, u, 0.0)
            st["tail"] = jnp.dot(later, u.astype(jnp.bfloat16),
                                 preferred_element_type=jnp.float32)
            st["usum"] = jnp.sum(u, axis=0, keepdims=True)

        def weights_pv(t, st, nxt):
            sub, c = pieces[t]
            masked = diagonal and sub == c
            lanes = slice(c * cw, (c + 1) * cw)
            carry = c_ref[:, lanes]
            if nxt is not None:
                carry = _ordered_after(carry, nxt["usum"])
            a = jnp.exp2(st.pop("log_sig") - st.pop("tail") - carry)
            if masked:
                a = jnp.where(strict, a, 0.0)
            c_ref[:, lanes] += st.pop("usum")
            st["pv"] = jnp.dot(vtbs[sub], a.astype(jnp.bfloat16),
                               preferred_element_type=jnp.float32)

        def accumulate(t, st, nxt):
            sub, c = pieces[t]
            lanes = slice(c * cw, (c + 1) * cw)
            acc_ref[:, lanes] += st.pop("pv")

        _emit_pipelined((scores, suffix, weights_pv, accumulate), len(pieces))

    block(i, diagonal=True)

    def unfinished():
        return jnp.min(c_ref[...]) < SB_DONE_LOG2

    def cond(carry):
        jj, go = carry
        return jnp.logical_and(jj < i, go)

    def body(carry):
        jj, _ = carry
        block(i - 1 - jj, diagonal=False)
        return jj + 1, unfinished()

    lax.while_loop(cond, body, (jnp.int32(0), unfinished()))
    o_ref[0] = acc_ref[...].T.astype(o_ref.dtype)


def _mix_ffn2_kernel(h_ref, a_ref, b_ref, gate_ref, wa_ref, wb_ref, wout_ref,
                     norm2_ref, wg_ref, wu_ref, wd_ref, normf_ref, o_ref):
    ya = jnp.dot(a_ref[...], wa_ref[...], preferred_element_type=jnp.float32)
    yb = jnp.dot(b_ref[...], wb_ref[...], preferred_element_type=jnp.float32)
    gate = gate_ref[...].astype(jnp.float32)
    y = (gate[:, :D_MODEL] * ya + gate[:, D_MODEL:] * yb).astype(jnp.bfloat16)
    h2 = h_ref[...] + jnp.dot(y, wout_ref[...], preferred_element_type=jnp.float32)
    h3 = _swiglu_half_step(h2, norm2_ref[...], wg_ref, wu_ref, wd_ref)
    o_ref[...] = _rms(h3, normf_ref[...])


def _const_spec(shape):
    return pl.BlockSpec(shape, lambda *_: (0,) * len(shape))


ATTN_FLAGS = None


def _tc_params(n_axes, flags=None):
    return pltpu.CompilerParams(dimension_semantics=("arbitrary",) * n_axes,
                                vmem_limit_bytes=VMEM_LIMIT_BYTES, flags=flags)


def kernel(x, ffn1_norm, ffn1_w_gate, ffn1_w_up, ffn1_w_down, mix_norm, w_in, b_gate, lambda_q1, lambda_k1, lambda_q2, lambda_k2, diff_subln, w_branch_diff, w_branch_sb, w_out, ffn2_norm, ffn2_w_gate, ffn2_w_up, ffn2_w_down, final_norm):
    B, S, D = x.shape
    T = B * S
    f32, bf16 = jnp.float32, jnp.bfloat16
    xt = x.reshape(T, D)
    row = lambda v: v.reshape(1, -1).astype(f32)

    tok_spec = pl.BlockSpec((FFN_TM, D), lambda t: (t, 0))
    h1 = pl.pallas_call(
        _ffn1_kernel,
        grid=(T // FFN_TM,),
        in_specs=[tok_spec, _const_spec((1, D)), _const_spec((D, D_FF)),
                  _const_spec((D, D_FF)), _const_spec((D_FF, D))],
        out_specs=tok_spec,
        out_shape=jax.ShapeDtypeStruct((T, D), f32),
        compiler_params=_tc_params(1),
        name="ffn1",
    )(xt, row(ffn1_norm[0]), ffn1_w_gate[0].astype(bf16), ffn1_w_up[0].astype(bf16),
      ffn1_w_down[0].astype(bf16))

    w = w_in[0]
    w_k = jnp.concatenate([w[:, ATT_W:2 * ATT_W], w[:, 4 * ATT_W:5 * ATT_W]], axis=1).astype(bf16)
    w_qvt = jnp.concatenate([w[:, 0:ATT_W], w[:, 3 * ATT_W:4 * ATT_W],
                             w[:, 2 * ATT_W:3 * ATT_W], w[:, 5 * ATT_W:6 * ATT_W]],
                            axis=1).T.astype(bf16)
    w_gate = w[:, 6 * ATT_W:].astype(bf16)
    rowscale = jnp.ones((QVT_W,), f32)
    rowscale = rowscale.at[0:ATT_W].set(DA_QK_DIM ** -0.5 * LOG2E)
    rowscale = rowscale.at[ATT_W:2 * ATT_W].set(HEAD_W ** -0.5 * LOG2E)
    k, qvt, gates = pl.pallas_call(
        _in_proj_kernel,
        grid=(T // PROJ_TM,),
        in_specs=[pl.BlockSpec((PROJ_TM, D), lambda t: (t, 0)), _const_spec((1, D)),
                  _const_spec((D, K_W)), _const_spec((QVT_W, D)), _const_spec((D, GATE_W)),
                  _const_spec((QVT_W, 1)), _const_spec((1, GATE_W))],
        out_specs=[pl.BlockSpec((PROJ_TM, K_W), lambda t: (t, 0)),
                   pl.BlockSpec((QVT_W, PROJ_TM), lambda t: (0, t)),
                   pl.BlockSpec((PROJ_TM, GATE_W), lambda t: (t, 0))],
        out_shape=[jax.ShapeDtypeStruct((T, K_W), bf16),
                   jax.ShapeDtypeStruct((QVT_W, T), bf16),
                   jax.ShapeDtypeStruct((T, GATE_W), bf16)],
        compiler_params=_tc_params(1),
        name="in_proj",
    )(h1, row(mix_norm[0]), w_k, w_qvt, w_gate, rowscale.reshape(-1, 1), row(b_gate[0]))
    k3 = k.reshape(B, S, K_W)

    nq = S // ATT_TQ

    def q_spec(slab):
        return pl.BlockSpec((HEAD_W, ATT_TQ), lambda b, h, i: (slab * N_HEADS + h, b * nq + i))

    def k_spec(slab):
        return pl.BlockSpec((1, S, HEAD_W), lambda b, h, i: (b, 0, slab * N_HEADS + h))

    def vt_spec(slab):
        return pl.BlockSpec((HEAD_W, S), lambda b, h, i: (slab * N_HEADS + h, b))

    att_out_spec = pl.BlockSpec((1, ATT_TQ, HEAD_W), lambda b, h, i: (b, i, h))
    att_out_shape = jax.ShapeDtypeStruct((B, S, ATT_W), bf16)
    smem_spec = pl.BlockSpec(memory_space=pltpu.SMEM)

    lam = (jnp.exp(jnp.sum(lambda_q1[0].astype(f32) * lambda_k1[0].astype(f32)))
           - jnp.exp(jnp.sum(lambda_q2[0].astype(f32) * lambda_k2[0].astype(f32)))
           + LAMBDA_INIT).reshape(1)
    slopes = jnp.exp2(-8.0 * jnp.arange(1, N_HEADS + 1, dtype=f32) / N_HEADS)

    a = pl.pallas_call(
        _diff_attn_kernel,
        grid=(B, N_HEADS, nq),
        in_specs=[smem_spec, smem_spec, q_spec(0), k_spec(0), vt_spec(2),
                  _const_spec((HEAD_W, 1))],
        out_specs=att_out_spec,
        out_shape=att_out_shape,
        scratch_shapes=[pltpu.VMEM((ATT_TK, ATT_TQ), f32), pltpu.VMEM((ATT_TK, ATT_TQ), f32),
                        pltpu.VMEM((1, 2 * ATT_TQ), f32), pltpu.VMEM((1, 2 * ATT_TQ), f32),
                        pltpu.VMEM((HEAD_W, 2 * ATT_TQ), f32)],
        compiler_params=_tc_params(3, ATTN_FLAGS),
        name="diff_attn",
    )(slopes, lam, qvt, k3, qvt, diff_subln[0].reshape(-1, 1).astype(f32))

    b = pl.pallas_call(
        _sb_attn_kernel,
        grid=(B, N_HEADS, nq),
        in_specs=[q_spec(1), k_spec(1), vt_spec(3)],
        out_specs=att_out_spec,
        out_shape=att_out_shape,
        scratch_shapes=[pltpu.VMEM((1, ATT_TQ), f32), pltpu.VMEM((HEAD_W, ATT_TQ), f32)],
        compiler_params=_tc_params(3, ATTN_FLAGS),
        name="sb_attn",
    )(qvt, k3, qvt)

    out = pl.pallas_call(
        _mix_ffn2_kernel,
        grid=(T // FFN_TM,),
        in_specs=[tok_spec,
                  pl.BlockSpec((FFN_TM, ATT_W), lambda t: (t, 0)),
                  pl.BlockSpec((FFN_TM, ATT_W), lambda t: (t, 0)),
                  pl.BlockSpec((FFN_TM, GATE_W), lambda t: (t, 0)),
                  _const_spec((ATT_W, D)), _const_spec((ATT_W, D)),
                  _const_spec((D, D)), _const_spec((1, D)), _const_spec((D, D_FF)),
                  _const_spec((D, D_FF)), _const_spec((D_FF, D)), _const_spec((1, D))],
        out_specs=tok_spec,
        out_shape=jax.ShapeDtypeStruct((T, D), f32),
        compiler_params=_tc_params(1),
        name="mix_ffn2",
    )(h1, a.reshape(T, -1), b.reshape(T, -1), gates,
      w_branch_diff[0].astype(bf16), w_branch_sb[0].astype(bf16), w_out[0].astype(bf16),
      row(ffn2_norm[0]), ffn2_w_gate[0].astype(bf16), ffn2_w_up[0].astype(bf16),
      ffn2_w_down[0].astype(bf16), row(final_norm))
    return out.reshape(B, S, D)
```

```python
import math

import jax
import jax.numpy as jnp
from jax import lax
from jax.experimental import pallas as pl
from jax.experimental.pallas import tpu as pltpu

D_MODEL = 1024
D_FF = 2816
N_HEADS = 4
HEAD_W = 128
DA_QK_DIM = 64
ATT_W = N_HEADS * HEAD_W
K_W = 2 * ATT_W
QVT_W = 4 * ATT_W
GATE_W = 2 * D_MODEL
NORM_EPS = 1e-5
LAMBDA_INIT = 0.8 - 0.6 * math.exp(-0.3 * 0)
LOG2E = 1.0 / math.log(2.0)

VMEM_LIMIT_BYTES = 56 * 1024 * 1024

FFN_TM = 256
PROJ_TM = 512
ATT_TQ = 512
ATT_TK = 512
CHAIN_W = 256
SB_DONE_LOG2 = 160.0

_NT = (((1,), (1,)), ((), ()))


def _rms(x, g):
    ms = jnp.mean(x * x, axis=-1, keepdims=True)
    return x * lax.rsqrt(ms + NORM_EPS) * g


def _swiglu_half_step(x, norm_g, wg_ref, wu_ref, wd_ref):
    xn = _rms(x, norm_g).astype(jnp.bfloat16)
    g = jnp.dot(xn, wg_ref[...], preferred_element_type=jnp.float32)
    u = jnp.dot(xn, wu_ref[...], preferred_element_type=jnp.float32)
    hact = (g * jax.nn.sigmoid(g) * u).astype(jnp.bfloat16)
    return x + 0.5 * jnp.dot(hact, wd_ref[...], preferred_element_type=jnp.float32)


def _ffn1_kernel(x_ref, norm_ref, wg_ref, wu_ref, wd_ref, o_ref):
    o_ref[...] = _swiglu_half_step(x_ref[...], norm_ref[...], wg_ref, wu_ref, wd_ref)


def _in_proj_kernel(h_ref, norm_ref, wk_ref, wqvt_ref, wgate_ref, rowscale_ref, bgate_ref,
                    k_ref, qvt_ref, gate_ref):
    n = _rms(h_ref[...], norm_ref[...]).astype(jnp.bfloat16)
    k_ref[...] = jnp.dot(n, wk_ref[...], preferred_element_type=jnp.float32).astype(jnp.bfloat16)
    qvt = lax.dot_general(wqvt_ref[...], n, _NT, preferred_element_type=jnp.float32)
    qvt_ref[...] = (qvt * rowscale_ref[...]).astype(jnp.bfloat16)
    g = jnp.dot(n, wgate_ref[...], preferred_element_type=jnp.float32)
    gate_ref[...] = jax.nn.sigmoid(g + bgate_ref[...]).astype(jnp.bfloat16)


def _emit_pipelined(stages, n):
    state = [dict() for _ in range(n)]
    for step in range(n + len(stages) - 1):
        for s, stage in enumerate(stages):
            t = step - s
            if 0 <= t < n:
                stage(t, state[t], state[t + 1] if t + 1 < n else None)


def _ordered_after(x, token):
    zero = lax.shift_right_logical(
        lax.shift_right_logical(pltpu.bitcast(token, jnp.uint32), jnp.uint32(16)), jnp.uint32(16))
    return pltpu.bitcast(pltpu.bitcast(x, jnp.uint32) + zero, jnp.float32)


def _diff_attn_kernel(slope_ref, lam_ref, q_ref, k_ref, vt_ref, subln_ref, o_ref,
                      bias_ref, bias_diag_ref, s_ref, m_ref, l_ref, acc_ref):
    tq, tk, cw = ATT_TQ, ATT_TK, CHAIN_W
    per_map = tq // cw
    n_chains = 2 * per_map
    h = pl.program_id(1)
    i = pl.program_id(2)
    slope = slope_ref[h] * LOG2E
    lam = lam_ref[0]

    qt = q_ref[...]
    chan = lax.broadcasted_iota(jnp.int32, (HEAD_W, tq), 0)
    zero = jnp.zeros_like(qt)
    q_maps = (jnp.where(chan < DA_QK_DIM, qt, zero), jnp.where(chan >= DA_QK_DIM, qt, zero))
    q_chain = [q_maps[c // per_map][:, (c % per_map) * cw:(c % per_map + 1) * cw]
               for c in range(n_chains)]

    krow = lax.broadcasted_iota(jnp.int32, (tk, tq), 0)
    qcol = lax.broadcasted_iota(jnp.int32, (tk, tq), 1)
    rel = qcol - krow
    bias = -slope * rel.astype(jnp.float32)
    bias_ref[...] = bias
    bias_diag_ref[...] = jnp.where(rel >= 0, bias, -jnp.inf)

    m_ref[...] = jnp.full_like(m_ref, -jnp.inf)
    l_ref[...] = jnp.zeros_like(l_ref)
    acc_ref[...] = jnp.zeros_like(acc_ref)

    def scores_to(slot, j):
        kb = k_ref[0, pl.ds(pl.multiple_of(j * tk, tk), tk), :]
        for c in range(n_chains):
            s_ref[slot, c] = jnp.dot(kb, q_chain[c], preferred_element_type=jnp.float32)

    def consume(slot, j, b_ref):
        vtb = vt_ref[:, pl.ds(pl.multiple_of(j * tk, tk), tk)]
        shift = -slope * ((i - j) * tq).astype(jnp.float32)

        def column_max(c, st, nxt):
            qp = c % per_map
            st["sb"] = s_ref[slot, c] + b_ref[:, qp * cw:(qp + 1) * cw]
            st["cmax"] = jnp.max(st["sb"], axis=0, keepdims=True) + shift

        def softmax_pv(c, st, nxt):
            lanes = slice(c * cw, (c + 1) * cw)
            m_prev = m_ref[:, lanes]
            m_new = jnp.maximum(m_prev, st.pop("cmax"))
            st["alpha"] = jnp.exp2(m_prev - m_new)
            p = jnp.exp2(st.pop("sb") - (m_new - shift))
            l_ref[:, lanes] = st["alpha"] * l_ref[:, lanes] + jnp.sum(p, axis=0, keepdims=True)
            m_ref[:, lanes] = m_new
            st["pv"] = jnp.dot(vtb, p.astype(jnp.bfloat16),
                               preferred_element_type=jnp.float32)

        def accumulate(c, st, nxt):
            lanes = slice(c * cw, (c + 1) * cw)
            acc_ref[:, lanes] = st.pop("alpha") * acc_ref[:, lanes] + st.pop("pv")

        _emit_pipelined((column_max, softmax_pv, accumulate), n_chains)

    def step(slot, j):
        scores_to(1 - slot, j + 1)
        consume(slot, j, bias_ref)

    scores_to(0, 0)

    def pair(jj, carry):
        step(0, 2 * jj)
        step(1, 2 * jj + 1)
        return carry

    lax.fori_loop(0, i // 2, pair, 0)

    @pl.when(i % 2 == 0)
    def _():
        consume(0, i, bias_diag_ref)

    @pl.when(i % 2 == 1)
    def _():
        step(0, i - 1)
        consume(1, i, bias_diag_ref)

    o = acc_ref[...] / l_ref[...]
    a = o[:, :tq] - lam * o[:, tq:]
    ms = jnp.mean(a * a, axis=0, keepdims=True)
    a = a * lax.rsqrt(ms + NORM_EPS) * subln_ref[...] * (1.0 - LAMBDA_INIT)
    o_ref[0] = a.T.astype(o_ref.dtype)


def _sb_attn_kernel(q_ref, k_ref, vt_ref, o_ref, c_ref, acc_ref):
    tq, tk, cw = ATT_TQ, ATT_TK, CHAIN_W
    n_chains = tq // cw
    n_sub = tk // cw
    i = pl.program_id(2)
    qt = q_ref[...]
    q_chain = [qt[:, c * cw:(c + 1) * cw] for c in range(n_chains)]

    krow = lax.broadcasted_iota(jnp.int32, (cw, cw), 0)
    qcol = lax.broadcasted_iota(jnp.int32, (cw, cw), 1)
    strict = krow < qcol
    later = jnp.where(qcol > krow, 1.0, 0.0).astype(jnp.bfloat16)

    c_ref[...] = jnp.zeros_like(c_ref)
    acc_ref[...] = jnp.zeros_like(acc_ref)

    def block(j, diagonal):
        pieces = [(sub, c) for sub in reversed(range(n_sub)) for c in range(n_chains)
                  if not (diagonal and sub > c)]
        kbs, vtbs = {}, {}
        for sub in range(n_sub):
            start = pl.multiple_of(j * tk + sub * cw, cw)
            kbs[sub] = k_ref[0, pl.ds(start, cw), :]
            vtbs[sub] = vt_ref[:, pl.ds(start, cw)]

        def scores(t, st, nxt):
            sub, c = pieces[t]
            st["z"] = jnp.dot(kbs[sub], q_chain[c],
                              preferred_element_type=jnp.float32)

        def suffix(t, st, nxt):
            sub, c = pieces[t]
            masked = diagonal and sub == c
            z = st.pop("z")
            sp = jnp.log2(1.0 + jnp.exp2(-jnp.abs(z)))
            u = jnp.maximum(z, 0.0) + sp
            st["log_sig"] = z - u
            if masked:
                u = jnp.where(strict, u, 0.0)
            st["tail"] = jnp.dot(later, u.astype(jnp.bfloat16),
                                 preferred_element_type=jnp.float32)
            st["usum"] = jnp.sum(u, axis=0, keepdims=True)

        def weights_pv(t, st, nxt):
            sub, c = pieces[t]
            masked = diagonal and sub == c
            lanes = slice(c * cw, (c + 1) * cw)
            carry = c_ref[:, lanes]
            if nxt is not None:
                carry = _ordered_after(carry, nxt["usum"])
            a = jnp.exp2(st.pop("log_sig") - st.pop("tail") - carry)
            if masked:
                a = jnp.where(strict, a, 0.0)
            c_ref[:, lanes] += st.pop("usum")
            st["pv"] = jnp.dot(vtbs[sub], a.astype(jnp.bfloat16),
                               preferred_element_type=jnp.float32)

        def accumulate(t, st, nxt):
            sub, c = pieces[t]
            lanes = slice(c * cw, (c + 1) * cw)
            acc_ref[:, lanes] += st.pop("pv")

        _emit_pipelined((scores, suffix, weights_pv, accumulate), len(pieces))

    block(i, diagonal=True)

    def unfinished():
        return jnp.min(c_ref[...]) < SB_DONE_LOG2

    def cond(carry):
        jj, go = carry
        return jnp.logical_and(jj < i, go)

    def body(carry):
        jj, _ = carry
        block(i - 1 - jj, diagonal=False)
        return jj + 1, unfinished()

    lax.while_loop(cond, body, (jnp.int32(0), unfinished()))
    o_ref[0] = acc_ref[...].T.astype(o_ref.dtype)


def _mix_ffn2_kernel(h_ref, a_ref, b_ref, gate_ref, wa_ref, wb_ref, wout_ref,
                     norm2_ref, wg_ref, wu_ref, wd_ref, normf_ref, o_ref):
    ya = jnp.dot(a_ref[...], wa_ref[...], preferred_element_type=jnp.float32)
    yb = jnp.dot(b_ref[...], wb_ref[...], preferred_element_type=jnp.float32)
    gate = gate_ref[...].astype(jnp.float32)
    y = (gate[:, :D_MODEL] * ya + gate[:, D_MODEL:] * yb).astype(jnp.bfloat16)
    h2 = h_ref[...] + jnp.dot(y, wout_ref[...], preferred_element_type=jnp.float32)
    h3 = _swiglu_half_step(h2, norm2_ref[...], wg_ref, wu_ref, wd_ref)
    o_ref[...] = _rms(h3, normf_ref[...])


def _const_spec(shape):
    return pl.BlockSpec(shape, lambda *_: (0,) * len(shape))


ATTN_FLAGS = None


def _tc_params(n_axes, flags=None):
    return pltpu.CompilerParams(dimension_semantics=("arbitrary",) * n_axes,
                                vmem_limit_bytes=VMEM_LIMIT_BYTES, flags=flags)


def kernel(x, ffn1_norm, ffn1_w_gate, ffn1_w_up, ffn1_w_down, mix_norm, w_in, b_gate, lambda_q1, lambda_k1, lambda_q2, lambda_k2, diff_subln, w_branch_diff, w_branch_sb, w_out, ffn2_norm, ffn2_w_gate, ffn2_w_up, ffn2_w_down, final_norm):
    B, S, D = x.shape
    T = B * S
    f32, bf16 = jnp.float32, jnp.bfloat16
    xt = x.reshape(T, D)
    row = lambda v: v.reshape(1, -1).astype(f32)

    tok_spec = pl.BlockSpec((FFN_TM, D), lambda t: (t, 0))
    h1 = pl.pallas_call(
        _ffn1_kernel,
        grid=(T // FFN_TM,),
        in_specs=[tok_spec, _const_spec((1, D)), _const_spec((D, D_FF)),
                  _const_spec((D, D_FF)), _const_spec((D_FF, D))],
        out_specs=tok_spec,
        out_shape=jax.ShapeDtypeStruct((T, D), f32),
        compiler_params=_tc_params(1),
        name="ffn1",
    )(xt, row(ffn1_norm[0]), ffn1_w_gate[0].astype(bf16), ffn1_w_up[0].astype(bf16),
      ffn1_w_down[0].astype(bf16))

    w = w_in[0]
    w_k = jnp.concatenate([w[:, ATT_W:2 * ATT_W], w[:, 4 * ATT_W:5 * ATT_W]], axis=1).astype(bf16)
    w_qvt = jnp.concatenate([w[:, 0:ATT_W], w[:, 3 * ATT_W:4 * ATT_W],
                             w[:, 2 * ATT_W:3 * ATT_W], w[:, 5 * ATT_W:6 * ATT_W]],
                            axis=1).T.astype(bf16)
    w_gate = w[:, 6 * ATT_W:].astype(bf16)
    rowscale = jnp.ones((QVT_W,), f32)
    rowscale = rowscale.at[0:ATT_W].set(DA_QK_DIM ** -0.5 * LOG2E)
    rowscale = rowscale.at[ATT_W:2 * ATT_W].set(HEAD_W ** -0.5 * LOG2E)
    k, qvt, gates = pl.pallas_call(
        _in_proj_kernel,
        grid=(T // PROJ_TM,),
        in_specs=[pl.BlockSpec((PROJ_TM, D), lambda t: (t, 0)), _const_spec((1, D)),
                  _const_spec((D, K_W)), _const_spec((QVT_W, D)), _const_spec((D, GATE_W)),
                  _const_spec((QVT_W, 1)), _const_spec((1, GATE_W))],
        out_specs=[pl.BlockSpec((PROJ_TM, K_W), lambda t: (t, 0)),
                   pl.BlockSpec((QVT_W, PROJ_TM), lambda t: (0, t)),
                   pl.BlockSpec((PROJ_TM, GATE_W), lambda t: (t, 0))],
        out_shape=[jax.ShapeDtypeStruct((T, K_W), bf16),
                   jax.ShapeDtypeStruct((QVT_W, T), bf16),
                   jax.ShapeDtypeStruct((T, GATE_W), bf16)],
        compiler_params=_tc_params(1),
        name="in_proj",
    )(h1, row(mix_norm[0]), w_k, w_qvt, w_gate, rowscale.reshape(-1, 1), row(b_gate[0]))
    k3 = k.reshape(B, S, K_W)

    nq = S // ATT_TQ

    def q_spec(slab):
        return pl.BlockSpec((HEAD_W, ATT_TQ), lambda b, h, i: (slab * N_HEADS + h, b * nq + i))

    def k_spec(slab):
        return pl.BlockSpec((1, S, HEAD_W), lambda b, h, i: (b, 0, slab * N_HEADS + h))

    def vt_spec(slab):
        return pl.BlockSpec((HEAD_W, S), lambda b, h, i: (slab * N_HEADS + h, b))

    att_out_spec = pl.BlockSpec((1, ATT_TQ, HEAD_W), lambda b, h, i: (b, i, h))
    att_out_shape = jax.ShapeDtypeStruct((B, S, ATT_W), bf16)
    smem_spec = pl.BlockSpec(memory_space=pltpu.SMEM)

    lam = (jnp.exp(jnp.sum(lambda_q1[0].astype(f32) * lambda_k1[0].astype(f32)))
           - jnp.exp(jnp.sum(lambda_q2[0].astype(f32) * lambda_k2[0].astype(f32)))
           + LAMBDA_INIT).reshape(1)
    slopes = jnp.exp2(-8.0 * jnp.arange(1, N_HEADS + 1, dtype=f32) / N_HEADS)

    a = pl.pallas_call(
        _diff_attn_kernel,
        grid=(B, N_HEADS, nq),
        in_specs=[smem_spec, smem_spec, q_spec(0), k_spec(0), vt_spec(2),
                  _const_spec((HEAD_W, 1))],
        out_specs=att_out_spec,
        out_shape=att_out_shape,
        scratch_shapes=[pltpu.VMEM((ATT_TK, ATT_TQ), f32), pltpu.VMEM((ATT_TK, ATT_TQ), f32),
                        pltpu.VMEM((2, 2 * ATT_TQ // CHAIN_W, ATT_TK, CHAIN_W), f32),
                        pltpu.VMEM((1, 2 * ATT_TQ), f32), pltpu.VMEM((1, 2 * ATT_TQ), f32),
                        pltpu.VMEM((HEAD_W, 2 * ATT_TQ), f32)],
        compiler_params=_tc_params(3, ATTN_FLAGS),
        name="diff_attn",
    )(slopes, lam, qvt, k3, qvt, diff_subln[0].reshape(-1, 1).astype(f32))

    b = pl.pallas_call(
        _sb_attn_kernel,
        grid=(B, N_HEADS, nq),
        in_specs=[q_spec(1), k_spec(1), vt_spec(3)],
        out_specs=att_out_spec,
        out_shape=att_out_shape,
        scratch_shapes=[pltpu.VMEM((1, ATT_TQ), f32), pltpu.VMEM((HEAD_W, ATT_TQ), f32)],
        compiler_params=_tc_params(3, ATTN_FLAGS),
        name="sb_attn",
    )(qvt, k3, qvt)

    out = pl.pallas_call(
        _mix_ffn2_kernel,
        grid=(T // FFN_TM,),
        in_specs=[tok_spec,
                  pl.BlockSpec((FFN_TM, ATT_W), lambda t: (t, 0)),
                  pl.BlockSpec((FFN_TM, ATT_W), lambda t: (t, 0)),
                  pl.BlockSpec((FFN_TM, GATE_W), lambda t: (t, 0)),
                  _const_spec((ATT_W, D)), _const_spec((ATT_W, D)),
                  _const_spec((D, D)), _const_spec((1, D)), _const_spec((D, D_FF)),
                  _const_spec((D, D_FF)), _const_spec((D_FF, D)), _const_spec((1, D))],
        out_specs=tok_spec,
        out_shape=jax.ShapeDtypeStruct((T, D), f32),
        compiler_params=_tc_params(1),
        name="mix_ffn2",
    )(h1, a.reshape(T, -1), b.reshape(T, -1), gates,
      w_branch_diff[0].astype(bf16), w_branch_sb[0].astype(bf16), w_out[0].astype(bf16),
      row(ffn2_norm[0]), ffn2_w_gate[0].astype(bf16), ffn2_w_up[0].astype(bf16),
      ffn2_w_down[0].astype(bf16), row(final_norm))
    return out.reshape(B, S, D)
```

```python
import math

import jax
import jax.numpy as jnp
from jax import lax
from jax.experimental import pallas as pl
from jax.experimental.pallas import tpu as pltpu

D_MODEL = 1024
D_FF = 2816
N_HEADS = 4
HEAD_W = 128
DA_QK_DIM = 64
ATT_W = N_HEADS * HEAD_W
K_W = 2 * ATT_W
QVT_W = 4 * ATT_W
GATE_W = 2 * D_MODEL
NORM_EPS = 1e-5
LAMBDA_INIT = 0.8 - 0.6 * math.exp(-0.3 * 0)
LOG2E = 1.0 / math.log(2.0)

VMEM_LIMIT_BYTES = 56 * 1024 * 1024

FFN_TM = 256
PROJ_TM = 512
ATT_TQ = 512
ATT_TK = 512
CHAIN_W = 256
SB_DONE_LOG2 = 160.0

_NT = (((1,), (1,)), ((), ()))


def _rms(x, g):
    ms = jnp.mean(x * x, axis=-1, keepdims=True)
    return x * lax.rsqrt(ms + NORM_EPS) * g


def _swiglu_half_step(x, norm_g, wg_ref, wu_ref, wd_ref):
    xn = _rms(x, norm_g).astype(jnp.bfloat16)
    g = jnp.dot(xn, wg_ref[...], preferred_element_type=jnp.float32)
    u = jnp.dot(xn, wu_ref[...], preferred_element_type=jnp.float32)
    hact = (g * jax.nn.sigmoid(g) * u).astype(jnp.bfloat16)
    return x + 0.5 * jnp.dot(hact, wd_ref[...], preferred_element_type=jnp.float32)


def _ffn1_kernel(x_ref, norm_ref, wg_ref, wu_ref, wd_ref, o_ref):
    o_ref[...] = _swiglu_half_step(x_ref[...], norm_ref[...], wg_ref, wu_ref, wd_ref)


def _in_proj_kernel(h_ref, norm_ref, wk_ref, wqvt_ref, wgate_ref, rowscale_ref, bgate_ref,
                    k_ref, qvt_ref, gate_ref):
    n = _rms(h_ref[...], norm_ref[...]).astype(jnp.bfloat16)
    k_ref[...] = jnp.dot(n, wk_ref[...], preferred_element_type=jnp.float32).astype(jnp.bfloat16)
    qvt = lax.dot_general(wqvt_ref[...], n, _NT, preferred_element_type=jnp.float32)
    qvt_ref[...] = (qvt * rowscale_ref[...]).astype(jnp.bfloat16)
    g = jnp.dot(n, wgate_ref[...], preferred_element_type=jnp.float32)
    gate_ref[...] = jax.nn.sigmoid(g + bgate_ref[...]).astype(jnp.bfloat16)


def _emit_pipelined(stages, n):
    state = [dict() for _ in range(n)]
    for step in range(n + len(stages) - 1):
        for s, stage in enumerate(stages):
            t = step - s
            if 0 <= t < n:
                stage(t, state[t], state[t + 1] if t + 1 < n else None)


def _ordered_after(x, token):
    zero = lax.shift_right_logical(
        lax.shift_right_logical(pltpu.bitcast(token, jnp.uint32), jnp.uint32(16)), jnp.uint32(16))
    return pltpu.bitcast(pltpu.bitcast(x, jnp.uint32) + zero, jnp.float32)


def _diff_attn_kernel(slope_ref, lam_ref, q_ref, k_ref, vt_ref, subln_ref, o_ref,
                      mask_ref, s_ref, m_ref, l_ref, acc_ref):
    tq, tk, cw = ATT_TQ, ATT_TK, CHAIN_W
    per_map = tq // cw
    n_chains = 2 * per_map
    h = pl.program_id(1)
    i = pl.program_id(2)
    slope = slope_ref[h] * LOG2E
    lam = lam_ref[0]

    qt = q_ref[...]
    chan = lax.broadcasted_iota(jnp.int32, (HEAD_W, tq), 0)
    zero = jnp.zeros_like(qt)
    q_maps = (jnp.where(chan < DA_QK_DIM, qt, zero), jnp.where(chan >= DA_QK_DIM, qt, zero))

    sl = jnp.full((HEAD_W, cw), slope, jnp.float32)
    hi = sl.astype(jnp.bfloat16).astype(jnp.float32)
    mid = (sl - hi).astype(jnp.bfloat16).astype(jnp.float32)
    lo = sl - hi - mid
    frow = lax.broadcasted_iota(jnp.int32, (HEAD_W, cw), 0)
    part = frow % 3
    q_feat = jnp.where(frow < 6, jnp.where(part == 0, hi, jnp.where(part == 1, mid, lo)),
                       0.0).astype(jnp.bfloat16)
    q_chain = [jnp.concatenate(
        [q_maps[c // per_map][:, (c % per_map) * cw:(c % per_map + 1) * cw], q_feat], axis=0)
        for c in range(n_chains)]

    kpos = lax.broadcasted_iota(jnp.int32, (tk, HEAD_W), 0)
    klane = lax.broadcasted_iota(jnp.int32, (tk, HEAD_W), 1)
    k_hi = jnp.where(kpos >= 256, 256, 0)
    k_feat = jnp.where(klane < 3, k_hi, jnp.where(klane < 6, kpos - k_hi, 0)
                       ).astype(jnp.float32).astype(jnp.bfloat16)

    krow = lax.broadcasted_iota(jnp.int32, (tk, tq), 0)
    qcol = lax.broadcasted_iota(jnp.int32, (tk, tq), 1)
    mask_ref[...] = jnp.where(qcol >= krow, 0.0, -jnp.inf)

    m_ref[...] = jnp.full_like(m_ref, -jnp.inf)
    l_ref[...] = jnp.zeros_like(l_ref)
    acc_ref[...] = jnp.zeros_like(acc_ref)

    def scores_to(slot, j):
        kb = k_ref[0, pl.ds(pl.multiple_of(j * tk, tk), tk), :]
        kb = jnp.concatenate([kb, k_feat], axis=1)
        for c in range(n_chains):
            s_ref[slot, c] = jnp.dot(kb, q_chain[c], preferred_element_type=jnp.float32)

    def consume(slot, j, diagonal):
        vtb = vt_ref[:, pl.ds(pl.multiple_of(j * tk, tk), tk)]
        shift = -slope * ((i - j) * tq).astype(jnp.float32)

        def masked_scores(c):
            s = s_ref[slot, c]
            if diagonal:
                qp = c % per_map
                s = s + mask_ref[:, qp * cw:(qp + 1) * cw]
            return s

        def column_max(c, st, nxt):
            st["cmax"] = jnp.max(masked_scores(c), axis=0, keepdims=True) + shift

        def softmax_pv(c, st, nxt):
            lanes = slice(c * cw, (c + 1) * cw)
            m_prev = m_ref[:, lanes]
            m_new = jnp.maximum(m_prev, st.pop("cmax"))
            st["alpha"] = jnp.exp2(m_prev - m_new)
            p = jnp.exp2(masked_scores(c) - (m_new - shift))
            l_ref[:, lanes] = st["alpha"] * l_ref[:, lanes] + jnp.sum(p, axis=0, keepdims=True)
            m_ref[:, lanes] = m_new
            st["pv"] = jnp.dot(vtb, p.astype(jnp.bfloat16),
                               preferred_element_type=jnp.float32)

        def accumulate(c, st, nxt):
            lanes = slice(c * cw, (c + 1) * cw)
            acc_ref[:, lanes] = st.pop("alpha") * acc_ref[:, lanes] + st.pop("pv")

        _emit_pipelined((column_max, softmax_pv, accumulate), n_chains)

    def step(slot, j):
        scores_to(1 - slot, j + 1)
        consume(slot, j, diagonal=False)

    scores_to(0, 0)

    def pair(jj, carry):
        step(0, 2 * jj)
        step(1, 2 * jj + 1)
        return carry

    lax.fori_loop(0, i // 2, pair, 0)

    @pl.when(i % 2 == 0)
    def _():
        consume(0, i, diagonal=True)

    @pl.when(i % 2 == 1)
    def _():
        step(0, i - 1)
        consume(1, i, diagonal=True)

    o = acc_ref[...] / l_ref[...]
    a = o[:, :tq] - lam * o[:, tq:]
    ms = jnp.mean(a * a, axis=0, keepdims=True)
    a = a * lax.rsqrt(ms + NORM_EPS) * subln_ref[...] * (1.0 - LAMBDA_INIT)
    o_ref[0] = a.T.astype(o_ref.dtype)


def _sb_attn_kernel(q_ref, k_ref, vt_ref, o_ref, c_ref, acc_ref):
    tq, tk, cw = ATT_TQ, ATT_TK, CHAIN_W
    n_chains = tq // cw
    n_sub = tk // cw
    i = pl.program_id(2)
    qt = q_ref[...]
    q_chain = [qt[:, c * cw:(c + 1) * cw] for c in range(n_chains)]

    krow = lax.broadcasted_iota(jnp.int32, (cw, cw), 0)
    qcol = lax.broadcasted_iota(jnp.int32, (cw, cw), 1)
    strict = krow < qcol
    later = jnp.where(qcol > krow, 1.0, 0.0).astype(jnp.bfloat16)

    c_ref[...] = jnp.zeros_like(c_ref)
    acc_ref[...] = jnp.zeros_like(acc_ref)

    def block(j, diagonal):
        pieces = [(sub, c) for sub in reversed(range(n_sub)) for c in range(n_chains)
                  if not (diagonal and sub > c)]
        kbs, vtbs = {}, {}
        for sub in range(n_sub):
            start = pl.multiple_of(j * tk + sub * cw, cw)
            kbs[sub] = k_ref[0, pl.ds(start, cw), :]
            vtbs[sub] = vt_ref[:, pl.ds(start, cw)]

        def scores(t, st, nxt):
            sub, c = pieces[t]
            st["z"] = jnp.dot(kbs[sub], q_chain[c],
                              preferred_element_type=jnp.float32)

        def suffix(t, st, nxt):
            sub, c = pieces[t]
            masked = diagonal and sub == c
            z = st.pop("z")
            sp = jnp.log2(1.0 + jnp.exp2(-jnp.abs(z)))
            u = jnp.maximum(z, 0.0) + sp
            st["log_sig"] = z - u
            if masked:
                u = jnp.where(strict, u, 0.0)
            st["tail"] = jnp.dot(later, u.astype(jnp.bfloat16),
                                 preferred_element_type=jnp.float32)
            st["usum"] = jnp.sum(u, axis=0, keepdims=True)

        def weights_pv(t, st, nxt):
            sub, c = pieces[t]
            masked = diagonal and sub == c
            lanes = slice(c * cw, (c + 1) * cw)
            carry = c_ref[:, lanes]
            if nxt is not None:
                carry = _ordered_after(carry, nxt["usum"])
            a = jnp.exp2(st.pop("log_sig") - st.pop("tail") - carry)
            if masked:
                a = jnp.where(strict, a, 0.0)
            c_ref[:, lanes] += st.pop("usum")
            st["pv"] = jnp.dot(vtbs[sub], a.astype(jnp.bfloat16),
                               preferred_element_type=jnp.float32)

        def accumulate(t, st, nxt):
            sub, c = pieces[t]
            lanes = slice(c * cw, (c + 1) * cw)
            acc_ref[:, lanes] += st.pop("pv")

        _emit_pipelined((scores, suffix, weights_pv, accumulate), len(pieces))

    block(i, diagonal=True)

    def unfinished():
        return jnp.min(c_ref[...]) < SB_DONE_LOG2

    def cond(carry):
        jj, go = carry
        return jnp.logical_and(jj < i, go)

    def body(carry):
        jj, _ = carry
        block(i - 1 - jj, diagonal=False)
        return jj + 1, unfinished()

    lax.while_loop(cond, body, (jnp.int32(0), unfinished()))
    o_ref[0] = acc_ref[...].T.astype(o_ref.dtype)


def _mix_ffn2_kernel(h_ref, a_ref, b_ref, gate_ref, wa_ref, wb_ref, wout_ref,
                     norm2_ref, wg_ref, wu_ref, wd_ref, normf_ref, o_ref):
    ya = jnp.dot(a_ref[...], wa_ref[...], preferred_element_type=jnp.float32)
    yb = jnp.dot(b_ref[...], wb_ref[...], preferred_element_type=jnp.float32)
    gate = gate_ref[...].astype(jnp.float32)
    y = (gate[:, :D_MODEL] * ya + gate[:, D_MODEL:] * yb).astype(jnp.bfloat16)
    h2 = h_ref[...] + jnp.dot(y, wout_ref[...], preferred_element_type=jnp.float32)
    h3 = _swiglu_half_step(h2, norm2_ref[...], wg_ref, wu_ref, wd_ref)
    o_ref[...] = _rms(h3, normf_ref[...])


def _const_spec(shape):
    return pl.BlockSpec(shape, lambda *_: (0,) * len(shape))


ATTN_FLAGS = None


def _tc_params(n_axes, flags=None):
    return pltpu.CompilerParams(dimension_semantics=("arbitrary",) * n_axes,
                                vmem_limit_bytes=VMEM_LIMIT_BYTES, flags=flags)


def kernel(x, ffn1_norm, ffn1_w_gate, ffn1_w_up, ffn1_w_down, mix_norm, w_in, b_gate, lambda_q1, lambda_k1, lambda_q2, lambda_k2, diff_subln, w_branch_diff, w_branch_sb, w_out, ffn2_norm, ffn2_w_gate, ffn2_w_up, ffn2_w_down, final_norm):
    B, S, D = x.shape
    T = B * S
    f32, bf16 = jnp.float32, jnp.bfloat16
    xt = x.reshape(T, D)
    row = lambda v: v.reshape(1, -1).astype(f32)

    tok_spec = pl.BlockSpec((FFN_TM, D), lambda t: (t, 0))
    h1 = pl.pallas_call(
        _ffn1_kernel,
        grid=(T // FFN_TM,),
        in_specs=[tok_spec, _const_spec((1, D)), _const_spec((D, D_FF)),
                  _const_spec((D, D_FF)), _const_spec((D_FF, D))],
        out_specs=tok_spec,
        out_shape=jax.ShapeDtypeStruct((T, D), f32),
        compiler_params=_tc_params(1),
        name="ffn1",
    )(xt, row(ffn1_norm[0]), ffn1_w_gate[0].astype(bf16), ffn1_w_up[0].astype(bf16),
      ffn1_w_down[0].astype(bf16))

    w = w_in[0]
    w_k = jnp.concatenate([w[:, ATT_W:2 * ATT_W], w[:, 4 * ATT_W:5 * ATT_W]], axis=1).astype(bf16)
    w_qvt = jnp.concatenate([w[:, 0:ATT_W], w[:, 3 * ATT_W:4 * ATT_W],
                             w[:, 2 * ATT_W:3 * ATT_W], w[:, 5 * ATT_W:6 * ATT_W]],
                            axis=1).T.astype(bf16)
    w_gate = w[:, 6 * ATT_W:].astype(bf16)
    rowscale = jnp.ones((QVT_W,), f32)
    rowscale = rowscale.at[0:ATT_W].set(DA_QK_DIM ** -0.5 * LOG2E)
    rowscale = rowscale.at[ATT_W:2 * ATT_W].set(HEAD_W ** -0.5 * LOG2E)
    k, qvt, gates = pl.pallas_call(
        _in_proj_kernel,
        grid=(T // PROJ_TM,),
        in_specs=[pl.BlockSpec((PROJ_TM, D), lambda t: (t, 0)), _const_spec((1, D)),
                  _const_spec((D, K_W)), _const_spec((QVT_W, D)), _const_spec((D, GATE_W)),
                  _const_spec((QVT_W, 1)), _const_spec((1, GATE_W))],
        out_specs=[pl.BlockSpec((PROJ_TM, K_W), lambda t: (t, 0)),
                   pl.BlockSpec((QVT_W, PROJ_TM), lambda t: (0, t)),
                   pl.BlockSpec((PROJ_TM, GATE_W), lambda t: (t, 0))],
        out_shape=[jax.ShapeDtypeStruct((T, K_W), bf16),
                   jax.ShapeDtypeStruct((QVT_W, T), bf16),
                   jax.ShapeDtypeStruct((T, GATE_W), bf16)],
        compiler_params=_tc_params(1),
        name="in_proj",
    )(h1, row(mix_norm[0]), w_k, w_qvt, w_gate, rowscale.reshape(-1, 1), row(b_gate[0]))
    k3 = k.reshape(B, S, K_W)

    nq = S // ATT_TQ

    def q_spec(slab):
        return pl.BlockSpec((HEAD_W, ATT_TQ), lambda b, h, i: (slab * N_HEADS + h, b * nq + i))

    def k_spec(slab):
        return pl.BlockSpec((1, S, HEAD_W), lambda b, h, i: (b, 0, slab * N_HEADS + h))

    def vt_spec(slab):
        return pl.BlockSpec((HEAD_W, S), lambda b, h, i: (slab * N_HEADS + h, b))

    att_out_spec = pl.BlockSpec((1, ATT_TQ, HEAD_W), lambda b, h, i: (b, i, h))
    att_out_shape = jax.ShapeDtypeStruct((B, S, ATT_W), bf16)
    smem_spec = pl.BlockSpec(memory_space=pltpu.SMEM)

    lam = (jnp.exp(jnp.sum(lambda_q1[0].astype(f32) * lambda_k1[0].astype(f32)))
           - jnp.exp(jnp.sum(lambda_q2[0].astype(f32) * lambda_k2[0].astype(f32)))
           + LAMBDA_INIT).reshape(1)
    slopes = jnp.exp2(-8.0 * jnp.arange(1, N_HEADS + 1, dtype=f32) / N_HEADS)

    a = pl.pallas_call(
        _diff_attn_kernel,
        grid=(B, N_HEADS, nq),
        in_specs=[smem_spec, smem_spec, q_spec(0), k_spec(0), vt_spec(2),
                  _const_spec((HEAD_W, 1))],
        out_specs=att_out_spec,
        out_shape=att_out_shape,
        scratch_shapes=[pltpu.VMEM((ATT_TK, ATT_TQ), f32),
                        pltpu.VMEM((2, 2 * ATT_TQ // CHAIN_W, ATT_TK, CHAIN_W), f32),
                        pltpu.VMEM((1, 2 * ATT_TQ), f32), pltpu.VMEM((1, 2 * ATT_TQ), f32),
                        pltpu.VMEM((HEAD_W, 2 * ATT_TQ), f32)],
        compiler_params=_tc_params(3, ATTN_FLAGS),
        name="diff_attn",
    )(slopes, lam, qvt, k3, qvt, diff_subln[0].reshape(-1, 1).astype(f32))

    b = pl.pallas_call(
        _sb_attn_kernel,
        grid=(B, N_HEADS, nq),
        in_specs=[q_spec(1), k_spec(1), vt_spec(3)],
        out_specs=att_out_spec,
        out_shape=att_out_shape,
        scratch_shapes=[pltpu.VMEM((1, ATT_TQ), f32), pltpu.VMEM((HEAD_W, ATT_TQ), f32)],
        compiler_params=_tc_params(3, ATTN_FLAGS),
        name="sb_attn",
    )(qvt, k3, qvt)

    out = pl.pallas_call(
        _mix_ffn2_kernel,
        grid=(T // FFN_TM,),
        in_specs=[tok_spec,
                  pl.BlockSpec((FFN_TM, ATT_W), lambda t: (t, 0)),
                  pl.BlockSpec((FFN_TM, ATT_W), lambda t: (t, 0)),
                  pl.BlockSpec((FFN_TM, GATE_W), lambda t: (t, 0)),
                  _const_spec((ATT_W, D)), _const_spec((ATT_W, D)),
                  _const_spec((D, D)), _const_spec((1, D)), _const_spec((D, D_FF)),
                  _const_spec((D, D_FF)), _const_spec((D_FF, D)), _const_spec((1, D))],
        out_specs=tok_spec,
        out_shape=jax.ShapeDtypeStruct((T, D), f32),
        compiler_params=_tc_params(1),
        name="mix_ffn2",
    )(h1, a.reshape(T, -1), b.reshape(T, -1), gates,
      w_branch_diff[0].astype(bf16), w_branch_sb[0].astype(bf16), w_out[0].astype(bf16),
      row(ffn2_norm[0]), ffn2_w_gate[0].astype(bf16), ffn2_w_up[0].astype(bf16),
      ffn2_w_down[0].astype(bf16), row(final_norm))
    return out.reshape(B, S, D)
```

```python
import math

import jax
import jax.numpy as jnp
from jax import lax
from jax.experimental import pallas as pl
from jax.experimental.pallas import tpu as pltpu

D_MODEL = 1024
D_FF = 2816
N_HEADS = 4
HEAD_W = 128
DA_QK_DIM = 64
ATT_W = N_HEADS * HEAD_W
K_W = 2 * ATT_W
QVT_W = 4 * ATT_W
GATE_W = 2 * D_MODEL
NORM_EPS = 1e-5
LAMBDA_INIT = 0.8 - 0.6 * math.exp(-0.3 * 0)
LOG2E = 1.0 / math.log(2.0)

VMEM_LIMIT_BYTES = 56 * 1024 * 1024

FFN_TM = 256
PROJ_TM = 512
ATT_TQ = 512
ATT_TK = 512
SB_TQ = 1024
CHAIN_W = 256
SB_DONE_LOG2 = 160.0
SUM_ROWS = 16

_NT = (((1,), (1,)), ((), ()))


def _rms(x, g):
    ms = jnp.mean(x * x, axis=-1, keepdims=True)
    return x * lax.rsqrt(ms + NORM_EPS) * g


def _swiglu_half_step(x, norm_g, wg_ref, wu_ref, wd_ref):
    xn = _rms(x, norm_g).astype(jnp.bfloat16)
    g = jnp.dot(xn, wg_ref[...], preferred_element_type=jnp.float32)
    u = jnp.dot(xn, wu_ref[...], preferred_element_type=jnp.float32)
    hact = (g * jax.nn.sigmoid(g) * u).astype(jnp.bfloat16)
    return x + 0.5 * jnp.dot(hact, wd_ref[...], preferred_element_type=jnp.float32)


def _ffn1_kernel(x_ref, norm_ref, wg_ref, wu_ref, wd_ref, o_ref):
    o_ref[...] = _swiglu_half_step(x_ref[...], norm_ref[...], wg_ref, wu_ref, wd_ref)


def _in_proj_kernel(h_ref, norm_ref, wk_ref, wqvt_ref, wgate_ref, rowscale_ref, bgate_ref,
                    k_ref, qvt_ref, gate_ref):
    n = _rms(h_ref[...], norm_ref[...]).astype(jnp.bfloat16)
    k_ref[...] = jnp.dot(n, wk_ref[...], preferred_element_type=jnp.float32).astype(jnp.bfloat16)
    qvt = lax.dot_general(wqvt_ref[...], n, _NT, preferred_element_type=jnp.float32)
    qvt_ref[...] = (qvt * rowscale_ref[...]).astype(jnp.bfloat16)
    g = jnp.dot(n, wgate_ref[...], preferred_element_type=jnp.float32)
    gate_ref[...] = jax.nn.sigmoid(g + bgate_ref[...]).astype(jnp.bfloat16)


def _emit_pipelined(stages, n):
    state = [dict() for _ in range(n)]
    for step in range(n + len(stages) - 1):
        for s, stage in enumerate(stages):
            t = step - s
            if 0 <= t < n:
                stage(t, state[t], state[t + 1] if t + 1 < n else None)


def _ordered_after(x, token):
    zero = lax.shift_right_logical(
        lax.shift_right_logical(pltpu.bitcast(token, jnp.uint32), jnp.uint32(16)), jnp.uint32(16))
    return pltpu.bitcast(pltpu.bitcast(x, jnp.uint32) + zero, jnp.float32)


def _diff_attn_kernel(slope_ref, lam_ref, q_ref, k_ref, vt_ref, subln_ref, o_ref,
                      mask_ref, s_ref, m_ref, l_ref, acc_ref):
    tq, tk, cw = ATT_TQ, ATT_TK, CHAIN_W
    per_map = tq // cw
    n_chains = 2 * per_map
    h = pl.program_id(1)
    i = pl.program_id(2)
    slope = slope_ref[h] * LOG2E
    lam = lam_ref[0]

    qt = q_ref[...]
    chan = lax.broadcasted_iota(jnp.int32, (HEAD_W, tq), 0)
    zero = jnp.zeros_like(qt)
    q_maps = (jnp.where(chan < DA_QK_DIM, qt, zero), jnp.where(chan >= DA_QK_DIM, qt, zero))

    sl = jnp.full((HEAD_W, cw), slope, jnp.float32)
    hi = sl.astype(jnp.bfloat16).astype(jnp.float32)
    mid = (sl - hi).astype(jnp.bfloat16).astype(jnp.float32)
    lo = sl - hi - mid
    frow = lax.broadcasted_iota(jnp.int32, (HEAD_W, cw), 0)
    part = frow % 3
    q_feat = jnp.where(frow < 6, jnp.where(part == 0, hi, jnp.where(part == 1, mid, lo)),
                       0.0).astype(jnp.bfloat16)
    q_chain = [jnp.concatenate(
        [q_maps[c // per_map][:, (c % per_map) * cw:(c % per_map + 1) * cw], q_feat], axis=0)
        for c in range(n_chains)]

    kpos = lax.broadcasted_iota(jnp.int32, (tk, HEAD_W), 0)
    klane = lax.broadcasted_iota(jnp.int32, (tk, HEAD_W), 1)
    k_hi = jnp.where(kpos >= 256, 256, 0)
    k_feat = jnp.where(klane < 3, k_hi, jnp.where(klane < 6, kpos - k_hi, 0)
                       ).astype(jnp.float32).astype(jnp.bfloat16)

    krow = lax.broadcasted_iota(jnp.int32, (tk, tq), 0)
    qcol = lax.broadcasted_iota(jnp.int32, (tk, tq), 1)
    mask_ref[...] = jnp.where(qcol >= krow, 0.0, -jnp.inf)

    m_ref[...] = jnp.full_like(m_ref, -jnp.inf)
    l_ref[...] = jnp.zeros_like(l_ref)
    acc_ref[...] = jnp.zeros_like(acc_ref)

    def scores_to(slot, j):
        kb = k_ref[0, pl.ds(pl.multiple_of(j * tk, tk), tk), :]
        kb = jnp.concatenate([kb, k_feat], axis=1)
        for c in range(n_chains):
            s_ref[slot, c] = jnp.dot(kb, q_chain[c], preferred_element_type=jnp.float32)

    def consume(slot, j, diagonal):
        vtb = vt_ref[:, pl.ds(pl.multiple_of(j * tk, tk), tk)]
        shift = -slope * ((i - j) * tq).astype(jnp.float32)

        def masked_scores(c):
            s = s_ref[slot, c]
            if diagonal:
                qp = c % per_map
                s = s + mask_ref[:, qp * cw:(qp + 1) * cw]
            return s

        def column_max(c, st, nxt):
            st["cmax"] = jnp.max(masked_scores(c), axis=0, keepdims=True) + shift

        def softmax_pv(c, st, nxt):
            lanes = slice(c * cw, (c + 1) * cw)
            m_prev = m_ref[:, lanes]
            m_new = jnp.maximum(m_prev, st.pop("cmax"))
            st["alpha"] = jnp.exp2(m_prev - m_new)
            p = jnp.exp2(masked_scores(c) - (m_new - shift))
            l_ref[:, lanes] = st["alpha"] * l_ref[:, lanes] + jnp.sum(p, axis=0, keepdims=True)
            m_ref[:, lanes] = m_new
            st["pv"] = jnp.dot(vtb, p.astype(jnp.bfloat16),
                               preferred_element_type=jnp.float32)

        def accumulate(c, st, nxt):
            lanes = slice(c * cw, (c + 1) * cw)
            acc_ref[:, lanes] = st.pop("alpha") * acc_ref[:, lanes] + st.pop("pv")

        _emit_pipelined((column_max, softmax_pv, accumulate), n_chains)

    def step(slot, j):
        scores_to(1 - slot, j + 1)
        consume(slot, j, diagonal=False)

    scores_to(0, 0)

    def pair(jj, carry):
        step(0, 2 * jj)
        step(1, 2 * jj + 1)
        return carry

    lax.fori_loop(0, i // 2, pair, 0)

    @pl.when(i % 2 == 0)
    def _():
        consume(0, i, diagonal=True)

    @pl.when(i % 2 == 1)
    def _():
        step(0, i - 1)
        consume(1, i, diagonal=True)

    o = acc_ref[...] / l_ref[...]
    a = o[:, :tq] - lam * o[:, tq:]
    ms = jnp.mean(a * a, axis=0, keepdims=True)
    a = a * lax.rsqrt(ms + NORM_EPS) * subln_ref[...] * (1.0 - LAMBDA_INIT)
    o_ref[0] = a.T.astype(o_ref.dtype)


def _sb_attn_kernel(q_ref, k_ref, vt_ref, o_ref, c_ref, acc_ref):
    tq, cw = SB_TQ, CHAIN_W
    n_chains = tq // cw
    i = pl.program_id(2)
    qt = q_ref[...]
    q_chain = [qt[:, c * cw:(c + 1) * cw] for c in range(n_chains)]

    krow = lax.broadcasted_iota(jnp.int32, (cw, cw), 0)
    qcol = lax.broadcasted_iota(jnp.int32, (cw, cw), 1)
    strict = krow < qcol
    lrow = lax.broadcasted_iota(jnp.int32, (cw + SUM_ROWS, cw), 0)
    lcol = lax.broadcasted_iota(jnp.int32, (cw + SUM_ROWS, cw), 1)
    later = jnp.where(jnp.logical_or(lcol > lrow, lrow >= cw), 1.0, 0.0).astype(jnp.bfloat16)
    sign_bit = jnp.uint32(0x80000000)

    c_ref[...] = jnp.zeros_like(c_ref)
    acc_ref[...] = jnp.zeros_like(acc_ref)

    def run_pieces(pieces):
        def scores(t, st, nxt):
            sub, c, _ = pieces[t]
            start = pl.multiple_of(sub * cw, cw)
            st["z"] = jnp.dot(k_ref[0, pl.ds(start, cw), :], q_chain[c],
                              preferred_element_type=jnp.float32)

        def suffix(t, st, nxt):
            _, _, triangular = pieces[t]
            z = st.pop("z")
            neg_abs = pltpu.bitcast(pltpu.bitcast(z, jnp.uint32) | sign_bit, jnp.float32)
            u = jnp.maximum(z, 0.0) + jnp.log2(1.0 + jnp.exp2(neg_abs))
            st["log_sig"] = z - u
            if triangular:
                u = jnp.where(strict, u, 0.0)
            st["tail"] = jnp.dot(later, u.astype(jnp.bfloat16),
                                 preferred_element_type=jnp.float32)

        def weights_pv(t, st, nxt):
            sub, _, triangular = pieces[t]
            start = pl.multiple_of(sub * cw, cw)
            tail = st.pop("tail")
            st["usum"] = tail[cw:cw + 1]
            a = jnp.exp2(st.pop("log_sig") - tail[:cw])
            if triangular:
                a = jnp.where(strict, a, 0.0)
            st["pv"] = jnp.dot(vt_ref[:, pl.ds(start, cw)], a.astype(jnp.bfloat16),
                               preferred_element_type=jnp.float32)

        def accumulate(t, st, nxt):
            _, c, _ = pieces[t]
            lanes = slice(c * cw, (c + 1) * cw)
            carry = c_ref[:, lanes]
            acc_ref[:, lanes] += st.pop("pv") * jnp.exp2(-carry)
            c_ref[:, lanes] = carry + st.pop("usum")

        _emit_pipelined((scores, suffix, weights_pv, accumulate), len(pieces))

    diag = [n_chains * i + c for c in range(n_chains)]
    head = [(diag[c], c, True) for c in reversed(range(n_chains))]
    second = [(diag[c] - 1, c, False) for c in reversed(range(n_chains))]

    @pl.when(i == 0)
    def _():
        run_pieces(head + [p for p in second if p[1] > 0])

    @pl.when(i > 0)
    def _():
        run_pieces(head + second)

    def unfinished(c, depth):
        lanes = slice(c * cw, (c + 1) * cw)
        return jnp.logical_and(diag[c] - depth >= 0,
                               jnp.min(c_ref[:, lanes]) < SB_DONE_LOG2)

    def any_unfinished(depth):
        go = unfinished(0, depth)
        for c in range(1, n_chains):
            go = jnp.logical_or(go, unfinished(c, depth))
        return go

    def body(carry):
        depth, _ = carry
        for c in range(n_chains):
            @pl.when(unfinished(c, depth))
            def _():
                run_pieces([(diag[c] - depth, c, False)])
        return depth + 1, any_unfinished(depth + 1)

    lax.while_loop(lambda carry: carry[1], body, (jnp.int32(2), any_unfinished(2)))
    o_ref[0] = acc_ref[...].T.astype(o_ref.dtype)


def _mix_ffn2_kernel(h_ref, a_ref, b_ref, gate_ref, wa_ref, wb_ref, wout_ref,
                     norm2_ref, wg_ref, wu_ref, wd_ref, normf_ref, o_ref):
    ya = jnp.dot(a_ref[...], wa_ref[...], preferred_element_type=jnp.float32)
    yb = jnp.dot(b_ref[...], wb_ref[...], preferred_element_type=jnp.float32)
    gate = gate_ref[...].astype(jnp.float32)
    y = (gate[:, :D_MODEL] * ya + gate[:, D_MODEL:] * yb).astype(jnp.bfloat16)
    h2 = h_ref[...] + jnp.dot(y, wout_ref[...], preferred_element_type=jnp.float32)
    h3 = _swiglu_half_step(h2, norm2_ref[...], wg_ref, wu_ref, wd_ref)
    o_ref[...] = _rms(h3, normf_ref[...])


def _const_spec(shape):
    return pl.BlockSpec(shape, lambda *_: (0,) * len(shape))


ATTN_FLAGS = None


def _tc_params(n_axes, flags=None):
    return pltpu.CompilerParams(dimension_semantics=("arbitrary",) * n_axes,
                                vmem_limit_bytes=VMEM_LIMIT_BYTES, flags=flags)


def kernel(x, ffn1_norm, ffn1_w_gate, ffn1_w_up, ffn1_w_down, mix_norm, w_in, b_gate, lambda_q1, lambda_k1, lambda_q2, lambda_k2, diff_subln, w_branch_diff, w_branch_sb, w_out, ffn2_norm, ffn2_w_gate, ffn2_w_up, ffn2_w_down, final_norm):
    B, S, D = x.shape
    T = B * S
    f32, bf16 = jnp.float32, jnp.bfloat16
    xt = x.reshape(T, D)
    row = lambda v: v.reshape(1, -1).astype(f32)

    tok_spec = pl.BlockSpec((FFN_TM, D), lambda t: (t, 0))
    h1 = pl.pallas_call(
        _ffn1_kernel,
        grid=(T // FFN_TM,),
        in_specs=[tok_spec, _const_spec((1, D)), _const_spec((D, D_FF)),
                  _const_spec((D, D_FF)), _const_spec((D_FF, D))],
        out_specs=tok_spec,
        out_shape=jax.ShapeDtypeStruct((T, D), f32),
        compiler_params=_tc_params(1),
        name="ffn1",
    )(xt, row(ffn1_norm[0]), ffn1_w_gate[0].astype(bf16), ffn1_w_up[0].astype(bf16),
      ffn1_w_down[0].astype(bf16))

    w = w_in[0]
    w_k = jnp.concatenate([w[:, ATT_W:2 * ATT_W], w[:, 4 * ATT_W:5 * ATT_W]], axis=1).astype(bf16)
    w_qvt = jnp.concatenate([w[:, 0:ATT_W], w[:, 3 * ATT_W:4 * ATT_W],
                             w[:, 2 * ATT_W:3 * ATT_W], w[:, 5 * ATT_W:6 * ATT_W]],
                            axis=1).T.astype(bf16)
    w_gate = w[:, 6 * ATT_W:].astype(bf16)
    rowscale = jnp.ones((QVT_W,), f32)
    rowscale = rowscale.at[0:ATT_W].set(DA_QK_DIM ** -0.5 * LOG2E)
    rowscale = rowscale.at[ATT_W:2 * ATT_W].set(HEAD_W ** -0.5 * LOG2E)
    k, qvt, gates = pl.pallas_call(
        _in_proj_kernel,
        grid=(T // PROJ_TM,),
        in_specs=[pl.BlockSpec((PROJ_TM, D), lambda t: (t, 0)), _const_spec((1, D)),
                  _const_spec((D, K_W)), _const_spec((QVT_W, D)), _const_spec((D, GATE_W)),
                  _const_spec((QVT_W, 1)), _const_spec((1, GATE_W))],
        out_specs=[pl.BlockSpec((PROJ_TM, K_W), lambda t: (t, 0)),
                   pl.BlockSpec((QVT_W, PROJ_TM), lambda t: (0, t)),
                   pl.BlockSpec((PROJ_TM, GATE_W), lambda t: (t, 0))],
        out_shape=[jax.ShapeDtypeStruct((T, K_W), bf16),
                   jax.ShapeDtypeStruct((QVT_W, T), bf16),
                   jax.ShapeDtypeStruct((T, GATE_W), bf16)],
        compiler_params=_tc_params(1),
        name="in_proj",
    )(h1, row(mix_norm[0]), w_k, w_qvt, w_gate, rowscale.reshape(-1, 1), row(b_gate[0]))
    k3 = k.reshape(B, S, K_W)

    def q_spec(slab, tq):
        return pl.BlockSpec((HEAD_W, tq),
                            lambda b, h, i: (slab * N_HEADS + h, b * (S // tq) + i))

    def k_spec(slab):
        return pl.BlockSpec((1, S, HEAD_W), lambda b, h, i: (b, 0, slab * N_HEADS + h))

    def vt_spec(slab):
        return pl.BlockSpec((HEAD_W, S), lambda b, h, i: (slab * N_HEADS + h, b))

    def att_out_spec(tq):
        return pl.BlockSpec((1, tq, HEAD_W), lambda b, h, i: (b, i, h))

    att_out_shape = jax.ShapeDtypeStruct((B, S, ATT_W), bf16)
    smem_spec = pl.BlockSpec(memory_space=pltpu.SMEM)

    lam = (jnp.exp(jnp.sum(lambda_q1[0].astype(f32) * lambda_k1[0].astype(f32)))
           - jnp.exp(jnp.sum(lambda_q2[0].astype(f32) * lambda_k2[0].astype(f32)))
           + LAMBDA_INIT).reshape(1)
    slopes = jnp.exp2(-8.0 * jnp.arange(1, N_HEADS + 1, dtype=f32) / N_HEADS)

    a = pl.pallas_call(
        _diff_attn_kernel,
        grid=(B, N_HEADS, S // ATT_TQ),
        in_specs=[smem_spec, smem_spec, q_spec(0, ATT_TQ), k_spec(0), vt_spec(2),
                  _const_spec((HEAD_W, 1))],
        out_specs=att_out_spec(ATT_TQ),
        out_shape=att_out_shape,
        scratch_shapes=[pltpu.VMEM((ATT_TK, ATT_TQ), f32),
                        pltpu.VMEM((2, 2 * ATT_TQ // CHAIN_W, ATT_TK, CHAIN_W), f32),
                        pltpu.VMEM((1, 2 * ATT_TQ), f32), pltpu.VMEM((1, 2 * ATT_TQ), f32),
                        pltpu.VMEM((HEAD_W, 2 * ATT_TQ), f32)],
        compiler_params=_tc_params(3, ATTN_FLAGS),
        name="diff_attn",
    )(slopes, lam, qvt, k3, qvt, diff_subln[0].reshape(-1, 1).astype(f32))

    b = pl.pallas_call(
        _sb_attn_kernel,
        grid=(B, N_HEADS, S // SB_TQ),
        in_specs=[q_spec(1, SB_TQ), k_spec(1), vt_spec(3)],
        out_specs=att_out_spec(SB_TQ),
        out_shape=att_out_shape,
        scratch_shapes=[pltpu.VMEM((1, SB_TQ), f32), pltpu.VMEM((HEAD_W, SB_TQ), f32)],
        compiler_params=_tc_params(3, ATTN_FLAGS),
        name="sb_attn",
    )(qvt, k3, qvt)

    out = pl.pallas_call(
        _mix_ffn2_kernel,
        grid=(T // FFN_TM,),
        in_specs=[tok_spec,
                  pl.BlockSpec((FFN_TM, ATT_W), lambda t: (t, 0)),
                  pl.BlockSpec((FFN_TM, ATT_W), lambda t: (t, 0)),
                  pl.BlockSpec((FFN_TM, GATE_W), lambda t: (t, 0)),
                  _const_spec((ATT_W, D)), _const_spec((ATT_W, D)),
                  _const_spec((D, D)), _const_spec((1, D)), _const_spec((D, D_FF)),
                  _const_spec((D, D_FF)), _const_spec((D_FF, D)), _const_spec((1, D))],
        out_specs=tok_spec,
        out_shape=jax.ShapeDtypeStruct((T, D), f32),
        compiler_params=_tc_params(1),
        name="mix_ffn2",
    )(h1, a.reshape(T, -1), b.reshape(T, -1), gates,
      w_branch_diff[0].astype(bf16), w_branch_sb[0].astype(bf16), w_out[0].astype(bf16),
      row(ffn2_norm[0]), ffn2_w_gate[0].astype(bf16), ffn2_w_up[0].astype(bf16),
      ffn2_w_down[0].astype(bf16), row(final_norm))
    return out.reshape(B, S, D)
```

```python
import math

import jax
import jax.numpy as jnp
from jax import lax
from jax.experimental import pallas as pl
from jax.experimental.pallas import tpu as pltpu

D_MODEL = 1024
D_FF = 2816
N_HEADS = 4
HEAD_W = 128
DA_QK_DIM = 64
ATT_W = N_HEADS * HEAD_W
K_W = 2 * ATT_W
QVT_W = 4 * ATT_W
GATE_W = 2 * D_MODEL
NORM_EPS = 1e-5
LAMBDA_INIT = 0.8 - 0.6 * math.exp(-0.3 * 0)
LOG2E = 1.0 / math.log(2.0)

VMEM_LIMIT_BYTES = 56 * 1024 * 1024

FFN_TM = 256
PROJ_TM = 512
ATT_TQ = 512
ATT_TK = 512
SB_TQ = 2048
CHAIN_W = 256
SB_DONE_LOG2 = 160.0
SUM_ROWS = 16

_NT = (((1,), (1,)), ((), ()))


def _rms(x, g):
    ms = jnp.mean(x * x, axis=-1, keepdims=True)
    return x * lax.rsqrt(ms + NORM_EPS) * g


def _swiglu_half_step(x, norm_g, wg_ref, wu_ref, wd_ref):
    xn = _rms(x, norm_g).astype(jnp.bfloat16)
    g = jnp.dot(xn, wg_ref[...], preferred_element_type=jnp.float32)
    u = jnp.dot(xn, wu_ref[...], preferred_element_type=jnp.float32)
    hact = (g * jax.nn.sigmoid(g) * u).astype(jnp.bfloat16)
    return x + 0.5 * jnp.dot(hact, wd_ref[...], preferred_element_type=jnp.float32)


def _ffn1_kernel(x_ref, norm_ref, wg_ref, wu_ref, wd_ref, o_ref):
    o_ref[...] = _swiglu_half_step(x_ref[...], norm_ref[...], wg_ref, wu_ref, wd_ref)


def _in_proj_kernel(h_ref, norm_ref, wk_ref, wqvt_ref, wgate_ref, rowscale_ref, bgate_ref,
                    k_ref, qvt_ref, gate_ref):
    n = _rms(h_ref[...], norm_ref[...]).astype(jnp.bfloat16)
    k_ref[...] = jnp.dot(n, wk_ref[...], preferred_element_type=jnp.float32).astype(jnp.bfloat16)
    qvt = lax.dot_general(wqvt_ref[...], n, _NT, preferred_element_type=jnp.float32)
    qvt_ref[...] = (qvt * rowscale_ref[...]).astype(jnp.bfloat16)
    g = jnp.dot(n, wgate_ref[...], preferred_element_type=jnp.float32)
    gate_ref[...] = jax.nn.sigmoid(g + bgate_ref[...]).astype(jnp.bfloat16)


def _emit_pipelined(stages, n):
    state = [dict() for _ in range(n)]
    for step in range(n + len(stages) - 1):
        for s, stage in enumerate(stages):
            t = step - s
            if 0 <= t < n:
                stage(t, state[t], state[t + 1] if t + 1 < n else None)


def _ordered_after(x, token):
    zero = lax.shift_right_logical(
        lax.shift_right_logical(pltpu.bitcast(token, jnp.uint32), jnp.uint32(16)), jnp.uint32(16))
    return pltpu.bitcast(pltpu.bitcast(x, jnp.uint32) + zero, jnp.float32)


def _diff_attn_kernel(slope_ref, lam_ref, q_ref, k_ref, vt_ref, subln_ref, o_ref,
                      mask_ref, s_ref, m_ref, l_ref, acc_ref):
    tq, tk, cw = ATT_TQ, ATT_TK, CHAIN_W
    per_map = tq // cw
    n_chains = 2 * per_map
    h = pl.program_id(1)
    i = pl.program_id(2)
    slope = slope_ref[h] * LOG2E
    lam = lam_ref[0]

    qt = q_ref[...]
    chan = lax.broadcasted_iota(jnp.int32, (HEAD_W, tq), 0)
    zero = jnp.zeros_like(qt)
    q_maps = (jnp.where(chan < DA_QK_DIM, qt, zero), jnp.where(chan >= DA_QK_DIM, qt, zero))

    sl = jnp.full((HEAD_W, cw), slope, jnp.float32)
    hi = sl.astype(jnp.bfloat16).astype(jnp.float32)
    mid = (sl - hi).astype(jnp.bfloat16).astype(jnp.float32)
    lo = sl - hi - mid
    frow = lax.broadcasted_iota(jnp.int32, (HEAD_W, cw), 0)
    part = frow % 3
    q_feat = jnp.where(frow < 6, jnp.where(part == 0, hi, jnp.where(part == 1, mid, lo)),
                       0.0).astype(jnp.bfloat16)
    q_chain = [jnp.concatenate(
        [q_maps[c // per_map][:, (c % per_map) * cw:(c % per_map + 1) * cw], q_feat], axis=0)
        for c in range(n_chains)]

    kpos = lax.broadcasted_iota(jnp.int32, (tk, HEAD_W), 0)
    klane = lax.broadcasted_iota(jnp.int32, (tk, HEAD_W), 1)
    k_hi = jnp.where(kpos >= 256, 256, 0)
    k_feat = jnp.where(klane < 3, k_hi, jnp.where(klane < 6, kpos - k_hi, 0)
                       ).astype(jnp.float32).astype(jnp.bfloat16)

    krow = lax.broadcasted_iota(jnp.int32, (tk, tq), 0)
    qcol = lax.broadcasted_iota(jnp.int32, (tk, tq), 1)
    mask_ref[...] = jnp.where(qcol >= krow, 0.0, -jnp.inf)

    m_ref[...] = jnp.full_like(m_ref, -jnp.inf)
    l_ref[...] = jnp.zeros_like(l_ref)
    acc_ref[...] = jnp.zeros_like(acc_ref)

    def scores_to(slot, j):
        kb = k_ref[0, pl.ds(pl.multiple_of(j * tk, tk), tk), :]
        kb = jnp.concatenate([kb, k_feat], axis=1)
        for c in range(n_chains):
            s_ref[slot, c] = jnp.dot(kb, q_chain[c], preferred_element_type=jnp.float32)

    def consume(slot, j, diagonal):
        vtb = vt_ref[:, pl.ds(pl.multiple_of(j * tk, tk), tk)]
        shift = -slope * ((i - j) * tq).astype(jnp.float32)

        def masked_scores(c):
            s = s_ref[slot, c]
            if diagonal:
                qp = c % per_map
                s = s + mask_ref[:, qp * cw:(qp + 1) * cw]
            return s

        def column_max(c, st, nxt):
            st["cmax"] = jnp.max(masked_scores(c), axis=0, keepdims=True) + shift

        def softmax_pv(c, st, nxt):
            lanes = slice(c * cw, (c + 1) * cw)
            m_prev = m_ref[:, lanes]
            m_new = jnp.maximum(m_prev, st.pop("cmax"))
            st["alpha"] = jnp.exp2(m_prev - m_new)
            p = jnp.exp2(masked_scores(c) - (m_new - shift))
            l_ref[:, lanes] = st["alpha"] * l_ref[:, lanes] + jnp.sum(p, axis=0, keepdims=True)
            m_ref[:, lanes] = m_new
            st["pv"] = jnp.dot(vtb, p.astype(jnp.bfloat16),
                               preferred_element_type=jnp.float32)

        def accumulate(c, st, nxt):
            lanes = slice(c * cw, (c + 1) * cw)
            acc_ref[:, lanes] = st.pop("alpha") * acc_ref[:, lanes] + st.pop("pv")

        _emit_pipelined((column_max, softmax_pv, accumulate), n_chains)

    def step(slot, j):
        scores_to(1 - slot, j + 1)
        consume(slot, j, diagonal=False)

    scores_to(0, 0)

    def pair(jj, carry):
        step(0, 2 * jj)
        step(1, 2 * jj + 1)
        return carry

    lax.fori_loop(0, i // 2, pair, 0)

    @pl.when(i % 2 == 0)
    def _():
        consume(0, i, diagonal=True)

    @pl.when(i % 2 == 1)
    def _():
        step(0, i - 1)
        consume(1, i, diagonal=True)

    o = acc_ref[...] / l_ref[...]
    a = o[:, :tq] - lam * o[:, tq:]
    ms = jnp.mean(a * a, axis=0, keepdims=True)
    a = a * lax.rsqrt(ms + NORM_EPS) * subln_ref[...] * (1.0 - LAMBDA_INIT)
    o_ref[0] = a.T.astype(o_ref.dtype)


def _sb_attn_kernel(q_ref, k_ref, vt_ref, o_ref, c_ref, acc_ref):
    tq, cw = SB_TQ, CHAIN_W
    n_chains = tq // cw
    i = pl.program_id(2)
    qt = q_ref[...]
    q_chain = [qt[:, c * cw:(c + 1) * cw] for c in range(n_chains)]

    krow = lax.broadcasted_iota(jnp.int32, (cw, cw), 0)
    qcol = lax.broadcasted_iota(jnp.int32, (cw, cw), 1)
    strict = krow < qcol
    lrow = lax.broadcasted_iota(jnp.int32, (cw + SUM_ROWS, cw), 0)
    lcol = lax.broadcasted_iota(jnp.int32, (cw + SUM_ROWS, cw), 1)
    later = jnp.where(jnp.logical_or(lcol > lrow, lrow >= cw), 1.0, 0.0).astype(jnp.bfloat16)

    c_ref[...] = jnp.zeros_like(c_ref)
    acc_ref[...] = jnp.zeros_like(acc_ref)

    def run_pieces(pieces):
        def scores(t, st, nxt):
            sub, c, _ = pieces[t]
            start = pl.multiple_of(sub * cw, cw)
            st["z"] = jnp.dot(k_ref[0, pl.ds(start, cw), :], q_chain[c],
                              preferred_element_type=jnp.float32)

        def suffix(t, st, nxt):
            _, _, triangular = pieces[t]
            z = st.pop("z")
            u = jnp.maximum(z, 0.0) + jnp.log2(1.0 + jnp.exp2(-jnp.abs(z)))
            st["log_sig"] = z - u
            if triangular:
                u = jnp.where(strict, u, 0.0)
            st["tail"] = jnp.dot(later, u.astype(jnp.bfloat16),
                                 preferred_element_type=jnp.float32)

        def weights_pv(t, st, nxt):
            sub, _, triangular = pieces[t]
            start = pl.multiple_of(sub * cw, cw)
            tail = st.pop("tail")
            st["usum"] = tail[cw:cw + 1]
            a = jnp.exp2(st.pop("log_sig") - tail[:cw])
            if triangular:
                a = jnp.where(strict, a, 0.0)
            st["pv"] = jnp.dot(vt_ref[:, pl.ds(start, cw)], a.astype(jnp.bfloat16),
                               preferred_element_type=jnp.float32)

        def accumulate(t, st, nxt):
            _, c, _ = pieces[t]
            lanes = slice(c * cw, (c + 1) * cw)
            carry = c_ref[:, lanes]
            acc_ref[:, lanes] += st.pop("pv") * jnp.exp2(-carry)
            c_ref[:, lanes] = carry + st.pop("usum")

        _emit_pipelined((scores, suffix, weights_pv, accumulate), len(pieces))

    diag = [n_chains * i + c for c in range(n_chains)]
    head = [(diag[c], c, True) for c in reversed(range(n_chains))]
    second = [(diag[c] - 1, c, False) for c in reversed(range(n_chains))]

    @pl.when(i == 0)
    def _():
        run_pieces(head + [p for p in second if p[1] > 0])

    @pl.when(i > 0)
    def _():
        run_pieces(head + second)

    def unfinished(c, depth):
        lanes = slice(c * cw, (c + 1) * cw)
        return jnp.logical_and(diag[c] - depth >= 0,
                               jnp.min(c_ref[:, lanes]) < SB_DONE_LOG2)

    def any_unfinished(depth):
        go = unfinished(0, depth)
        for c in range(1, n_chains):
            go = jnp.logical_or(go, unfinished(c, depth))
        return go

    def body(carry):
        depth, _ = carry
        for c in range(n_chains):
            @pl.when(unfinished(c, depth))
            def _():
                run_pieces([(diag[c] - depth, c, False)])
        return depth + 1, any_unfinished(depth + 1)

    lax.while_loop(lambda carry: carry[1], body, (jnp.int32(2), any_unfinished(2)))
    o_ref[0] = acc_ref[...].T.astype(o_ref.dtype)


def _mix_ffn2_kernel(h_ref, a_ref, b_ref, gate_ref, wa_ref, wb_ref, wout_ref,
                     norm2_ref, wg_ref, wu_ref, wd_ref, normf_ref, o_ref):
    ya = jnp.dot(a_ref[...], wa_ref[...], preferred_element_type=jnp.float32)
    yb = jnp.dot(b_ref[...], wb_ref[...], preferred_element_type=jnp.float32)
    gate = gate_ref[...].astype(jnp.float32)
    y = (gate[:, :D_MODEL] * ya + gate[:, D_MODEL:] * yb).astype(jnp.bfloat16)
    h2 = h_ref[...] + jnp.dot(y, wout_ref[...], preferred_element_type=jnp.float32)
    h3 = _swiglu_half_step(h2, norm2_ref[...], wg_ref, wu_ref, wd_ref)
    o_ref[...] = _rms(h3, normf_ref[...])


def _const_spec(shape):
    return pl.BlockSpec(shape, lambda *_: (0,) * len(shape))


ATTN_FLAGS = None


def _tc_params(n_axes, flags=None):
    return pltpu.CompilerParams(dimension_semantics=("arbitrary",) * n_axes,
                                vmem_limit_bytes=VMEM_LIMIT_BYTES, flags=flags)


def kernel(x, ffn1_norm, ffn1_w_gate, ffn1_w_up, ffn1_w_down, mix_norm, w_in, b_gate, lambda_q1, lambda_k1, lambda_q2, lambda_k2, diff_subln, w_branch_diff, w_branch_sb, w_out, ffn2_norm, ffn2_w_gate, ffn2_w_up, ffn2_w_down, final_norm):
    B, S, D = x.shape
    T = B * S
    f32, bf16 = jnp.float32, jnp.bfloat16
    xt = x.reshape(T, D)
    row = lambda v: v.reshape(1, -1).astype(f32)

    tok_spec = pl.BlockSpec((FFN_TM, D), lambda t: (t, 0))
    h1 = pl.pallas_call(
        _ffn1_kernel,
        grid=(T // FFN_TM,),
        in_specs=[tok_spec, _const_spec((1, D)), _const_spec((D, D_FF)),
                  _const_spec((D, D_FF)), _const_spec((D_FF, D))],
        out_specs=tok_spec,
        out_shape=jax.ShapeDtypeStruct((T, D), f32),
        compiler_params=_tc_params(1),
        name="ffn1",
    )(xt, row(ffn1_norm[0]), ffn1_w_gate[0].astype(bf16), ffn1_w_up[0].astype(bf16),
      ffn1_w_down[0].astype(bf16))

    w = w_in[0]
    w_k = jnp.concatenate([w[:, ATT_W:2 * ATT_W], w[:, 4 * ATT_W:5 * ATT_W]], axis=1).astype(bf16)
    w_qvt = jnp.concatenate([w[:, 0:ATT_W], w[:, 3 * ATT_W:4 * ATT_W],
                             w[:, 2 * ATT_W:3 * ATT_W], w[:, 5 * ATT_W:6 * ATT_W]],
                            axis=1).T.astype(bf16)
    w_gate = w[:, 6 * ATT_W:].astype(bf16)
    rowscale = jnp.ones((QVT_W,), f32)
    rowscale = rowscale.at[0:ATT_W].set(DA_QK_DIM ** -0.5 * LOG2E)
    rowscale = rowscale.at[ATT_W:2 * ATT_W].set(HEAD_W ** -0.5 * LOG2E)
    k, qvt, gates = pl.pallas_call(
        _in_proj_kernel,
        grid=(T // PROJ_TM,),
        in_specs=[pl.BlockSpec((PROJ_TM, D), lambda t: (t, 0)), _const_spec((1, D)),
                  _const_spec((D, K_W)), _const_spec((QVT_W, D)), _const_spec((D, GATE_W)),
                  _const_spec((QVT_W, 1)), _const_spec((1, GATE_W))],
        out_specs=[pl.BlockSpec((PROJ_TM, K_W), lambda t: (t, 0)),
                   pl.BlockSpec((QVT_W, PROJ_TM), lambda t: (0, t)),
                   pl.BlockSpec((PROJ_TM, GATE_W), lambda t: (t, 0))],
        out_shape=[jax.ShapeDtypeStruct((T, K_W), bf16),
                   jax.ShapeDtypeStruct((QVT_W, T), bf16),
                   jax.ShapeDtypeStruct((T, GATE_W), bf16)],
        compiler_params=_tc_params(1),
        name="in_proj",
    )(h1, row(mix_norm[0]), w_k, w_qvt, w_gate, rowscale.reshape(-1, 1), row(b_gate[0]))
    k3 = k.reshape(B, S, K_W)

    def q_spec(slab, tq):
        return pl.BlockSpec((HEAD_W, tq),
                            lambda b, h, i: (slab * N_HEADS + h, b * (S // tq) + i))

    def k_spec(slab):
        return pl.BlockSpec((1, S, HEAD_W), lambda b, h, i: (b, 0, slab * N_HEADS + h))

    def vt_spec(slab):
        return pl.BlockSpec((HEAD_W, S), lambda b, h, i: (slab * N_HEADS + h, b))

    def att_out_spec(tq):
        return pl.BlockSpec((1, tq, HEAD_W), lambda b, h, i: (b, i, h))

    att_out_shape = jax.ShapeDtypeStruct((B, S, ATT_W), bf16)
    smem_spec = pl.BlockSpec(memory_space=pltpu.SMEM)

    lam = (jnp.exp(jnp.sum(lambda_q1[0].astype(f32) * lambda_k1[0].astype(f32)))
           - jnp.exp(jnp.sum(lambda_q2[0].astype(f32) * lambda_k2[0].astype(f32)))
           + LAMBDA_INIT).reshape(1)
    slopes = jnp.exp2(-8.0 * jnp.arange(1, N_HEADS + 1, dtype=f32) / N_HEADS)

    a = pl.pallas_call(
        _diff_attn_kernel,
        grid=(B, N_HEADS, S // ATT_TQ),
        in_specs=[smem_spec, smem_spec, q_spec(0, ATT_TQ), k_spec(0), vt_spec(2),
                  _const_spec((HEAD_W, 1))],
        out_specs=att_out_spec(ATT_TQ),
        out_shape=att_out_shape,
        scratch_shapes=[pltpu.VMEM((ATT_TK, ATT_TQ), f32),
                        pltpu.VMEM((2, 2 * ATT_TQ // CHAIN_W, ATT_TK, CHAIN_W), f32),
                        pltpu.VMEM((1, 2 * ATT_TQ), f32), pltpu.VMEM((1, 2 * ATT_TQ), f32),
                        pltpu.VMEM((HEAD_W, 2 * ATT_TQ), f32)],
        compiler_params=_tc_params(3, ATTN_FLAGS),
        name="diff_attn",
    )(slopes, lam, qvt, k3, qvt, diff_subln[0].reshape(-1, 1).astype(f32))

    b = pl.pallas_call(
        _sb_attn_kernel,
        grid=(B, N_HEADS, S // SB_TQ),
        in_specs=[q_spec(1, SB_TQ), k_spec(1), vt_spec(3)],
        out_specs=att_out_spec(SB_TQ),
        out_shape=att_out_shape,
        scratch_shapes=[pltpu.VMEM((1, SB_TQ), f32), pltpu.VMEM((HEAD_W, SB_TQ), f32)],
        compiler_params=_tc_params(3, ATTN_FLAGS),
        name="sb_attn",
    )(qvt, k3, qvt)

    out = pl.pallas_call(
        _mix_ffn2_kernel,
        grid=(T // FFN_TM,),
        in_specs=[tok_spec,
                  pl.BlockSpec((FFN_TM, ATT_W), lambda t: (t, 0)),
                  pl.BlockSpec((FFN_TM, ATT_W), lambda t: (t, 0)),
                  pl.BlockSpec((FFN_TM, GATE_W), lambda t: (t, 0)),
                  _const_spec((ATT_W, D)), _const_spec((ATT_W, D)),
                  _const_spec((D, D)), _const_spec((1, D)), _const_spec((D, D_FF)),
                  _const_spec((D, D_FF)), _const_spec((D_FF, D)), _const_spec((1, D))],
        out_specs=tok_spec,
        out_shape=jax.ShapeDtypeStruct((T, D), f32),
        compiler_params=_tc_params(1),
        name="mix_ffn2",
    )(h1, a.reshape(T, -1), b.reshape(T, -1), gates,
      w_branch_diff[0].astype(bf16), w_branch_sb[0].astype(bf16), w_out[0].astype(bf16),
      row(ffn2_norm[0]), ffn2_w_gate[0].astype(bf16), ffn2_w_up[0].astype(bf16),
      ffn2_w_down[0].astype(bf16), row(final_norm))
    return out.reshape(B, S, D)
```

```python
import math

import jax
import jax.numpy as jnp
from jax import lax
from jax.experimental import pallas as pl
from jax.experimental.pallas import tpu as pltpu

D_MODEL = 1024
D_FF = 2816
N_HEADS = 4
HEAD_W = 128
DA_QK_DIM = 64
ATT_W = N_HEADS * HEAD_W
K_W = 2 * ATT_W
QVT_W = 4 * ATT_W
GATE_W = 2 * D_MODEL
NORM_EPS = 1e-5
LAMBDA_INIT = 0.8 - 0.6 * math.exp(-0.3 * 0)
LOG2E = 1.0 / math.log(2.0)

VMEM_LIMIT_BYTES = 56 * 1024 * 1024

FFN_TM = 512
PROJ_TM = 512
ATT_TQ = 512
ATT_TK = 512
SB_TQ = 2048
CHAIN_W = 256
SB_DONE_LOG2 = 160.0
SUM_ROWS = 16

_NT = (((1,), (1,)), ((), ()))


def _rms(x, g):
    ms = jnp.mean(x * x, axis=-1, keepdims=True)
    return x * lax.rsqrt(ms + NORM_EPS) * g


def _swiglu_half_step(x, norm_g, wg_ref, wu_ref, wd_ref):
    xn = _rms(x, norm_g).astype(jnp.bfloat16)
    g = jnp.dot(xn, wg_ref[...], preferred_element_type=jnp.float32)
    u = jnp.dot(xn, wu_ref[...], preferred_element_type=jnp.float32)
    hact = (g * jax.nn.sigmoid(g) * u).astype(jnp.bfloat16)
    return x + 0.5 * jnp.dot(hact, wd_ref[...], preferred_element_type=jnp.float32)


def _ffn1_kernel(x_ref, norm_ref, wg_ref, wu_ref, wd_ref, o_ref):
    o_ref[...] = _swiglu_half_step(x_ref[...], norm_ref[...], wg_ref, wu_ref, wd_ref)


def _in_proj_kernel(h_ref, norm_ref, wk_ref, wqvt_ref, wgate_ref, rowscale_ref, bgate_ref,
                    k_ref, qvt_ref, gate_ref):
    n = _rms(h_ref[...], norm_ref[...]).astype(jnp.bfloat16)
    k_ref[...] = jnp.dot(n, wk_ref[...], preferred_element_type=jnp.float32).astype(jnp.bfloat16)
    qvt = lax.dot_general(wqvt_ref[...], n, _NT, preferred_element_type=jnp.float32)
    qvt_ref[...] = (qvt * rowscale_ref[...]).astype(jnp.bfloat16)
    g = jnp.dot(n, wgate_ref[...], preferred_element_type=jnp.float32)
    gate_ref[...] = jax.nn.sigmoid(g + bgate_ref[...]).astype(jnp.bfloat16)


def _emit_pipelined(stages, n):
    state = [dict() for _ in range(n)]
    for step in range(n + len(stages) - 1):
        for s, stage in enumerate(stages):
            t = step - s
            if 0 <= t < n:
                stage(t, state[t], state[t + 1] if t + 1 < n else None)


def _ordered_after(x, token):
    zero = lax.shift_right_logical(
        lax.shift_right_logical(pltpu.bitcast(token, jnp.uint32), jnp.uint32(16)), jnp.uint32(16))
    return pltpu.bitcast(pltpu.bitcast(x, jnp.uint32) + zero, jnp.float32)


def _diff_attn_kernel(slope_ref, lam_ref, q_ref, k_ref, vt_ref, subln_ref, o_ref,
                      mask_ref, s_ref, m_ref, l_ref, acc_ref):
    tq, tk, cw = ATT_TQ, ATT_TK, CHAIN_W
    per_map = tq // cw
    n_chains = 2 * per_map
    h = pl.program_id(1)
    i = pl.program_id(2)
    slope = slope_ref[h] * LOG2E
    lam = lam_ref[0]

    qt = q_ref[...]
    chan = lax.broadcasted_iota(jnp.int32, (HEAD_W, tq), 0)
    zero = jnp.zeros_like(qt)
    q_maps = (jnp.where(chan < DA_QK_DIM, qt, zero), jnp.where(chan >= DA_QK_DIM, qt, zero))

    sl = jnp.full((HEAD_W, cw), slope, jnp.float32)
    hi = sl.astype(jnp.bfloat16).astype(jnp.float32)
    mid = (sl - hi).astype(jnp.bfloat16).astype(jnp.float32)
    lo = sl - hi - mid
    frow = lax.broadcasted_iota(jnp.int32, (HEAD_W, cw), 0)
    part = frow % 3
    q_feat = jnp.where(frow < 6, jnp.where(part == 0, hi, jnp.where(part == 1, mid, lo)),
                       0.0).astype(jnp.bfloat16)
    q_chain = [jnp.concatenate(
        [q_maps[c // per_map][:, (c % per_map) * cw:(c % per_map + 1) * cw], q_feat], axis=0)
        for c in range(n_chains)]

    kpos = lax.broadcasted_iota(jnp.int32, (tk, HEAD_W), 0)
    klane = lax.broadcasted_iota(jnp.int32, (tk, HEAD_W), 1)
    k_hi = jnp.where(kpos >= 256, 256, 0)
    k_feat = jnp.where(klane < 3, k_hi, jnp.where(klane < 6, kpos - k_hi, 0)
                       ).astype(jnp.float32).astype(jnp.bfloat16)

    krow = lax.broadcasted_iota(jnp.int32, (tk, tq), 0)
    qcol = lax.broadcasted_iota(jnp.int32, (tk, tq), 1)
    mask_ref[...] = jnp.where(qcol >= krow, 0.0, -jnp.inf)

    m_ref[...] = jnp.full_like(m_ref, -jnp.inf)
    l_ref[...] = jnp.zeros_like(l_ref)
    acc_ref[...] = jnp.zeros_like(acc_ref)

    def scores_to(slot, j):
        kb = k_ref[0, pl.ds(pl.multiple_of(j * tk, tk), tk), :]
        kb = jnp.concatenate([kb, k_feat], axis=1)
        for c in range(n_chains):
            s_ref[slot, c] = jnp.dot(kb, q_chain[c], preferred_element_type=jnp.float32)

    def consume(slot, j, diagonal):
        vtb = vt_ref[:, pl.ds(pl.multiple_of(j * tk, tk), tk)]
        shift = -slope * ((i - j) * tq).astype(jnp.float32)

        def masked_scores(c):
            s = s_ref[slot, c]
            if diagonal:
                qp = c % per_map
                s = s + mask_ref[:, qp * cw:(qp + 1) * cw]
            return s

        def column_max(c, st, nxt):
            st["cmax"] = jnp.max(masked_scores(c), axis=0, keepdims=True) + shift

        def softmax_pv(c, st, nxt):
            lanes = slice(c * cw, (c + 1) * cw)
            m_prev = m_ref[:, lanes]
            m_new = jnp.maximum(m_prev, st.pop("cmax"))
            st["alpha"] = jnp.exp2(m_prev - m_new)
            p = jnp.exp2(masked_scores(c) - (m_new - shift))
            l_ref[:, lanes] = st["alpha"] * l_ref[:, lanes] + jnp.sum(p, axis=0, keepdims=True)
            m_ref[:, lanes] = m_new
            st["pv"] = jnp.dot(vtb, p.astype(jnp.bfloat16),
                               preferred_element_type=jnp.float32)

        def accumulate(c, st, nxt):
            lanes = slice(c * cw, (c + 1) * cw)
            acc_ref[:, lanes] = st.pop("alpha") * acc_ref[:, lanes] + st.pop("pv")

        _emit_pipelined((column_max, softmax_pv, accumulate), n_chains)

    def step(slot, j):
        scores_to(1 - slot, j + 1)
        consume(slot, j, diagonal=False)

    scores_to(0, 0)

    def pair(jj, carry):
        step(0, 2 * jj)
        step(1, 2 * jj + 1)
        return carry

    lax.fori_loop(0, i // 2, pair, 0)

    @pl.when(i % 2 == 0)
    def _():
        consume(0, i, diagonal=True)

    @pl.when(i % 2 == 1)
    def _():
        step(0, i - 1)
        consume(1, i, diagonal=True)

    o = acc_ref[...] / l_ref[...]
    a = o[:, :tq] - lam * o[:, tq:]
    ms = jnp.mean(a * a, axis=0, keepdims=True)
    a = a * lax.rsqrt(ms + NORM_EPS) * subln_ref[...] * (1.0 - LAMBDA_INIT)
    o_ref[0] = a.T.astype(o_ref.dtype)


def _sb_attn_kernel(q_ref, k_ref, vt_ref, o_ref, c_ref, acc_ref):
    tq, cw = SB_TQ, CHAIN_W
    n_chains = tq // cw
    i = pl.program_id(2)
    qt = q_ref[...]
    q_chain = [qt[:, c * cw:(c + 1) * cw] for c in range(n_chains)]

    krow = lax.broadcasted_iota(jnp.int32, (cw, cw), 0)
    qcol = lax.broadcasted_iota(jnp.int32, (cw, cw), 1)
    strict = krow < qcol
    lrow = lax.broadcasted_iota(jnp.int32, (cw + SUM_ROWS, cw), 0)
    lcol = lax.broadcasted_iota(jnp.int32, (cw + SUM_ROWS, cw), 1)
    later = jnp.where(jnp.logical_or(lcol > lrow, lrow >= cw), 1.0, 0.0).astype(jnp.bfloat16)

    c_ref[...] = jnp.zeros_like(c_ref)
    acc_ref[...] = jnp.zeros_like(acc_ref)

    def run_pieces(pieces):
        def scores(t, st, nxt):
            sub, c, _ = pieces[t]
            start = pl.multiple_of(sub * cw, cw)
            st["z"] = jnp.dot(k_ref[0, pl.ds(start, cw), :], q_chain[c],
                              preferred_element_type=jnp.float32)

        def suffix(t, st, nxt):
            _, _, triangular = pieces[t]
            z = st.pop("z")
            u = jnp.maximum(z, 0.0) + jnp.log2(1.0 + jnp.exp2(-jnp.abs(z)))
            st["log_sig"] = z - u
            if triangular:
                u = jnp.where(strict, u, 0.0)
            st["tail"] = jnp.dot(later, u.astype(jnp.bfloat16),
                                 preferred_element_type=jnp.float32)

        def weights_pv(t, st, nxt):
            sub, _, triangular = pieces[t]
            start = pl.multiple_of(sub * cw, cw)
            tail = st.pop("tail")
            st["usum"] = tail[cw:cw + 1]
            a = jnp.exp2(st.pop("log_sig") - tail[:cw])
            if triangular:
                a = jnp.where(strict, a, 0.0)
            st["pv"] = jnp.dot(vt_ref[:, pl.ds(start, cw)], a.astype(jnp.bfloat16),
                               preferred_element_type=jnp.float32)

        def accumulate(t, st, nxt):
            _, c, _ = pieces[t]
            lanes = slice(c * cw, (c + 1) * cw)
            carry = c_ref[:, lanes]
            acc_ref[:, lanes] += st.pop("pv") * jnp.exp2(-carry)
            c_ref[:, lanes] = carry + st.pop("usum")

        _emit_pipelined((scores, suffix, weights_pv, accumulate), len(pieces))

    diag = [n_chains * i + c for c in range(n_chains)]
    head = [(diag[c], c, True) for c in reversed(range(n_chains))]
    second = [(diag[c] - 1, c, False) for c in reversed(range(n_chains))]

    @pl.when(i == 0)
    def _():
        run_pieces(head + [p for p in second if p[1] > 0])

    @pl.when(i > 0)
    def _():
        run_pieces(head + second)

    def unfinished(c, depth):
        lanes = slice(c * cw, (c + 1) * cw)
        return jnp.logical_and(diag[c] - depth >= 0,
                               jnp.min(c_ref[:, lanes]) < SB_DONE_LOG2)

    def any_unfinished(depth):
        go = unfinished(0, depth)
        for c in range(1, n_chains):
            go = jnp.logical_or(go, unfinished(c, depth))
        return go

    def body(carry):
        depth, _ = carry
        for c in range(n_chains):
            @pl.when(unfinished(c, depth))
            def _():
                run_pieces([(diag[c] - depth, c, False)])
        return depth + 1, any_unfinished(depth + 1)

    lax.while_loop(lambda carry: carry[1], body, (jnp.int32(2), any_unfinished(2)))
    o_ref[0] = acc_ref[...].T.astype(o_ref.dtype)


def _mix_ffn2_kernel(h_ref, a_ref, b_ref, gate_ref, wa_ref, wb_ref, wout_ref,
                     norm2_ref, wg_ref, wu_ref, wd_ref, normf_ref, o_ref):
    ya = jnp.dot(a_ref[...], wa_ref[...], preferred_element_type=jnp.float32)
    yb = jnp.dot(b_ref[...], wb_ref[...], preferred_element_type=jnp.float32)
    gate = gate_ref[...].astype(jnp.float32)
    y = (gate[:, :D_MODEL] * ya + gate[:, D_MODEL:] * yb).astype(jnp.bfloat16)
    h2 = h_ref[...] + jnp.dot(y, wout_ref[...], preferred_element_type=jnp.float32)
    h3 = _swiglu_half_step(h2, norm2_ref[...], wg_ref, wu_ref, wd_ref)
    o_ref[...] = _rms(h3, normf_ref[...])


def _const_spec(shape):
    return pl.BlockSpec(shape, lambda *_: (0,) * len(shape), pipeline_mode=pl.Buffered(1))


ATTN_FLAGS = None


def _tc_params(n_axes, flags=None):
    return pltpu.CompilerParams(dimension_semantics=("arbitrary",) * n_axes,
                                vmem_limit_bytes=VMEM_LIMIT_BYTES, flags=flags)


def kernel(x, ffn1_norm, ffn1_w_gate, ffn1_w_up, ffn1_w_down, mix_norm, w_in, b_gate, lambda_q1, lambda_k1, lambda_q2, lambda_k2, diff_subln, w_branch_diff, w_branch_sb, w_out, ffn2_norm, ffn2_w_gate, ffn2_w_up, ffn2_w_down, final_norm):
    B, S, D = x.shape
    T = B * S
    f32, bf16 = jnp.float32, jnp.bfloat16
    xt = x.reshape(T, D)
    row = lambda v: v.reshape(1, -1).astype(f32)

    tok_spec = pl.BlockSpec((FFN_TM, D), lambda t: (t, 0))
    h1 = pl.pallas_call(
        _ffn1_kernel,
        grid=(T // FFN_TM,),
        in_specs=[tok_spec, _const_spec((1, D)), _const_spec((D, D_FF)),
                  _const_spec((D, D_FF)), _const_spec((D_FF, D))],
        out_specs=tok_spec,
        out_shape=jax.ShapeDtypeStruct((T, D), f32),
        compiler_params=_tc_params(1),
        name="ffn1",
    )(xt, row(ffn1_norm[0]), ffn1_w_gate[0].astype(bf16), ffn1_w_up[0].astype(bf16),
      ffn1_w_down[0].astype(bf16))

    w = w_in[0]
    w_k = jnp.concatenate([w[:, ATT_W:2 * ATT_W], w[:, 4 * ATT_W:5 * ATT_W]], axis=1).astype(bf16)
    w_qvt = jnp.concatenate([w[:, 0:ATT_W], w[:, 3 * ATT_W:4 * ATT_W],
                             w[:, 2 * ATT_W:3 * ATT_W], w[:, 5 * ATT_W:6 * ATT_W]],
                            axis=1).T.astype(bf16)
    w_gate = w[:, 6 * ATT_W:].astype(bf16)
    rowscale = jnp.ones((QVT_W,), f32)
    rowscale = rowscale.at[0:ATT_W].set(DA_QK_DIM ** -0.5 * LOG2E)
    rowscale = rowscale.at[ATT_W:2 * ATT_W].set(HEAD_W ** -0.5 * LOG2E)
    k, qvt, gates = pl.pallas_call(
        _in_proj_kernel,
        grid=(T // PROJ_TM,),
        in_specs=[pl.BlockSpec((PROJ_TM, D), lambda t: (t, 0)), _const_spec((1, D)),
                  _const_spec((D, K_W)), _const_spec((QVT_W, D)), _const_spec((D, GATE_W)),
                  _const_spec((QVT_W, 1)), _const_spec((1, GATE_W))],
        out_specs=[pl.BlockSpec((PROJ_TM, K_W), lambda t: (t, 0)),
                   pl.BlockSpec((QVT_W, PROJ_TM), lambda t: (0, t)),
                   pl.BlockSpec((PROJ_TM, GATE_W), lambda t: (t, 0))],
        out_shape=[jax.ShapeDtypeStruct((T, K_W), bf16),
                   jax.ShapeDtypeStruct((QVT_W, T), bf16),
                   jax.ShapeDtypeStruct((T, GATE_W), bf16)],
        compiler_params=_tc_params(1),
        name="in_proj",
    )(h1, row(mix_norm[0]), w_k, w_qvt, w_gate, rowscale.reshape(-1, 1), row(b_gate[0]))
    k3 = k.reshape(B, S, K_W)

    def q_spec(slab, tq):
        return pl.BlockSpec((HEAD_W, tq),
                            lambda b, h, i: (slab * N_HEADS + h, b * (S // tq) + i))

    def k_spec(slab):
        return pl.BlockSpec((1, S, HEAD_W), lambda b, h, i: (b, 0, slab * N_HEADS + h))

    def vt_spec(slab):
        return pl.BlockSpec((HEAD_W, S), lambda b, h, i: (slab * N_HEADS + h, b))

    def att_out_spec(tq):
        return pl.BlockSpec((1, tq, HEAD_W), lambda b, h, i: (b, i, h))

    att_out_shape = jax.ShapeDtypeStruct((B, S, ATT_W), bf16)
    smem_spec = pl.BlockSpec(memory_space=pltpu.SMEM)

    lam = (jnp.exp(jnp.sum(lambda_q1[0].astype(f32) * lambda_k1[0].astype(f32)))
           - jnp.exp(jnp.sum(lambda_q2[0].astype(f32) * lambda_k2[0].astype(f32)))
           + LAMBDA_INIT).reshape(1)
    slopes = jnp.exp2(-8.0 * jnp.arange(1, N_HEADS + 1, dtype=f32) / N_HEADS)

    a = pl.pallas_call(
        _diff_attn_kernel,
        grid=(B, N_HEADS, S // ATT_TQ),
        in_specs=[smem_spec, smem_spec, q_spec(0, ATT_TQ), k_spec(0), vt_spec(2),
                  _const_spec((HEAD_W, 1))],
        out_specs=att_out_spec(ATT_TQ),
        out_shape=att_out_shape,
        scratch_shapes=[pltpu.VMEM((ATT_TK, ATT_TQ), f32),
                        pltpu.VMEM((2, 2 * ATT_TQ // CHAIN_W, ATT_TK, CHAIN_W), f32),
                        pltpu.VMEM((1, 2 * ATT_TQ), f32), pltpu.VMEM((1, 2 * ATT_TQ), f32),
                        pltpu.VMEM((HEAD_W, 2 * ATT_TQ), f32)],
        compiler_params=_tc_params(3, ATTN_FLAGS),
        name="diff_attn",
    )(slopes, lam, qvt, k3, qvt, diff_subln[0].reshape(-1, 1).astype(f32))

    b = pl.pallas_call(
        _sb_attn_kernel,
        grid=(B, N_HEADS, S // SB_TQ),
        in_specs=[q_spec(1, SB_TQ), k_spec(1), vt_spec(3)],
        out_specs=att_out_spec(SB_TQ),
        out_shape=att_out_shape,
        scratch_shapes=[pltpu.VMEM((1, SB_TQ), f32), pltpu.VMEM((HEAD_W, SB_TQ), f32)],
        compiler_params=_tc_params(3, ATTN_FLAGS),
        name="sb_attn",
    )(qvt, k3, qvt)

    out = pl.pallas_call(
        _mix_ffn2_kernel,
        grid=(T // FFN_TM,),
        in_specs=[tok_spec,
                  pl.BlockSpec((FFN_TM, ATT_W), lambda t: (t, 0)),
                  pl.BlockSpec((FFN_TM, ATT_W), lambda t: (t, 0)),
                  pl.BlockSpec((FFN_TM, GATE_W), lambda t: (t, 0)),
                  _const_spec((ATT_W, D)), _const_spec((ATT_W, D)),
                  _const_spec((D, D)), _const_spec((1, D)), _const_spec((D, D_FF)),
                  _const_spec((D, D_FF)), _const_spec((D_FF, D)), _const_spec((1, D))],
        out_specs=tok_spec,
        out_shape=jax.ShapeDtypeStruct((T, D), f32),
        compiler_params=_tc_params(1),
        name="mix_ffn2",
    )(h1, a.reshape(T, -1), b.reshape(T, -1), gates,
      w_branch_diff[0].astype(bf16), w_branch_sb[0].astype(bf16), w_out[0].astype(bf16),
      row(ffn2_norm[0]), ffn2_w_gate[0].astype(bf16), ffn2_w_up[0].astype(bf16),
      ffn2_w_down[0].astype(bf16), row(final_norm))
    return out.reshape(B, S, D)
```

```python
import math

import jax
import jax.numpy as jnp
from jax import lax
from jax.experimental import pallas as pl
from jax.experimental.pallas import tpu as pltpu

D_MODEL = 1024
D_FF = 2816
N_HEADS = 4
HEAD_W = 128
DA_QK_DIM = 64
ATT_W = N_HEADS * HEAD_W
K_W = 2 * ATT_W
QVT_W = 4 * ATT_W
GATE_W = 2 * D_MODEL
NORM_EPS = 1e-5
LAMBDA_INIT = 0.8 - 0.6 * math.exp(-0.3 * 0)
LOG2E = 1.0 / math.log(2.0)

VMEM_LIMIT_BYTES = 56 * 1024 * 1024

FFN_TM = 512
PROJ_TM = 512
ATT_TQ = 1024
ATT_TK = 512
SB_TQ = 2048
CHAIN_W = 256
SB_DONE_LOG2 = 160.0
SUM_ROWS = 16

_NT = (((1,), (1,)), ((), ()))


def _rms(x, g):
    ms = jnp.mean(x * x, axis=-1, keepdims=True)
    return x * lax.rsqrt(ms + NORM_EPS) * g


def _swiglu_half_step(x, norm_g, wg_ref, wu_ref, wd_ref):
    xn = _rms(x, norm_g).astype(jnp.bfloat16)
    g = jnp.dot(xn, wg_ref[...], preferred_element_type=jnp.float32)
    u = jnp.dot(xn, wu_ref[...], preferred_element_type=jnp.float32)
    hact = (g * jax.nn.sigmoid(g) * u).astype(jnp.bfloat16)
    return x + 0.5 * jnp.dot(hact, wd_ref[...], preferred_element_type=jnp.float32)


def _ffn1_kernel(x_ref, norm_ref, wg_ref, wu_ref, wd_ref, o_ref):
    o_ref[...] = _swiglu_half_step(x_ref[...], norm_ref[...], wg_ref, wu_ref, wd_ref)


def _in_proj_kernel(h_ref, norm_ref, wk_ref, wqvt_ref, wgate_ref, rowscale_ref, bgate_ref,
                    k_ref, qvt_ref, gate_ref):
    n = _rms(h_ref[...], norm_ref[...]).astype(jnp.bfloat16)
    k_ref[...] = jnp.dot(n, wk_ref[...], preferred_element_type=jnp.float32).astype(jnp.bfloat16)
    qvt = lax.dot_general(wqvt_ref[...], n, _NT, preferred_element_type=jnp.float32)
    qvt_ref[...] = (qvt * rowscale_ref[...]).astype(jnp.bfloat16)
    g = jnp.dot(n, wgate_ref[...], preferred_element_type=jnp.float32)
    gate_ref[...] = jax.nn.sigmoid(g + bgate_ref[...]).astype(jnp.bfloat16)


def _emit_pipelined(stages, n):
    state = [dict() for _ in range(n)]
    for step in range(n + len(stages) - 1):
        for s, stage in enumerate(stages):
            t = step - s
            if 0 <= t < n:
                stage(t, state[t], state[t + 1] if t + 1 < n else None)


def _ordered_after(x, token):
    zero = lax.shift_right_logical(
        lax.shift_right_logical(pltpu.bitcast(token, jnp.uint32), jnp.uint32(16)), jnp.uint32(16))
    return pltpu.bitcast(pltpu.bitcast(x, jnp.uint32) + zero, jnp.float32)


def _diff_attn_kernel(slope_ref, lam_ref, q_ref, k_ref, vt_ref, subln_ref, o_ref,
                      mask_ref, kfeat_ref, s_ref, m_ref, l_ref, acc_ref):
    tq, tk, cw = ATT_TQ, ATT_TK, CHAIN_W
    per_map = tq // cw
    n_chains = 2 * per_map
    n_diag = tq // tk
    assert n_diag % 2 == 0 and tk == 2 * cw
    h = pl.program_id(1)
    i = pl.program_id(2)
    slope = slope_ref[h] * LOG2E
    lam = lam_ref[0]

    @pl.when(jnp.logical_and(pl.program_id(0) == 0, jnp.logical_and(h == 0, i == 0)))
    def _():
        krow = lax.broadcasted_iota(jnp.int32, (tk, tk), 0)
        qcol = lax.broadcasted_iota(jnp.int32, (tk, tk), 1)
        mask_ref[...] = jnp.where(qcol >= krow, 0.0, -jnp.inf)
        kpos = lax.broadcasted_iota(jnp.int32, (tk, HEAD_W), 0)
        klane = lax.broadcasted_iota(jnp.int32, (tk, HEAD_W), 1)
        k_hi = jnp.where(kpos >= 256, 256, 0)
        kfeat_ref[...] = jnp.where(klane < 3, k_hi, jnp.where(klane < 6, kpos - k_hi, 0)
                                   ).astype(jnp.float32).astype(jnp.bfloat16)

    qt = q_ref[...]
    chan = lax.broadcasted_iota(jnp.int32, (HEAD_W, tq), 0)
    zero = jnp.zeros_like(qt)
    q_maps = (jnp.where(chan < DA_QK_DIM, qt, zero), jnp.where(chan >= DA_QK_DIM, qt, zero))

    sl = jnp.full((HEAD_W, cw), slope, jnp.float32)
    hi = sl.astype(jnp.bfloat16).astype(jnp.float32)
    mid = (sl - hi).astype(jnp.bfloat16).astype(jnp.float32)
    lo = sl - hi - mid
    frow = lax.broadcasted_iota(jnp.int32, (HEAD_W, cw), 0)
    part = frow % 3
    q_feat = jnp.where(frow < 6, jnp.where(part == 0, hi, jnp.where(part == 1, mid, lo)),
                       0.0).astype(jnp.bfloat16)
    q_chain = [jnp.concatenate(
        [q_maps[c // per_map][:, (c % per_map) * cw:(c % per_map + 1) * cw], q_feat], axis=0)
        for c in range(n_chains)]

    m_ref[...] = jnp.full_like(m_ref, -jnp.inf)
    l_ref[...] = jnp.zeros_like(l_ref)
    acc_ref[...] = jnp.zeros_like(acc_ref)

    def chain_mode(c, d):
        if d is None:
            return "full"
        q_lo = (c % per_map) * cw
        if q_lo + cw <= d * tk:
            return "skip"
        if q_lo >= (d + 1) * tk:
            return "full"
        return q_lo - d * tk

    def scores_to(slot, j, d=None):
        kb = k_ref[0, pl.ds(pl.multiple_of(j * tk, tk), tk), :]
        kb = jnp.concatenate([kb, kfeat_ref[...]], axis=1)
        for c in range(n_chains):
            if chain_mode(c, d) != "skip":
                s_ref[slot, c] = jnp.dot(kb, q_chain[c], preferred_element_type=jnp.float32)

    def consume(slot, j, d=None):
        chains = [c for c in range(n_chains) if chain_mode(c, d) != "skip"]
        vtb = vt_ref[:, pl.ds(pl.multiple_of(j * tk, tk), tk)]
        shift = -slope * (i * tq - j * tk).astype(jnp.float32)

        def masked_scores(c):
            s = s_ref[slot, c]
            mode = chain_mode(c, d)
            if mode != "full":
                s = s + mask_ref[:, mode:mode + cw]
            return s

        def column_max(t, st, nxt):
            st["cmax"] = jnp.max(masked_scores(chains[t]), axis=0, keepdims=True) + shift

        def softmax_pv(t, st, nxt):
            c = chains[t]
            lanes = slice(c * cw, (c + 1) * cw)
            m_prev = m_ref[:, lanes]
            m_new = jnp.maximum(m_prev, st.pop("cmax"))
            st["alpha"] = jnp.exp2(m_prev - m_new)
            p = jnp.exp2(masked_scores(c) - (m_new - shift))
            l_ref[:, lanes] = st["alpha"] * l_ref[:, lanes] + jnp.sum(p, axis=0, keepdims=True)
            m_ref[:, lanes] = m_new
            st["pv"] = jnp.dot(vtb, p.astype(jnp.bfloat16),
                               preferred_element_type=jnp.float32)

        def accumulate(t, st, nxt):
            c = chains[t]
            lanes = slice(c * cw, (c + 1) * cw)
            acc_ref[:, lanes] = st.pop("alpha") * acc_ref[:, lanes] + st.pop("pv")

        _emit_pipelined((column_max, softmax_pv, accumulate), len(chains))

    def step(slot, j):
        scores_to(1 - slot, j + 1)
        consume(slot, j)

    scores_to(0, 0)

    def pair(jj, carry):
        step(0, 2 * jj)
        step(1, 2 * jj + 1)
        return carry

    first_diag = n_diag * i
    lax.fori_loop(0, first_diag // 2, pair, 0)
    for d in range(n_diag):
        if d + 1 < n_diag:
            scores_to((d + 1) % 2, first_diag + d + 1, d + 1)
        consume(d % 2, first_diag + d, d)

    o = acc_ref[...] / l_ref[...]
    a = o[:, :tq] - lam * o[:, tq:]
    ms = jnp.mean(a * a, axis=0, keepdims=True)
    a = a * lax.rsqrt(ms + NORM_EPS) * subln_ref[...] * (1.0 - LAMBDA_INIT)
    o_ref[0] = a.T.astype(o_ref.dtype)


def _sb_attn_kernel(q_ref, k_ref, vt_ref, o_ref, c_ref, acc_ref):
    tq, cw = SB_TQ, CHAIN_W
    n_chains = tq // cw
    i = pl.program_id(2)
    qt = q_ref[...]
    q_chain = [qt[:, c * cw:(c + 1) * cw] for c in range(n_chains)]

    krow = lax.broadcasted_iota(jnp.int32, (cw, cw), 0)
    qcol = lax.broadcasted_iota(jnp.int32, (cw, cw), 1)
    strict = krow < qcol
    lrow = lax.broadcasted_iota(jnp.int32, (cw + SUM_ROWS, cw), 0)
    lcol = lax.broadcasted_iota(jnp.int32, (cw + SUM_ROWS, cw), 1)
    later = jnp.where(jnp.logical_or(lcol > lrow, lrow >= cw), 1.0, 0.0).astype(jnp.bfloat16)

    c_ref[...] = jnp.zeros_like(c_ref)
    acc_ref[...] = jnp.zeros_like(acc_ref)

    def run_pieces(pieces):
        def scores(t, st, nxt):
            sub, c, _ = pieces[t]
            start = pl.multiple_of(sub * cw, cw)
            st["z"] = jnp.dot(k_ref[0, pl.ds(start, cw), :], q_chain[c],
                              preferred_element_type=jnp.float32)

        def suffix(t, st, nxt):
            _, _, triangular = pieces[t]
            z = st.pop("z")
            u = jnp.maximum(z, 0.0) + jnp.log2(1.0 + jnp.exp2(-jnp.abs(z)))
            st["log_sig"] = z - u
            if triangular:
                u = jnp.where(strict, u, 0.0)
            st["tail"] = jnp.dot(later, u.astype(jnp.bfloat16),
                                 preferred_element_type=jnp.float32)

        def weights_pv(t, st, nxt):
            sub, _, triangular = pieces[t]
            start = pl.multiple_of(sub * cw, cw)
            tail = st.pop("tail")
            st["usum"] = tail[cw:cw + 1]
            a = jnp.exp2(st.pop("log_sig") - tail[:cw])
            if triangular:
                a = jnp.where(strict, a, 0.0)
            st["pv"] = jnp.dot(vt_ref[:, pl.ds(start, cw)], a.astype(jnp.bfloat16),
                               preferred_element_type=jnp.float32)

        def accumulate(t, st, nxt):
            _, c, _ = pieces[t]
            lanes = slice(c * cw, (c + 1) * cw)
            carry = c_ref[:, lanes]
            acc_ref[:, lanes] += st.pop("pv") * jnp.exp2(-carry)
            c_ref[:, lanes] = carry + st.pop("usum")

        _emit_pipelined((scores, suffix, weights_pv, accumulate), len(pieces))

    diag = [n_chains * i + c for c in range(n_chains)]
    head = [(diag[c], c, True) for c in reversed(range(n_chains))]
    second = [(diag[c] - 1, c, False) for c in reversed(range(n_chains))]

    @pl.when(i == 0)
    def _():
        run_pieces(head + [p for p in second if p[1] > 0])

    @pl.when(i > 0)
    def _():
        run_pieces(head + second)

    def unfinished(c, depth):
        lanes = slice(c * cw, (c + 1) * cw)
        return jnp.logical_and(diag[c] - depth >= 0,
                               jnp.min(c_ref[:, lanes]) < SB_DONE_LOG2)

    def any_unfinished(depth):
        go = unfinished(0, depth)
        for c in range(1, n_chains):
            go = jnp.logical_or(go, unfinished(c, depth))
        return go

    def body(carry):
        depth, _ = carry
        for c in range(n_chains):
            @pl.when(unfinished(c, depth))
            def _():
                run_pieces([(diag[c] - depth, c, False)])
        return depth + 1, any_unfinished(depth + 1)

    lax.while_loop(lambda carry: carry[1], body, (jnp.int32(2), any_unfinished(2)))
    o_ref[0] = acc_ref[...].T.astype(o_ref.dtype)


def _mix_ffn2_kernel(h_ref, a_ref, b_ref, gate_ref, wa_ref, wb_ref, wout_ref,
                     norm2_ref, wg_ref, wu_ref, wd_ref, normf_ref, o_ref):
    ya = jnp.dot(a_ref[...], wa_ref[...], preferred_element_type=jnp.float32)
    yb = jnp.dot(b_ref[...], wb_ref[...], preferred_element_type=jnp.float32)
    gate = gate_ref[...].astype(jnp.float32)
    y = (gate[:, :D_MODEL] * ya + gate[:, D_MODEL:] * yb).astype(jnp.bfloat16)
    h2 = h_ref[...] + jnp.dot(y, wout_ref[...], preferred_element_type=jnp.float32)
    h3 = _swiglu_half_step(h2, norm2_ref[...], wg_ref, wu_ref, wd_ref)
    o_ref[...] = _rms(h3, normf_ref[...])


def _const_spec(shape):
    return pl.BlockSpec(shape, lambda *_: (0,) * len(shape), pipeline_mode=pl.Buffered(1))


ATTN_FLAGS = None


def _tc_params(n_axes, flags=None):
    return pltpu.CompilerParams(dimension_semantics=("arbitrary",) * n_axes,
                                vmem_limit_bytes=VMEM_LIMIT_BYTES, flags=flags)


def kernel(x, ffn1_norm, ffn1_w_gate, ffn1_w_up, ffn1_w_down, mix_norm, w_in, b_gate, lambda_q1, lambda_k1, lambda_q2, lambda_k2, diff_subln, w_branch_diff, w_branch_sb, w_out, ffn2_norm, ffn2_w_gate, ffn2_w_up, ffn2_w_down, final_norm):
    B, S, D = x.shape
    T = B * S
    f32, bf16 = jnp.float32, jnp.bfloat16
    xt = x.reshape(T, D)
    row = lambda v: v.reshape(1, -1).astype(f32)

    tok_spec = pl.BlockSpec((FFN_TM, D), lambda t: (t, 0))
    h1 = pl.pallas_call(
        _ffn1_kernel,
        grid=(T // FFN_TM,),
        in_specs=[tok_spec, _const_spec((1, D)), _const_spec((D, D_FF)),
                  _const_spec((D, D_FF)), _const_spec((D_FF, D))],
        out_specs=tok_spec,
        out_shape=jax.ShapeDtypeStruct((T, D), f32),
        compiler_params=_tc_params(1),
        name="ffn1",
    )(xt, row(ffn1_norm[0]), ffn1_w_gate[0].astype(bf16), ffn1_w_up[0].astype(bf16),
      ffn1_w_down[0].astype(bf16))

    w = w_in[0]
    w_k = jnp.concatenate([w[:, ATT_W:2 * ATT_W], w[:, 4 * ATT_W:5 * ATT_W]], axis=1).astype(bf16)
    w_qvt = jnp.concatenate([w[:, 0:ATT_W], w[:, 3 * ATT_W:4 * ATT_W],
                             w[:, 2 * ATT_W:3 * ATT_W], w[:, 5 * ATT_W:6 * ATT_W]],
                            axis=1).T.astype(bf16)
    w_gate = w[:, 6 * ATT_W:].astype(bf16)
    rowscale = jnp.ones((QVT_W,), f32)
    rowscale = rowscale.at[0:ATT_W].set(DA_QK_DIM ** -0.5 * LOG2E)
    rowscale = rowscale.at[ATT_W:2 * ATT_W].set(HEAD_W ** -0.5 * LOG2E)
    k, qvt, gates = pl.pallas_call(
        _in_proj_kernel,
        grid=(T // PROJ_TM,),
        in_specs=[pl.BlockSpec((PROJ_TM, D), lambda t: (t, 0)), _const_spec((1, D)),
                  _const_spec((D, K_W)), _const_spec((QVT_W, D)), _const_spec((D, GATE_W)),
                  _const_spec((QVT_W, 1)), _const_spec((1, GATE_W))],
        out_specs=[pl.BlockSpec((PROJ_TM, K_W), lambda t: (t, 0)),
                   pl.BlockSpec((QVT_W, PROJ_TM), lambda t: (0, t)),
                   pl.BlockSpec((PROJ_TM, GATE_W), lambda t: (t, 0))],
        out_shape=[jax.ShapeDtypeStruct((T, K_W), bf16),
                   jax.ShapeDtypeStruct((QVT_W, T), bf16),
                   jax.ShapeDtypeStruct((T, GATE_W), bf16)],
        compiler_params=_tc_params(1),
        name="in_proj",
    )(h1, row(mix_norm[0]), w_k, w_qvt, w_gate, rowscale.reshape(-1, 1), row(b_gate[0]))
    k3 = k.reshape(B, S, K_W)

    def q_spec(slab, tq):
        return pl.BlockSpec((HEAD_W, tq),
                            lambda b, h, i: (slab * N_HEADS + h, b * (S // tq) + i))

    def k_spec(slab):
        return pl.BlockSpec((1, S, HEAD_W), lambda b, h, i: (b, 0, slab * N_HEADS + h))

    def vt_spec(slab):
        return pl.BlockSpec((HEAD_W, S), lambda b, h, i: (slab * N_HEADS + h, b))

    def att_out_spec(tq):
        return pl.BlockSpec((1, tq, HEAD_W), lambda b, h, i: (b, i, h))

    att_out_shape = jax.ShapeDtypeStruct((B, S, ATT_W), bf16)
    smem_spec = pl.BlockSpec(memory_space=pltpu.SMEM)

    lam = (jnp.exp(jnp.sum(lambda_q1[0].astype(f32) * lambda_k1[0].astype(f32)))
           - jnp.exp(jnp.sum(lambda_q2[0].astype(f32) * lambda_k2[0].astype(f32)))
           + LAMBDA_INIT).reshape(1)
    slopes = jnp.exp2(-8.0 * jnp.arange(1, N_HEADS + 1, dtype=f32) / N_HEADS)

    a = pl.pallas_call(
        _diff_attn_kernel,
        grid=(B, N_HEADS, S // ATT_TQ),
        in_specs=[smem_spec, smem_spec, q_spec(0, ATT_TQ), k_spec(0), vt_spec(2),
                  _const_spec((HEAD_W, 1))],
        out_specs=att_out_spec(ATT_TQ),
        out_shape=att_out_shape,
        scratch_shapes=[pltpu.VMEM((ATT_TK, ATT_TK), f32), pltpu.VMEM((ATT_TK, HEAD_W), bf16),
                        pltpu.VMEM((2, 2 * ATT_TQ // CHAIN_W, ATT_TK, CHAIN_W), f32),
                        pltpu.VMEM((1, 2 * ATT_TQ), f32), pltpu.VMEM((1, 2 * ATT_TQ), f32),
                        pltpu.VMEM((HEAD_W, 2 * ATT_TQ), f32)],
        compiler_params=_tc_params(3, ATTN_FLAGS),
        name="diff_attn",
    )(slopes, lam, qvt, k3, qvt, diff_subln[0].reshape(-1, 1).astype(f32))

    b = pl.pallas_call(
        _sb_attn_kernel,
        grid=(B, N_HEADS, S // SB_TQ),
        in_specs=[q_spec(1, SB_TQ), k_spec(1), vt_spec(3)],
        out_specs=att_out_spec(SB_TQ),
        out_shape=att_out_shape,
        scratch_shapes=[pltpu.VMEM((1, SB_TQ), f32), pltpu.VMEM((HEAD_W, SB_TQ), f32)],
        compiler_params=_tc_params(3, ATTN_FLAGS),
        name="sb_attn",
    )(qvt, k3, qvt)

    out = pl.pallas_call(
        _mix_ffn2_kernel,
        grid=(T // FFN_TM,),
        in_specs=[tok_spec,
                  pl.BlockSpec((FFN_TM, ATT_W), lambda t: (t, 0)),
                  pl.BlockSpec((FFN_TM, ATT_W), lambda t: (t, 0)),
                  pl.BlockSpec((FFN_TM, GATE_W), lambda t: (t, 0)),
                  _const_spec((ATT_W, D)), _const_spec((ATT_W, D)),
                  _const_spec((D, D)), _const_spec((1, D)), _const_spec((D, D_FF)),
                  _const_spec((D, D_FF)), _const_spec((D_FF, D)), _const_spec((1, D))],
        out_specs=tok_spec,
        out_shape=jax.ShapeDtypeStruct((T, D), f32),
        compiler_params=_tc_params(1),
        name="mix_ffn2",
    )(h1, a.reshape(T, -1), b.reshape(T, -1), gates,
      w_branch_diff[0].astype(bf16), w_branch_sb[0].astype(bf16), w_out[0].astype(bf16),
      row(ffn2_norm[0]), ffn2_w_gate[0].astype(bf16), ffn2_w_up[0].astype(bf16),
      ffn2_w_down[0].astype(bf16), row(final_norm))
    return out.reshape(B, S, D)
```

```python
import math

import jax
import jax.numpy as jnp
from jax import lax
from jax.experimental import pallas as pl
from jax.experimental.pallas import tpu as pltpu

D_MODEL = 1024
D_FF = 2816
N_HEADS = 4
HEAD_W = 128
DA_QK_DIM = 64
ATT_W = N_HEADS * HEAD_W
K_W = 2 * ATT_W
QVT_W = 4 * ATT_W
GATE_W = 2 * D_MODEL
NORM_EPS = 1e-5
LAMBDA_INIT = 0.8 - 0.6 * math.exp(-0.3 * 0)
LOG2E = 1.0 / math.log(2.0)

VMEM_LIMIT_BYTES = 56 * 1024 * 1024

FFN_TM = 512
PROJ_TM = 512
ATT_TQ = 2048
ATT_TK = 512
SB_TQ = 2048
CHAIN_W = 256
SB_DONE_LOG2 = 160.0
SUM_ROWS = 16

_NT = (((1,), (1,)), ((), ()))


def _rms(x, g):
    ms = jnp.mean(x * x, axis=-1, keepdims=True)
    return x * lax.rsqrt(ms + NORM_EPS) * g


def _swiglu_half_step(x, norm_g, wg_ref, wu_ref, wd_ref):
    xn = _rms(x, norm_g).astype(jnp.bfloat16)
    g = jnp.dot(xn, wg_ref[...], preferred_element_type=jnp.float32)
    u = jnp.dot(xn, wu_ref[...], preferred_element_type=jnp.float32)
    hact = (g * jax.nn.sigmoid(g) * u).astype(jnp.bfloat16)
    return x + 0.5 * jnp.dot(hact, wd_ref[...], preferred_element_type=jnp.float32)


def _ffn1_kernel(x_ref, norm_ref, wg_ref, wu_ref, wd_ref, o_ref):
    o_ref[...] = _swiglu_half_step(x_ref[...], norm_ref[...], wg_ref, wu_ref, wd_ref)


def _in_proj_kernel(h_ref, norm_ref, wk_ref, wqvt_ref, wgate_ref, rowscale_ref, bgate_ref,
                    k_ref, qvt_ref, gate_ref):
    n = _rms(h_ref[...], norm_ref[...]).astype(jnp.bfloat16)
    k_ref[...] = jnp.dot(n, wk_ref[...], preferred_element_type=jnp.float32).astype(jnp.bfloat16)
    qvt = lax.dot_general(wqvt_ref[...], n, _NT, preferred_element_type=jnp.float32)
    qvt_ref[...] = (qvt * rowscale_ref[...]).astype(jnp.bfloat16)
    g = jnp.dot(n, wgate_ref[...], preferred_element_type=jnp.float32)
    gate_ref[...] = jax.nn.sigmoid(g + bgate_ref[...]).astype(jnp.bfloat16)


def _emit_pipelined(stages, n):
    state = [dict() for _ in range(n)]
    for step in range(n + len(stages) - 1):
        for s, stage in enumerate(stages):
            t = step - s
            if 0 <= t < n:
                stage(t, state[t], state[t + 1] if t + 1 < n else None)


def _ordered_after(x, token):
    zero = lax.shift_right_logical(
        lax.shift_right_logical(pltpu.bitcast(token, jnp.uint32), jnp.uint32(16)), jnp.uint32(16))
    return pltpu.bitcast(pltpu.bitcast(x, jnp.uint32) + zero, jnp.float32)


def _diff_attn_kernel(slope_ref, lam_ref, q_ref, k_ref, vt_ref, subln_ref, o_ref,
                      mask_ref, kfeat_ref, s_ref, m_ref, l_ref, acc_ref):
    tq, tk, cw = ATT_TQ, ATT_TK, CHAIN_W
    per_map = tq // cw
    n_chains = 2 * per_map
    n_diag = tq // tk
    assert n_diag % 2 == 0 and tk == 2 * cw
    h = pl.program_id(1)
    i = pl.program_id(2)
    slope = slope_ref[h] * LOG2E
    lam = lam_ref[0]

    @pl.when(jnp.logical_and(pl.program_id(0) == 0, jnp.logical_and(h == 0, i == 0)))
    def _():
        krow = lax.broadcasted_iota(jnp.int32, (tk, tk), 0)
        qcol = lax.broadcasted_iota(jnp.int32, (tk, tk), 1)
        mask_ref[...] = jnp.where(qcol >= krow, 0.0, -jnp.inf)
        kpos = lax.broadcasted_iota(jnp.int32, (tk, HEAD_W), 0)
        klane = lax.broadcasted_iota(jnp.int32, (tk, HEAD_W), 1)
        k_hi = jnp.where(kpos >= 256, 256, 0)
        kfeat_ref[...] = jnp.where(klane < 3, k_hi, jnp.where(klane < 6, kpos - k_hi, 0)
                                   ).astype(jnp.float32).astype(jnp.bfloat16)

    qt = q_ref[...]
    chan = lax.broadcasted_iota(jnp.int32, (HEAD_W, tq), 0)
    zero = jnp.zeros_like(qt)
    q_maps = (jnp.where(chan < DA_QK_DIM, qt, zero), jnp.where(chan >= DA_QK_DIM, qt, zero))

    sl = jnp.full((HEAD_W, cw), slope, jnp.float32)
    hi = sl.astype(jnp.bfloat16).astype(jnp.float32)
    mid = (sl - hi).astype(jnp.bfloat16).astype(jnp.float32)
    lo = sl - hi - mid
    frow = lax.broadcasted_iota(jnp.int32, (HEAD_W, cw), 0)
    part = frow % 3
    q_feat = jnp.where(frow < 6, jnp.where(part == 0, hi, jnp.where(part == 1, mid, lo)),
                       0.0).astype(jnp.bfloat16)
    q_chain = [jnp.concatenate(
        [q_maps[c // per_map][:, (c % per_map) * cw:(c % per_map + 1) * cw], q_feat], axis=0)
        for c in range(n_chains)]

    m_ref[...] = jnp.full_like(m_ref, -jnp.inf)
    l_ref[...] = jnp.zeros_like(l_ref)
    acc_ref[...] = jnp.zeros_like(acc_ref)

    def chain_mode(c, d):
        if d is None:
            return "full"
        q_lo = (c % per_map) * cw
        if q_lo + cw <= d * tk:
            return "skip"
        if q_lo >= (d + 1) * tk:
            return "full"
        return q_lo - d * tk

    def scores_to(slot, j, d=None):
        kb = k_ref[0, pl.ds(pl.multiple_of(j * tk, tk), tk), :]
        kb = jnp.concatenate([kb, kfeat_ref[...]], axis=1)
        for c in range(n_chains):
            if chain_mode(c, d) != "skip":
                s_ref[slot, c] = jnp.dot(kb, q_chain[c], preferred_element_type=jnp.float32)

    def consume(slot, j, d=None):
        chains = [c for c in range(n_chains) if chain_mode(c, d) != "skip"]
        vtb = vt_ref[:, pl.ds(pl.multiple_of(j * tk, tk), tk)]
        vtb = jnp.concatenate([vtb, jnp.ones((SUM_ROWS, tk), vtb.dtype)], axis=0)
        shift = -slope * (i * tq - j * tk).astype(jnp.float32)

        def column_max(t, st, nxt):
            c = chains[t]
            s = s_ref[slot, c]
            mode = chain_mode(c, d)
            if mode != "full":
                s = s + mask_ref[:, mode:mode + cw]
                s_ref[slot, c] = s
            st["cmax"] = jnp.max(s, axis=0, keepdims=True) + shift

        def softmax_pv(t, st, nxt):
            c = chains[t]
            lanes = slice(c * cw, (c + 1) * cw)
            m_prev = m_ref[:, lanes]
            m_new = jnp.maximum(m_prev, st.pop("cmax"))
            st["alpha"] = jnp.exp2(m_prev - m_new)
            p = jnp.exp2(s_ref[slot, c] - (m_new - shift))
            m_ref[:, lanes] = m_new
            st["pv"] = jnp.dot(vtb, p.astype(jnp.bfloat16),
                               preferred_element_type=jnp.float32)

        def accumulate(t, st, nxt):
            c = chains[t]
            lanes = slice(c * cw, (c + 1) * cw)
            alpha, pv = st.pop("alpha"), st.pop("pv")
            acc_ref[:, lanes] = alpha * acc_ref[:, lanes] + pv[:HEAD_W]
            l_ref[:, lanes] = alpha * l_ref[:, lanes] + pv[HEAD_W:HEAD_W + 1]

        _emit_pipelined((column_max, softmax_pv, accumulate), len(chains))

    def step(slot, j):
        scores_to(1 - slot, j + 1)
        consume(slot, j)

    scores_to(0, 0)

    def pair(jj, carry):
        step(0, 2 * jj)
        step(1, 2 * jj + 1)
        return carry

    first_diag = n_diag * i
    lax.fori_loop(0, first_diag // 2, pair, 0)
    for d in range(n_diag):
        if d + 1 < n_diag:
            scores_to((d + 1) % 2, first_diag + d + 1, d + 1)
        consume(d % 2, first_diag + d, d)

    o = acc_ref[...] / l_ref[...]
    a = o[:, :tq] - lam * o[:, tq:]
    ms = jnp.mean(a * a, axis=0, keepdims=True)
    a = a * lax.rsqrt(ms + NORM_EPS) * subln_ref[...] * (1.0 - LAMBDA_INIT)
    o_ref[0] = a.T.astype(o_ref.dtype)


def _sb_attn_kernel(q_ref, k_ref, vt_ref, o_ref, c_ref, acc_ref):
    tq, cw = SB_TQ, CHAIN_W
    n_chains = tq // cw
    i = pl.program_id(2)
    qt = q_ref[...]
    q_chain = [qt[:, c * cw:(c + 1) * cw] for c in range(n_chains)]

    krow = lax.broadcasted_iota(jnp.int32, (cw, cw), 0)
    qcol = lax.broadcasted_iota(jnp.int32, (cw, cw), 1)
    strict = krow < qcol
    lrow = lax.broadcasted_iota(jnp.int32, (cw + SUM_ROWS, cw), 0)
    lcol = lax.broadcasted_iota(jnp.int32, (cw + SUM_ROWS, cw), 1)
    later = jnp.where(jnp.logical_or(lcol > lrow, lrow >= cw), 1.0, 0.0).astype(jnp.bfloat16)

    c_ref[...] = jnp.zeros_like(c_ref)
    acc_ref[...] = jnp.zeros_like(acc_ref)

    def run_pieces(pieces):
        def scores(t, st, nxt):
            sub, c, _ = pieces[t]
            start = pl.multiple_of(sub * cw, cw)
            st["z"] = jnp.dot(k_ref[0, pl.ds(start, cw), :], q_chain[c],
                              preferred_element_type=jnp.float32)

        def suffix(t, st, nxt):
            _, _, triangular = pieces[t]
            z = st.pop("z")
            u = jnp.maximum(z, 0.0) + jnp.log2(1.0 + jnp.exp2(-jnp.abs(z)))
            st["log_sig"] = z - u
            if triangular:
                u = jnp.where(strict, u, 0.0)
            st["tail"] = jnp.dot(later, u.astype(jnp.bfloat16),
                                 preferred_element_type=jnp.float32)

        def weights_pv(t, st, nxt):
            sub, _, triangular = pieces[t]
            start = pl.multiple_of(sub * cw, cw)
            tail = st.pop("tail")
            st["usum"] = tail[cw:cw + 1]
            a = jnp.exp2(st.pop("log_sig") - tail[:cw])
            if triangular:
                a = jnp.where(strict, a, 0.0)
            st["pv"] = jnp.dot(vt_ref[:, pl.ds(start, cw)], a.astype(jnp.bfloat16),
                               preferred_element_type=jnp.float32)

        def accumulate(t, st, nxt):
            _, c, _ = pieces[t]
            lanes = slice(c * cw, (c + 1) * cw)
            carry = c_ref[:, lanes]
            acc_ref[:, lanes] += st.pop("pv") * jnp.exp2(-carry)
            c_ref[:, lanes] = carry + st.pop("usum")

        _emit_pipelined((scores, suffix, weights_pv, accumulate), len(pieces))

    diag = [n_chains * i + c for c in range(n_chains)]
    head = [(diag[c], c, True) for c in reversed(range(n_chains))]
    second = [(diag[c] - 1, c, False) for c in reversed(range(n_chains))]

    @pl.when(i == 0)
    def _():
        run_pieces(head + [p for p in second if p[1] > 0])

    @pl.when(i > 0)
    def _():
        run_pieces(head + second)

    def unfinished(c, depth):
        lanes = slice(c * cw, (c + 1) * cw)
        return jnp.logical_and(diag[c] - depth >= 0,
                               jnp.min(c_ref[:, lanes]) < SB_DONE_LOG2)

    def any_unfinished(depth):
        go = unfinished(0, depth)
        for c in range(1, n_chains):
            go = jnp.logical_or(go, unfinished(c, depth))
        return go

    def body(carry):
        depth, _ = carry
        for c in range(n_chains):
            @pl.when(unfinished(c, depth))
            def _():
                run_pieces([(diag[c] - depth, c, False)])
        return depth + 1, any_unfinished(depth + 1)

    lax.while_loop(lambda carry: carry[1], body, (jnp.int32(2), any_unfinished(2)))
    o_ref[0] = acc_ref[...].T.astype(o_ref.dtype)


def _mix_ffn2_kernel(h_ref, a_ref, b_ref, gate_ref, wa_ref, wb_ref, wout_ref,
                     norm2_ref, wg_ref, wu_ref, wd_ref, normf_ref, o_ref):
    ya = jnp.dot(a_ref[...], wa_ref[...], preferred_element_type=jnp.float32)
    yb = jnp.dot(b_ref[...], wb_ref[...], preferred_element_type=jnp.float32)
    gate = gate_ref[...].astype(jnp.float32)
    y = (gate[:, :D_MODEL] * ya + gate[:, D_MODEL:] * yb).astype(jnp.bfloat16)
    h2 = h_ref[...] + jnp.dot(y, wout_ref[...], preferred_element_type=jnp.float32)
    h3 = _swiglu_half_step(h2, norm2_ref[...], wg_ref, wu_ref, wd_ref)
    o_ref[...] = _rms(h3, normf_ref[...])


def _const_spec(shape):
    return pl.BlockSpec(shape, lambda *_: (0,) * len(shape), pipeline_mode=pl.Buffered(1))


ATTN_FLAGS = None


def _tc_params(n_axes, flags=None):
    return pltpu.CompilerParams(dimension_semantics=("arbitrary",) * n_axes,
                                vmem_limit_bytes=VMEM_LIMIT_BYTES, flags=flags)


def kernel(x, ffn1_norm, ffn1_w_gate, ffn1_w_up, ffn1_w_down, mix_norm, w_in, b_gate, lambda_q1, lambda_k1, lambda_q2, lambda_k2, diff_subln, w_branch_diff, w_branch_sb, w_out, ffn2_norm, ffn2_w_gate, ffn2_w_up, ffn2_w_down, final_norm):
    B, S, D = x.shape
    T = B * S
    f32, bf16 = jnp.float32, jnp.bfloat16
    xt = x.reshape(T, D)
    row = lambda v: v.reshape(1, -1).astype(f32)

    tok_spec = pl.BlockSpec((FFN_TM, D), lambda t: (t, 0))
    h1 = pl.pallas_call(
        _ffn1_kernel,
        grid=(T // FFN_TM,),
        in_specs=[tok_spec, _const_spec((1, D)), _const_spec((D, D_FF)),
                  _const_spec((D, D_FF)), _const_spec((D_FF, D))],
        out_specs=tok_spec,
        out_shape=jax.ShapeDtypeStruct((T, D), f32),
        compiler_params=_tc_params(1),
        name="ffn1",
    )(xt, row(ffn1_norm[0]), ffn1_w_gate[0].astype(bf16), ffn1_w_up[0].astype(bf16),
      ffn1_w_down[0].astype(bf16))

    w = w_in[0]
    w_k = jnp.concatenate([w[:, ATT_W:2 * ATT_W], w[:, 4 * ATT_W:5 * ATT_W]], axis=1).astype(bf16)
    w_qvt = jnp.concatenate([w[:, 0:ATT_W], w[:, 3 * ATT_W:4 * ATT_W],
                             w[:, 2 * ATT_W:3 * ATT_W], w[:, 5 * ATT_W:6 * ATT_W]],
                            axis=1).T.astype(bf16)
    w_gate = w[:, 6 * ATT_W:].astype(bf16)
    rowscale = jnp.ones((QVT_W,), f32)
    rowscale = rowscale.at[0:ATT_W].set(DA_QK_DIM ** -0.5 * LOG2E)
    rowscale = rowscale.at[ATT_W:2 * ATT_W].set(HEAD_W ** -0.5 * LOG2E)
    k, qvt, gates = pl.pallas_call(
        _in_proj_kernel,
        grid=(T // PROJ_TM,),
        in_specs=[pl.BlockSpec((PROJ_TM, D), lambda t: (t, 0)), _const_spec((1, D)),
                  _const_spec((D, K_W)), _const_spec((QVT_W, D)), _const_spec((D, GATE_W)),
                  _const_spec((QVT_W, 1)), _const_spec((1, GATE_W))],
        out_specs=[pl.BlockSpec((PROJ_TM, K_W), lambda t: (t, 0)),
                   pl.BlockSpec((QVT_W, PROJ_TM), lambda t: (0, t)),
                   pl.BlockSpec((PROJ_TM, GATE_W), lambda t: (t, 0))],
        out_shape=[jax.ShapeDtypeStruct((T, K_W), bf16),
                   jax.ShapeDtypeStruct((QVT_W, T), bf16),
                   jax.ShapeDtypeStruct((T, GATE_W), bf16)],
        compiler_params=_tc_params(1),
        name="in_proj",
    )(h1, row(mix_norm[0]), w_k, w_qvt, w_gate, rowscale.reshape(-1, 1), row(b_gate[0]))
    k3 = k.reshape(B, S, K_W)

    def q_spec(slab, tq):
        return pl.BlockSpec((HEAD_W, tq),
                            lambda b, h, i: (slab * N_HEADS + h, b * (S // tq) + i))

    def k_spec(slab):
        return pl.BlockSpec((1, S, HEAD_W), lambda b, h, i: (b, 0, slab * N_HEADS + h))

    def vt_spec(slab):
        return pl.BlockSpec((HEAD_W, S), lambda b, h, i: (slab * N_HEADS + h, b))

    def att_out_spec(tq):
        return pl.BlockSpec((1, tq, HEAD_W), lambda b, h, i: (b, i, h))

    att_out_shape = jax.ShapeDtypeStruct((B, S, ATT_W), bf16)
    smem_spec = pl.BlockSpec(memory_space=pltpu.SMEM)

    lam = (jnp.exp(jnp.sum(lambda_q1[0].astype(f32) * lambda_k1[0].astype(f32)))
           - jnp.exp(jnp.sum(lambda_q2[0].astype(f32) * lambda_k2[0].astype(f32)))
           + LAMBDA_INIT).reshape(1)
    slopes = jnp.exp2(-8.0 * jnp.arange(1, N_HEADS + 1, dtype=f32) / N_HEADS)

    a = pl.pallas_call(
        _diff_attn_kernel,
        grid=(B, N_HEADS, S // ATT_TQ),
        in_specs=[smem_spec, smem_spec, q_spec(0, ATT_TQ), k_spec(0), vt_spec(2),
                  _const_spec((HEAD_W, 1))],
        out_specs=att_out_spec(ATT_TQ),
        out_shape=att_out_shape,
        scratch_shapes=[pltpu.VMEM((ATT_TK, ATT_TK), f32), pltpu.VMEM((ATT_TK, HEAD_W), bf16),
                        pltpu.VMEM((2, 2 * ATT_TQ // CHAIN_W, ATT_TK, CHAIN_W), f32),
                        pltpu.VMEM((1, 2 * ATT_TQ), f32), pltpu.VMEM((1, 2 * ATT_TQ), f32),
                        pltpu.VMEM((HEAD_W, 2 * ATT_TQ), f32)],
        compiler_params=_tc_params(3, ATTN_FLAGS),
        name="diff_attn",
    )(slopes, lam, qvt, k3, qvt, diff_subln[0].reshape(-1, 1).astype(f32))

    b = pl.pallas_call(
        _sb_attn_kernel,
        grid=(B, N_HEADS, S // SB_TQ),
        in_specs=[q_spec(1, SB_TQ), k_spec(1), vt_spec(3)],
        out_specs=att_out_spec(SB_TQ),
        out_shape=att_out_shape,
        scratch_shapes=[pltpu.VMEM((1, SB_TQ), f32), pltpu.VMEM((HEAD_W, SB_TQ), f32)],
        compiler_params=_tc_params(3, ATTN_FLAGS),
        name="sb_attn",
    )(qvt, k3, qvt)

    out = pl.pallas_call(
        _mix_ffn2_kernel,
        grid=(T // FFN_TM,),
        in_specs=[tok_spec,
                  pl.BlockSpec((FFN_TM, ATT_W), lambda t: (t, 0)),
                  pl.BlockSpec((FFN_TM, ATT_W), lambda t: (t, 0)),
                  pl.BlockSpec((FFN_TM, GATE_W), lambda t: (t, 0)),
                  _const_spec((ATT_W, D)), _const_spec((ATT_W, D)),
                  _const_spec((D, D)), _const_spec((1, D)), _const_spec((D, D_FF)),
                  _const_spec((D, D_FF)), _const_spec((D_FF, D)), _const_spec((1, D))],
        out_specs=tok_spec,
        out_shape=jax.ShapeDtypeStruct((T, D), f32),
        compiler_params=_tc_params(1),
        name="mix_ffn2",
    )(h1, a.reshape(T, -1), b.reshape(T, -1), gates,
      w_branch_diff[0].astype(bf16), w_branch_sb[0].astype(bf16), w_out[0].astype(bf16),
      row(ffn2_norm[0]), ffn2_w_gate[0].astype(bf16), ffn2_w_up[0].astype(bf16),
      ffn2_w_down[0].astype(bf16), row(final_norm))
    return out.reshape(B, S, D)
```

```python
import math

import jax
import jax.numpy as jnp
from jax import lax
from jax.experimental import pallas as pl
from jax.experimental.pallas import tpu as pltpu

D_MODEL = 1024
D_FF = 2816
N_HEADS = 4
HEAD_W = 128
DA_QK_DIM = 64
ATT_W = N_HEADS * HEAD_W
K_W = 2 * ATT_W
QVT_W = 4 * ATT_W
GATE_W = 2 * D_MODEL
NORM_EPS = 1e-5
LAMBDA_INIT = 0.8 - 0.6 * math.exp(-0.3 * 0)
LOG2E = 1.0 / math.log(2.0)

VMEM_LIMIT_BYTES = 56 * 1024 * 1024

STAGE_WIDE_ROWS = 128
STAGE_TALL_ROWS = 352
STAGE_SQUARE_ROWS = 256
STAGE_IN_ROWS = 256

FFN_TM = 512
PROJ_TM = 512
ATT_TQ = 2048
ATT_TK = 512
SB_TQ = 2048
CHAIN_W = 256
SB_DONE_LOG2 = 160.0
SUM_ROWS = 16

_NT = (((1,), (1,)), ((), ()))


def _rms(x, g):
    ms = jnp.mean(x * x, axis=-1, keepdims=True)
    return x * lax.rsqrt(ms + NORM_EPS) * g


def _swiglu_half_step(x, norm_g, wg_ref, wu_ref, wd_ref):
    xn = _rms(x, norm_g).astype(jnp.bfloat16)
    g = jnp.dot(xn, wg_ref[...], preferred_element_type=jnp.float32)
    u = jnp.dot(xn, wu_ref[...], preferred_element_type=jnp.float32)
    hact = (g * jax.nn.sigmoid(g) * u).astype(jnp.bfloat16)
    return x + 0.5 * jnp.dot(hact, wd_ref[...], preferred_element_type=jnp.float32)


def _stage_weight(src_hbm, dst_ref, stage_ref, sem_ref, *, rows, col0=0, ncols=None,
                  dst_row0=0, dst_col0=0, transpose=False):
    n_rows = src_hbm.shape[0]
    ncols = src_hbm.shape[1] - col0 if ncols is None else ncols
    assert n_rows % rows == 0 and rows <= stage_ref.shape[1] and ncols <= stage_ref.shape[2]
    n_slabs = n_rows // rows

    def slab_copy(c):
        return pltpu.make_async_copy(
            src_hbm.at[pl.ds(c * rows, rows), pl.ds(col0, ncols)],
            stage_ref.at[c % 2, pl.ds(0, rows), pl.ds(0, ncols)],
            sem_ref.at[c % 2])

    slab_copy(0).start()
    for c in range(n_slabs):
        if c + 1 < n_slabs:
            slab_copy(c + 1).start()
        slab_copy(c).wait()
        slab = stage_ref[c % 2, :rows, :ncols]
        if transpose:
            dst_ref[dst_row0:dst_row0 + ncols,
                    dst_col0 + c * rows:dst_col0 + (c + 1) * rows] = slab.T.astype(dst_ref.dtype)
        else:
            dst_ref[dst_row0 + c * rows:dst_row0 + (c + 1) * rows,
                    dst_col0:dst_col0 + ncols] = slab.astype(dst_ref.dtype)


def _stage_ffn_weights(wg_hbm, wu_hbm, wd_hbm, wg_ref, wu_ref, wd_ref, stage_wide, stage_tall, sem):
    _stage_weight(wg_hbm, wg_ref, stage_wide, sem, rows=STAGE_WIDE_ROWS)
    _stage_weight(wu_hbm, wu_ref, stage_wide, sem, rows=STAGE_WIDE_ROWS)
    _stage_weight(wd_hbm, wd_ref, stage_tall, sem, rows=STAGE_TALL_ROWS)


def _ffn1_kernel(x_ref, norm_ref, wg_hbm, wu_hbm, wd_hbm, o_ref,
                 wg_ref, wu_ref, wd_ref, stage_wide, stage_tall, sem):
    @pl.when(pl.program_id(0) == 0)
    def _():
        _stage_ffn_weights(wg_hbm, wu_hbm, wd_hbm, wg_ref, wu_ref, wd_ref,
                           stage_wide, stage_tall, sem)

    o_ref[...] = _swiglu_half_step(x_ref[...], norm_ref[...], wg_ref, wu_ref, wd_ref)


def _in_proj_kernel(h_ref, norm_ref, win_hbm, rowscale_ref, bgate_ref,
                    k_ref, qvt_ref, gate_ref, wk_ref, wqvt_ref, wgate_ref, stage, sem):
    @pl.when(pl.program_id(0) == 0)
    def _():
        w = ATT_W
        for piece, src_block in enumerate((1, 4)):
            _stage_weight(win_hbm, wk_ref, stage, sem, rows=STAGE_IN_ROWS,
                          col0=src_block * w, ncols=w, dst_col0=piece * w)
        _stage_weight(win_hbm, wgate_ref, stage, sem, rows=STAGE_IN_ROWS, col0=6 * w, ncols=GATE_W)
        for piece, src_block in enumerate((0, 3, 2, 5)):
            _stage_weight(win_hbm, wqvt_ref, stage, sem, rows=STAGE_IN_ROWS,
                          col0=src_block * w, ncols=w, dst_row0=piece * w, transpose=True)

    n = _rms(h_ref[...], norm_ref[...]).astype(jnp.bfloat16)
    g = jnp.dot(n, wgate_ref[...], preferred_element_type=jnp.float32)
    gate_ref[...] = jax.nn.sigmoid(g + bgate_ref[...]).astype(jnp.bfloat16)
    k_ref[...] = jnp.dot(n, wk_ref[...], preferred_element_type=jnp.float32).astype(jnp.bfloat16)
    qvt = lax.dot_general(wqvt_ref[...], n, _NT, preferred_element_type=jnp.float32)
    qvt_ref[...] = (qvt * rowscale_ref[...]).astype(jnp.bfloat16)


def _emit_pipelined(stages, n):
    state = [dict() for _ in range(n)]
    for step in range(n + len(stages) - 1):
        for s, stage in enumerate(stages):
            t = step - s
            if 0 <= t < n:
                stage(t, state[t], state[t + 1] if t + 1 < n else None)


def _ordered_after(x, token):
    zero = lax.shift_right_logical(
        lax.shift_right_logical(pltpu.bitcast(token, jnp.uint32), jnp.uint32(16)), jnp.uint32(16))
    return pltpu.bitcast(pltpu.bitcast(x, jnp.uint32) + zero, jnp.float32)


def _diff_attn_kernel(slope_ref, lam_ref, q_ref, k_ref, vt_ref, subln_ref, o_ref,
                      mask_ref, kfeat_ref, s_ref, m_ref, l_ref, acc_ref):
    tq, tk, cw = ATT_TQ, ATT_TK, CHAIN_W
    per_map = tq // cw
    n_chains = 2 * per_map
    n_diag = tq // tk
    assert n_diag % 2 == 0 and tk == 2 * cw
    h = pl.program_id(1)
    i = pl.program_id(2)
    slope = slope_ref[h] * LOG2E
    lam = lam_ref[0]

    @pl.when(jnp.logical_and(pl.program_id(0) == 0, jnp.logical_and(h == 0, i == 0)))
    def _():
        krow = lax.broadcasted_iota(jnp.int32, (tk, tk), 0)
        qcol = lax.broadcasted_iota(jnp.int32, (tk, tk), 1)
        mask_ref[...] = jnp.where(qcol >= krow, 0.0, -jnp.inf)
        kpos = lax.broadcasted_iota(jnp.int32, (tk, HEAD_W), 0)
        klane = lax.broadcasted_iota(jnp.int32, (tk, HEAD_W), 1)
        k_hi = jnp.where(kpos >= 256, 256, 0)
        kfeat_ref[...] = jnp.where(klane < 3, k_hi, jnp.where(klane < 6, kpos - k_hi, 0)
                                   ).astype(jnp.float32).astype(jnp.bfloat16)

    qt = q_ref[...]
    chan = lax.broadcasted_iota(jnp.int32, (HEAD_W, tq), 0)
    zero = jnp.zeros_like(qt)
    q_maps = (jnp.where(chan < DA_QK_DIM, qt, zero), jnp.where(chan >= DA_QK_DIM, qt, zero))

    sl = jnp.full((HEAD_W, cw), slope, jnp.float32)
    hi = sl.astype(jnp.bfloat16).astype(jnp.float32)
    mid = (sl - hi).astype(jnp.bfloat16).astype(jnp.float32)
    lo = sl - hi - mid
    frow = lax.broadcasted_iota(jnp.int32, (HEAD_W, cw), 0)
    part = frow % 3
    q_feat = jnp.where(frow < 6, jnp.where(part == 0, hi, jnp.where(part == 1, mid, lo)),
                       0.0).astype(jnp.bfloat16)
    q_chain = [jnp.concatenate(
        [q_maps[c // per_map][:, (c % per_map) * cw:(c % per_map + 1) * cw], q_feat], axis=0)
        for c in range(n_chains)]

    m_ref[...] = jnp.full_like(m_ref, -jnp.inf)
    l_ref[...] = jnp.zeros_like(l_ref)
    acc_ref[...] = jnp.zeros_like(acc_ref)

    def chain_mode(c, d):
        if d is None:
            return "full"
        q_lo = (c % per_map) * cw
        if q_lo + cw <= d * tk:
            return "skip"
        if q_lo >= (d + 1) * tk:
            return "full"
        return q_lo - d * tk

    def scores_to(slot, j, d=None):
        kb = k_ref[0, pl.ds(pl.multiple_of(j * tk, tk), tk), :]
        kb = jnp.concatenate([kb, kfeat_ref[...]], axis=1)
        for c in range(n_chains):
            if chain_mode(c, d) != "skip":
                s_ref[slot, c] = jnp.dot(kb, q_chain[c], preferred_element_type=jnp.float32)

    def consume(slot, j, d=None):
        chains = [c for c in range(n_chains) if chain_mode(c, d) != "skip"]
        vtb = vt_ref[:, pl.ds(pl.multiple_of(j * tk, tk), tk)]
        vtb = jnp.concatenate([vtb, jnp.ones((SUM_ROWS, tk), vtb.dtype)], axis=0)
        shift = -slope * (i * tq - j * tk).astype(jnp.float32)

        def column_max(t, st, nxt):
            c = chains[t]
            s = s_ref[slot, c]
            mode = chain_mode(c, d)
            if mode != "full":
                s = s + mask_ref[:, mode:mode + cw]
                s_ref[slot, c] = s
            st["cmax"] = jnp.max(s, axis=0, keepdims=True) + shift

        def softmax_pv(t, st, nxt):
            c = chains[t]
            lanes = slice(c * cw, (c + 1) * cw)
            m_prev = m_ref[:, lanes]
            m_new = jnp.maximum(m_prev, st.pop("cmax"))
            st["alpha"] = jnp.exp2(m_prev - m_new)
            p = jnp.exp2(s_ref[slot, c] - (m_new - shift))
            m_ref[:, lanes] = m_new
            st["pv"] = jnp.dot(vtb, p.astype(jnp.bfloat16),
                               preferred_element_type=jnp.float32)

        def accumulate(t, st, nxt):
            c = chains[t]
            lanes = slice(c * cw, (c + 1) * cw)
            alpha, pv = st.pop("alpha"), st.pop("pv")
            acc_ref[:, lanes] = alpha * acc_ref[:, lanes] + pv[:HEAD_W]
            l_ref[:, lanes] = alpha * l_ref[:, lanes] + pv[HEAD_W:HEAD_W + 1]

        _emit_pipelined((column_max, softmax_pv, accumulate), len(chains))

    def step(slot, j):
        scores_to(1 - slot, j + 1)
        consume(slot, j)

    scores_to(0, 0)

    def pair(jj, carry):
        step(0, 2 * jj)
        step(1, 2 * jj + 1)
        return carry

    first_diag = n_diag * i
    lax.fori_loop(0, first_diag // 2, pair, 0)
    for d in range(n_diag):
        if d + 1 < n_diag:
            scores_to((d + 1) % 2, first_diag + d + 1, d + 1)
        consume(d % 2, first_diag + d, d)

    o = acc_ref[...] / l_ref[...]
    a = o[:, :tq] - lam * o[:, tq:]
    ms = jnp.mean(a * a, axis=0, keepdims=True)
    a = a * lax.rsqrt(ms + NORM_EPS) * subln_ref[...] * (1.0 - LAMBDA_INIT)
    o_ref[0] = a.T.astype(o_ref.dtype)


def _sb_attn_kernel(q_ref, k_ref, vt_ref, o_ref, c_ref, acc_ref):
    tq, cw = SB_TQ, CHAIN_W
    n_chains = tq // cw
    i = pl.program_id(2)
    qt = q_ref[...]
    q_chain = [qt[:, c * cw:(c + 1) * cw] for c in range(n_chains)]

    krow = lax.broadcasted_iota(jnp.int32, (cw, cw), 0)
    qcol = lax.broadcasted_iota(jnp.int32, (cw, cw), 1)
    strict = krow < qcol
    lrow = lax.broadcasted_iota(jnp.int32, (cw + SUM_ROWS, cw), 0)
    lcol = lax.broadcasted_iota(jnp.int32, (cw + SUM_ROWS, cw), 1)
    later = jnp.where(jnp.logical_or(lcol > lrow, lrow >= cw), 1.0, 0.0).astype(jnp.bfloat16)

    c_ref[...] = jnp.zeros_like(c_ref)
    acc_ref[...] = jnp.zeros_like(acc_ref)

    def run_pieces(pieces):
        def scores(t, st, nxt):
            sub, c, _ = pieces[t]
            start = pl.multiple_of(sub * cw, cw)
            st["z"] = jnp.dot(k_ref[0, pl.ds(start, cw), :], q_chain[c],
                              preferred_element_type=jnp.float32)

        def suffix(t, st, nxt):
            _, _, triangular = pieces[t]
            z = st.pop("z")
            u = jnp.maximum(z, 0.0) + jnp.log2(1.0 + jnp.exp2(-jnp.abs(z)))
            st["log_sig"] = z - u
            if triangular:
                u = jnp.where(strict, u, 0.0)
            st["tail"] = jnp.dot(later, u.astype(jnp.bfloat16),
                                 preferred_element_type=jnp.float32)

        def weights_pv(t, st, nxt):
            sub, _, triangular = pieces[t]
            start = pl.multiple_of(sub * cw, cw)
            tail = st.pop("tail")
            st["usum"] = tail[cw:cw + 1]
            a = jnp.exp2(st.pop("log_sig") - tail[:cw])
            if triangular:
                a = jnp.where(strict, a, 0.0)
            st["pv"] = jnp.dot(vt_ref[:, pl.ds(start, cw)], a.astype(jnp.bfloat16),
                               preferred_element_type=jnp.float32)

        def accumulate(t, st, nxt):
            _, c, _ = pieces[t]
            lanes = slice(c * cw, (c + 1) * cw)
            carry = c_ref[:, lanes]
            acc_ref[:, lanes] += st.pop("pv") * jnp.exp2(-carry)
            c_ref[:, lanes] = carry + st.pop("usum")

        _emit_pipelined((scores, suffix, weights_pv, accumulate), len(pieces))

    diag = [n_chains * i + c for c in range(n_chains)]
    head = [(diag[c], c, True) for c in reversed(range(n_chains))]
    second = [(diag[c] - 1, c, False) for c in reversed(range(n_chains))]

    @pl.when(i == 0)
    def _():
        run_pieces(head + [p for p in second if p[1] > 0])

    @pl.when(i > 0)
    def _():
        run_pieces(head + second)

    def unfinished(c, depth):
        lanes = slice(c * cw, (c + 1) * cw)
        return jnp.logical_and(diag[c] - depth >= 0,
                               jnp.min(c_ref[:, lanes]) < SB_DONE_LOG2)

    def any_unfinished(depth):
        go = unfinished(0, depth)
        for c in range(1, n_chains):
            go = jnp.logical_or(go, unfinished(c, depth))
        return go

    def body(carry):
        depth, _ = carry
        for c in range(n_chains):
            @pl.when(unfinished(c, depth))
            def _():
                run_pieces([(diag[c] - depth, c, False)])
        return depth + 1, any_unfinished(depth + 1)

    lax.while_loop(lambda carry: carry[1], body, (jnp.int32(2), any_unfinished(2)))
    o_ref[0] = acc_ref[...].T.astype(o_ref.dtype)


def _mix_ffn2_kernel(h_ref, a_ref, b_ref, gate_ref, wa_hbm, wb_hbm, wout_hbm,
                     norm2_ref, wg_hbm, wu_hbm, wd_hbm, normf_ref, o_ref,
                     wa_ref, wb_ref, wout_ref, wg_ref, wu_ref, wd_ref,
                     stage_wide, stage_tall, sem):
    @pl.when(pl.program_id(0) == 0)
    def _():
        for src, dst in ((wa_hbm, wa_ref), (wb_hbm, wb_ref), (wout_hbm, wout_ref)):
            _stage_weight(src, dst, stage_tall, sem, rows=STAGE_SQUARE_ROWS)
        _stage_ffn_weights(wg_hbm, wu_hbm, wd_hbm, wg_ref, wu_ref, wd_ref,
                           stage_wide, stage_tall, sem)

    ya = jnp.dot(a_ref[...], wa_ref[...], preferred_element_type=jnp.float32)
    yb = jnp.dot(b_ref[...], wb_ref[...], preferred_element_type=jnp.float32)
    gate = gate_ref[...].astype(jnp.float32)
    y = (gate[:, :D_MODEL] * ya + gate[:, D_MODEL:] * yb).astype(jnp.bfloat16)
    h2 = h_ref[...] + jnp.dot(y, wout_ref[...], preferred_element_type=jnp.float32)
    h3 = _swiglu_half_step(h2, norm2_ref[...], wg_ref, wu_ref, wd_ref)
    o_ref[...] = _rms(h3, normf_ref[...])


def _const_spec(shape):
    return pl.BlockSpec(shape, lambda *_: (0,) * len(shape), pipeline_mode=pl.Buffered(1))


ATTN_FLAGS = None


def _tc_params(n_axes, flags=None):
    return pltpu.CompilerParams(dimension_semantics=("arbitrary",) * n_axes,
                                vmem_limit_bytes=VMEM_LIMIT_BYTES, flags=flags)


def kernel(x, ffn1_norm, ffn1_w_gate, ffn1_w_up, ffn1_w_down, mix_norm, w_in, b_gate, lambda_q1, lambda_k1, lambda_q2, lambda_k2, diff_subln, w_branch_diff, w_branch_sb, w_out, ffn2_norm, ffn2_w_gate, ffn2_w_up, ffn2_w_down, final_norm):
    B, S, D = x.shape
    T = B * S
    f32, bf16 = jnp.float32, jnp.bfloat16
    xt = x.reshape(T, D)
    row = lambda v: v.reshape(1, -1).astype(f32)

    tok_spec = pl.BlockSpec((FFN_TM, D), lambda t: (t, 0))
    hbm_spec = pl.BlockSpec(memory_space=pl.ANY)
    ffn_weight_scratch = [pltpu.VMEM((D, D_FF), bf16), pltpu.VMEM((D, D_FF), bf16),
                          pltpu.VMEM((D_FF, D), bf16)]
    ffn_stage_scratch = [pltpu.VMEM((2, STAGE_WIDE_ROWS, D_FF), f32),
                         pltpu.VMEM((2, STAGE_TALL_ROWS, D), f32),
                         pltpu.SemaphoreType.DMA((2,))]
    h1 = pl.pallas_call(
        _ffn1_kernel,
        grid=(T // FFN_TM,),
        in_specs=[tok_spec, _const_spec((1, D)), hbm_spec, hbm_spec, hbm_spec],
        out_specs=tok_spec,
        out_shape=jax.ShapeDtypeStruct((T, D), f32),
        scratch_shapes=ffn_weight_scratch + ffn_stage_scratch,
        compiler_params=_tc_params(1),
        name="ffn1",
    )(xt, row(ffn1_norm[0]), ffn1_w_gate[0], ffn1_w_up[0], ffn1_w_down[0])

    rowscale = jnp.ones((QVT_W,), f32)
    rowscale = rowscale.at[0:ATT_W].set(DA_QK_DIM ** -0.5 * LOG2E)
    rowscale = rowscale.at[ATT_W:2 * ATT_W].set(HEAD_W ** -0.5 * LOG2E)
    k, qvt, gates = pl.pallas_call(
        _in_proj_kernel,
        grid=(T // PROJ_TM,),
        in_specs=[pl.BlockSpec((PROJ_TM, D), lambda t: (t, 0)), _const_spec((1, D)), hbm_spec,
                  _const_spec((QVT_W, 1)), _const_spec((1, GATE_W))],
        out_specs=[pl.BlockSpec((PROJ_TM, K_W), lambda t: (t, 0)),
                   pl.BlockSpec((QVT_W, PROJ_TM), lambda t: (0, t)),
                   pl.BlockSpec((PROJ_TM, GATE_W), lambda t: (t, 0))],
        out_shape=[jax.ShapeDtypeStruct((T, K_W), bf16),
                   jax.ShapeDtypeStruct((QVT_W, T), bf16),
                   jax.ShapeDtypeStruct((T, GATE_W), bf16)],
        scratch_shapes=[pltpu.VMEM((D, K_W), bf16), pltpu.VMEM((QVT_W, D), bf16),
                        pltpu.VMEM((D, GATE_W), bf16),
                        pltpu.VMEM((2, STAGE_IN_ROWS, GATE_W), f32),
                        pltpu.SemaphoreType.DMA((2,))],
        compiler_params=_tc_params(1),
        name="in_proj",
    )(h1, row(mix_norm[0]), w_in[0], rowscale.reshape(-1, 1), row(b_gate[0]))
    k3 = k.reshape(B, S, K_W)

    def q_spec(slab, tq):
        return pl.BlockSpec((HEAD_W, tq),
                            lambda b, h, i: (slab * N_HEADS + h, b * (S // tq) + i))

    def k_spec(slab):
        return pl.BlockSpec((1, S, HEAD_W), lambda b, h, i: (b, 0, slab * N_HEADS + h))

    def vt_spec(slab):
        return pl.BlockSpec((HEAD_W, S), lambda b, h, i: (slab * N_HEADS + h, b))

    def att_out_spec(tq):
        return pl.BlockSpec((1, tq, HEAD_W), lambda b, h, i: (b, i, h))

    att_out_shape = jax.ShapeDtypeStruct((B, S, ATT_W), bf16)
    smem_spec = pl.BlockSpec(memory_space=pltpu.SMEM)

    lam = (jnp.exp(jnp.sum(lambda_q1[0].astype(f32) * lambda_k1[0].astype(f32)))
           - jnp.exp(jnp.sum(lambda_q2[0].astype(f32) * lambda_k2[0].astype(f32)))
           + LAMBDA_INIT).reshape(1)
    slopes = jnp.exp2(-8.0 * jnp.arange(1, N_HEADS + 1, dtype=f32) / N_HEADS)

    a = pl.pallas_call(
        _diff_attn_kernel,
        grid=(B, N_HEADS, S // ATT_TQ),
        in_specs=[smem_spec, smem_spec, q_spec(0, ATT_TQ), k_spec(0), vt_spec(2),
                  _const_spec((HEAD_W, 1))],
        out_specs=att_out_spec(ATT_TQ),
        out_shape=att_out_shape,
        scratch_shapes=[pltpu.VMEM((ATT_TK, ATT_TK), f32), pltpu.VMEM((ATT_TK, HEAD_W), bf16),
                        pltpu.VMEM((2, 2 * ATT_TQ // CHAIN_W, ATT_TK, CHAIN_W), f32),
                        pltpu.VMEM((1, 2 * ATT_TQ), f32), pltpu.VMEM((1, 2 * ATT_TQ), f32),
                        pltpu.VMEM((HEAD_W, 2 * ATT_TQ), f32)],
        compiler_params=_tc_params(3, ATTN_FLAGS),
        name="diff_attn",
    )(slopes, lam, qvt, k3, qvt, diff_subln[0].reshape(-1, 1).astype(f32))

    b = pl.pallas_call(
        _sb_attn_kernel,
        grid=(B, N_HEADS, S // SB_TQ),
        in_specs=[q_spec(1, SB_TQ), k_spec(1), vt_spec(3)],
        out_specs=att_out_spec(SB_TQ),
        out_shape=att_out_shape,
        scratch_shapes=[pltpu.VMEM((1, SB_TQ), f32), pltpu.VMEM((HEAD_W, SB_TQ), f32)],
        compiler_params=_tc_params(3, ATTN_FLAGS),
        name="sb_attn",
    )(qvt, k3, qvt)

    out = pl.pallas_call(
        _mix_ffn2_kernel,
        grid=(T // FFN_TM,),
        in_specs=[tok_spec,
                  pl.BlockSpec((FFN_TM, ATT_W), lambda t: (t, 0)),
                  pl.BlockSpec((FFN_TM, ATT_W), lambda t: (t, 0)),
                  pl.BlockSpec((FFN_TM, GATE_W), lambda t: (t, 0)),
                  hbm_spec, hbm_spec, hbm_spec, _const_spec((1, D)),
                  hbm_spec, hbm_spec, hbm_spec, _const_spec((1, D))],
        out_specs=tok_spec,
        out_shape=jax.ShapeDtypeStruct((T, D), f32),
        scratch_shapes=[pltpu.VMEM((ATT_W, D), bf16), pltpu.VMEM((ATT_W, D), bf16),
                        pltpu.VMEM((D, D), bf16)] + ffn_weight_scratch + ffn_stage_scratch,
        compiler_params=_tc_params(1),
        name="mix_ffn2",
    )(h1, a.reshape(T, -1), b.reshape(T, -1), gates,
      w_branch_diff[0], w_branch_sb[0], w_out[0],
      row(ffn2_norm[0]), ffn2_w_gate[0], ffn2_w_up[0], ffn2_w_down[0], row(final_norm))
    return out.reshape(B, S, D)
```

```python
import math

import jax
import jax.numpy as jnp
from jax import lax
from jax.experimental import pallas as pl
from jax.experimental.pallas import tpu as pltpu

D_MODEL = 1024
D_FF = 2816
N_HEADS = 4
HEAD_W = 128
DA_QK_DIM = 64
ATT_W = N_HEADS * HEAD_W
K_W = 2 * ATT_W
QVT_W = 4 * ATT_W
GATE_W = 2 * D_MODEL
NORM_EPS = 1e-5
LAMBDA_INIT = 0.8 - 0.6 * math.exp(-0.3 * 0)
LOG2E = 1.0 / math.log(2.0)

VMEM_LIMIT_BYTES = 56 * 1024 * 1024

STAGE_SLOTS = 4
STAGE_WIDE_ROWS = 64
STAGE_TALL_ROWS = 176
STAGE_SQUARE_ROWS = 128
STAGE_IN_ROWS = 128

FFN_TM = 512
PROJ_TM = 512
ATT_TQ = 2048
ATT_TK = 512
SB_TQ = 2048
CHAIN_W = 256
SB_DONE_LOG2 = 160.0
SUM_ROWS = 16

_NT = (((1,), (1,)), ((), ()))


def _rms(x, g):
    ms = jnp.mean(x * x, axis=-1, keepdims=True)
    return x * lax.rsqrt(ms + NORM_EPS) * g


def _swiglu_half_step(x, norm_g, wg_ref, wu_ref, wd_ref):
    xn = _rms(x, norm_g).astype(jnp.bfloat16)
    g = jnp.dot(xn, wg_ref[...], preferred_element_type=jnp.float32)
    u = jnp.dot(xn, wu_ref[...], preferred_element_type=jnp.float32)
    hact = (g * jax.nn.sigmoid(g) * u).astype(jnp.bfloat16)
    return x + 0.5 * jnp.dot(hact, wd_ref[...], preferred_element_type=jnp.float32)


def _stage_weight(src_hbm, dst_ref, stage_ref, sem_ref, *, rows, col0=0, ncols=None,
                  dst_row0=0, dst_col0=0, transpose=False):
    n_rows = src_hbm.shape[0]
    ncols = src_hbm.shape[1] - col0 if ncols is None else ncols
    n_slots = stage_ref.shape[0]
    assert n_rows % rows == 0 and rows <= stage_ref.shape[1] and ncols <= stage_ref.shape[2]
    n_slabs = n_rows // rows

    def slab_copy(c):
        return pltpu.make_async_copy(
            src_hbm.at[pl.ds(c * rows, rows), pl.ds(col0, ncols)],
            stage_ref.at[c % n_slots, pl.ds(0, rows), pl.ds(0, ncols)],
            sem_ref.at[c % n_slots])

    for c in range(min(n_slots - 1, n_slabs)):
        slab_copy(c).start()
    for c in range(n_slabs):
        if c + n_slots - 1 < n_slabs:
            slab_copy(c + n_slots - 1).start()
        slab_copy(c).wait()
        slab = stage_ref[c % n_slots, :rows, :ncols]
        if transpose:
            dst_ref[dst_row0:dst_row0 + ncols,
                    dst_col0 + c * rows:dst_col0 + (c + 1) * rows] = slab.T.astype(dst_ref.dtype)
        else:
            dst_ref[dst_row0 + c * rows:dst_row0 + (c + 1) * rows,
                    dst_col0:dst_col0 + ncols] = slab.astype(dst_ref.dtype)


def _stage_ffn_weights(wg_hbm, wu_hbm, wd_hbm, wg_ref, wu_ref, wd_ref, stage_wide, stage_tall, sem):
    _stage_weight(wg_hbm, wg_ref, stage_wide, sem, rows=STAGE_WIDE_ROWS)
    _stage_weight(wu_hbm, wu_ref, stage_wide, sem, rows=STAGE_WIDE_ROWS)
    _stage_weight(wd_hbm, wd_ref, stage_tall, sem, rows=STAGE_TALL_ROWS)


def _ffn1_kernel(x_ref, norm_ref, wg_hbm, wu_hbm, wd_hbm, o_ref,
                 wg_ref, wu_ref, wd_ref, stage_wide, stage_tall, sem):
    @pl.when(pl.program_id(0) == 0)
    def _():
        _stage_ffn_weights(wg_hbm, wu_hbm, wd_hbm, wg_ref, wu_ref, wd_ref,
                           stage_wide, stage_tall, sem)

    o_ref[...] = _swiglu_half_step(x_ref[...], norm_ref[...], wg_ref, wu_ref, wd_ref)


def _in_proj_kernel(h_ref, norm_ref, win_hbm, rowscale_ref, bgate_ref,
                    k_ref, qvt_ref, gate_ref, wk_ref, wqvt_ref, wgate_ref, stage, sem):
    @pl.when(pl.program_id(0) == 0)
    def _():
        w = ATT_W
        for piece, src_block in enumerate((1, 4)):
            _stage_weight(win_hbm, wk_ref, stage, sem, rows=STAGE_IN_ROWS,
                          col0=src_block * w, ncols=w, dst_col0=piece * w)
        _stage_weight(win_hbm, wgate_ref, stage, sem, rows=STAGE_IN_ROWS, col0=6 * w, ncols=GATE_W)
        for piece, src_block in enumerate((0, 3, 2, 5)):
            _stage_weight(win_hbm, wqvt_ref, stage, sem, rows=STAGE_IN_ROWS,
                          col0=src_block * w, ncols=w, dst_row0=piece * w, transpose=True)

    n = _rms(h_ref[...], norm_ref[...]).astype(jnp.bfloat16)
    g = jnp.dot(n, wgate_ref[...], preferred_element_type=jnp.float32)
    gate_ref[...] = jax.nn.sigmoid(g + bgate_ref[...]).astype(jnp.bfloat16)
    k_ref[...] = jnp.dot(n, wk_ref[...], preferred_element_type=jnp.float32).astype(jnp.bfloat16)
    qvt = lax.dot_general(wqvt_ref[...], n, _NT, preferred_element_type=jnp.float32)
    qvt_ref[...] = (qvt * rowscale_ref[...]).astype(jnp.bfloat16)


def _emit_pipelined(stages, n):
    state = [dict() for _ in range(n)]
    for step in range(n + len(stages) - 1):
        for s, stage in enumerate(stages):
            t = step - s
            if 0 <= t < n:
                stage(t, state[t], state[t + 1] if t + 1 < n else None)


def _ordered_after(x, token):
    zero = lax.shift_right_logical(
        lax.shift_right_logical(pltpu.bitcast(token, jnp.uint32), jnp.uint32(16)), jnp.uint32(16))
    return pltpu.bitcast(pltpu.bitcast(x, jnp.uint32) + zero, jnp.float32)


def _diff_attn_kernel(slope_ref, lam_ref, q_ref, k_ref, vt_ref, subln_ref, o_ref,
                      mask_ref, kfeat_ref, s_ref, m_ref, l_ref, acc_ref):
    tq, tk, cw = ATT_TQ, ATT_TK, CHAIN_W
    per_map = tq // cw
    n_chains = 2 * per_map
    n_diag = tq // tk
    assert n_diag % 2 == 0 and tk == 2 * cw
    h = pl.program_id(1)
    i = pl.program_id(2)
    slope = slope_ref[h] * LOG2E
    lam = lam_ref[0]

    @pl.when(jnp.logical_and(pl.program_id(0) == 0, jnp.logical_and(h == 0, i == 0)))
    def _():
        krow = lax.broadcasted_iota(jnp.int32, (tk, tk), 0)
        qcol = lax.broadcasted_iota(jnp.int32, (tk, tk), 1)
        mask_ref[...] = jnp.where(qcol >= krow, 0.0, -jnp.inf)
        kpos = lax.broadcasted_iota(jnp.int32, (tk, HEAD_W), 0)
        klane = lax.broadcasted_iota(jnp.int32, (tk, HEAD_W), 1)
        k_hi = jnp.where(kpos >= 256, 256, 0)
        kfeat_ref[...] = jnp.where(klane < 3, k_hi, jnp.where(klane < 6, kpos - k_hi, 0)
                                   ).astype(jnp.float32).astype(jnp.bfloat16)

    qt = q_ref[...]
    chan = lax.broadcasted_iota(jnp.int32, (HEAD_W, tq), 0)
    zero = jnp.zeros_like(qt)
    q_maps = (jnp.where(chan < DA_QK_DIM, qt, zero), jnp.where(chan >= DA_QK_DIM, qt, zero))

    sl = jnp.full((HEAD_W, cw), slope, jnp.float32)
    hi = sl.astype(jnp.bfloat16).astype(jnp.float32)
    mid = (sl - hi).astype(jnp.bfloat16).astype(jnp.float32)
    lo = sl - hi - mid
    frow = lax.broadcasted_iota(jnp.int32, (HEAD_W, cw), 0)
    part = frow % 3
    q_feat = jnp.where(frow < 6, jnp.where(part == 0, hi, jnp.where(part == 1, mid, lo)),
                       0.0).astype(jnp.bfloat16)
    q_chain = [jnp.concatenate(
        [q_maps[c // per_map][:, (c % per_map) * cw:(c % per_map + 1) * cw], q_feat], axis=0)
        for c in range(n_chains)]

    m_ref[...] = jnp.full_like(m_ref, -jnp.inf)
    l_ref[...] = jnp.zeros_like(l_ref)
    acc_ref[...] = jnp.zeros_like(acc_ref)

    def chain_mode(c, d):
        if d is None:
            return "full"
        q_lo = (c % per_map) * cw
        if q_lo + cw <= d * tk:
            return "skip"
        if q_lo >= (d + 1) * tk:
            return "full"
        return q_lo - d * tk

    def scores_to(slot, j, d=None):
        kb = k_ref[0, pl.ds(pl.multiple_of(j * tk, tk), tk), :]
        kb = jnp.concatenate([kb, kfeat_ref[...]], axis=1)
        for c in range(n_chains):
            if chain_mode(c, d) != "skip":
                s_ref[slot, c] = jnp.dot(kb, q_chain[c], preferred_element_type=jnp.float32)

    def consume(slot, j, d=None):
        chains = [c for c in range(n_chains) if chain_mode(c, d) != "skip"]
        vtb = vt_ref[:, pl.ds(pl.multiple_of(j * tk, tk), tk)]
        vtb = jnp.concatenate([vtb, jnp.ones((SUM_ROWS, tk), vtb.dtype)], axis=0)
        shift = -slope * (i * tq - j * tk).astype(jnp.float32)

        def column_max(t, st, nxt):
            c = chains[t]
            s = s_ref[slot, c]
            mode = chain_mode(c, d)
            if mode != "full":
                s = s + mask_ref[:, mode:mode + cw]
                s_ref[slot, c] = s
            st["cmax"] = jnp.max(s, axis=0, keepdims=True) + shift

        def softmax_pv(t, st, nxt):
            c = chains[t]
            lanes = slice(c * cw, (c + 1) * cw)
            m_prev = m_ref[:, lanes]
            m_new = jnp.maximum(m_prev, st.pop("cmax"))
            st["alpha"] = jnp.exp2(m_prev - m_new)
            p = jnp.exp2(s_ref[slot, c] - (m_new - shift))
            m_ref[:, lanes] = m_new
            st["pv"] = jnp.dot(vtb, p.astype(jnp.bfloat16),
                               preferred_element_type=jnp.float32)

        def accumulate(t, st, nxt):
            c = chains[t]
            lanes = slice(c * cw, (c + 1) * cw)
            alpha, pv = st.pop("alpha"), st.pop("pv")
            acc_ref[:, lanes] = alpha * acc_ref[:, lanes] + pv[:HEAD_W]
            l_ref[:, lanes] = alpha * l_ref[:, lanes] + pv[HEAD_W:HEAD_W + 1]

        _emit_pipelined((column_max, softmax_pv, accumulate), len(chains))

    def step(slot, j):
        scores_to(1 - slot, j + 1)
        consume(slot, j)

    scores_to(0, 0)

    def pair(jj, carry):
        step(0, 2 * jj)
        step(1, 2 * jj + 1)
        return carry

    first_diag = n_diag * i
    lax.fori_loop(0, first_diag // 2, pair, 0)
    for d in range(n_diag):
        if d + 1 < n_diag:
            scores_to((d + 1) % 2, first_diag + d + 1, d + 1)
        consume(d % 2, first_diag + d, d)

    o = acc_ref[...] / l_ref[...]
    a = o[:, :tq] - lam * o[:, tq:]
    ms = jnp.mean(a * a, axis=0, keepdims=True)
    a = a * lax.rsqrt(ms + NORM_EPS) * subln_ref[...] * (1.0 - LAMBDA_INIT)
    o_ref[0] = a.T.astype(o_ref.dtype)


def _sb_attn_kernel(q_ref, k_ref, vt_ref, o_ref, c_ref, acc_ref):
    tq, cw = SB_TQ, CHAIN_W
    n_chains = tq // cw
    i = pl.program_id(2)
    qt = q_ref[...]
    q_chain = [qt[:, c * cw:(c + 1) * cw] for c in range(n_chains)]

    krow = lax.broadcasted_iota(jnp.int32, (cw, cw), 0)
    qcol = lax.broadcasted_iota(jnp.int32, (cw, cw), 1)
    strict = krow < qcol
    lrow = lax.broadcasted_iota(jnp.int32, (cw + SUM_ROWS, cw), 0)
    lcol = lax.broadcasted_iota(jnp.int32, (cw + SUM_ROWS, cw), 1)
    later = jnp.where(jnp.logical_or(lcol > lrow, lrow >= cw), 1.0, 0.0).astype(jnp.bfloat16)

    c_ref[...] = jnp.zeros_like(c_ref)
    acc_ref[...] = jnp.zeros_like(acc_ref)

    def run_pieces(pieces):
        def scores(t, st, nxt):
            sub, c, _ = pieces[t]
            start = pl.multiple_of(sub * cw, cw)
            st["z"] = jnp.dot(k_ref[0, pl.ds(start, cw), :], q_chain[c],
                              preferred_element_type=jnp.float32)

        def suffix(t, st, nxt):
            _, _, triangular = pieces[t]
            z = st.pop("z")
            u = jnp.maximum(z, 0.0) + jnp.log2(1.0 + jnp.exp2(-jnp.abs(z)))
            st["log_sig"] = z - u
            if triangular:
                u = jnp.where(strict, u, 0.0)
            st["tail"] = jnp.dot(later, u.astype(jnp.bfloat16),
                                 preferred_element_type=jnp.float32)

        def weights_pv(t, st, nxt):
            sub, _, triangular = pieces[t]
            start = pl.multiple_of(sub * cw, cw)
            tail = st.pop("tail")
            st["usum"] = tail[cw:cw + 1]
            a = jnp.exp2(st.pop("log_sig") - tail[:cw])
            if triangular:
                a = jnp.where(strict, a, 0.0)
            st["pv"] = jnp.dot(vt_ref[:, pl.ds(start, cw)], a.astype(jnp.bfloat16),
                               preferred_element_type=jnp.float32)

        def accumulate(t, st, nxt):
            _, c, _ = pieces[t]
            lanes = slice(c * cw, (c + 1) * cw)
            carry = c_ref[:, lanes]
            acc_ref[:, lanes] += st.pop("pv") * jnp.exp2(-carry)
            c_ref[:, lanes] = carry + st.pop("usum")

        _emit_pipelined((scores, suffix, weights_pv, accumulate), len(pieces))

    diag = [n_chains * i + c for c in range(n_chains)]
    head = [(diag[c], c, True) for c in reversed(range(n_chains))]
    second = [(diag[c] - 1, c, False) for c in reversed(range(n_chains))]

    @pl.when(i == 0)
    def _():
        run_pieces(head + [p for p in second if p[1] > 0])

    @pl.when(i > 0)
    def _():
        run_pieces(head + second)

    def unfinished(c, depth):
        lanes = slice(c * cw, (c + 1) * cw)
        return jnp.logical_and(diag[c] - depth >= 0,
                               jnp.min(c_ref[:, lanes]) < SB_DONE_LOG2)

    def any_unfinished(depth):
        go = unfinished(0, depth)
        for c in range(1, n_chains):
            go = jnp.logical_or(go, unfinished(c, depth))
        return go

    def body(carry):
        depth, _ = carry
        for c in range(n_chains):
            @pl.when(unfinished(c, depth))
            def _():
                run_pieces([(diag[c] - depth, c, False)])
        return depth + 1, any_unfinished(depth + 1)

    lax.while_loop(lambda carry: carry[1], body, (jnp.int32(2), any_unfinished(2)))
    o_ref[0] = acc_ref[...].T.astype(o_ref.dtype)


def _mix_ffn2_kernel(h_ref, a_ref, b_ref, gate_ref, wa_hbm, wb_hbm, wout_hbm,
                     norm2_ref, wg_hbm, wu_hbm, wd_hbm, normf_ref, o_ref,
                     wa_ref, wb_ref, wout_ref, wg_ref, wu_ref, wd_ref,
                     stage_wide, stage_tall, sem):
    @pl.when(pl.program_id(0) == 0)
    def _():
        for src, dst in ((wa_hbm, wa_ref), (wb_hbm, wb_ref), (wout_hbm, wout_ref)):
            _stage_weight(src, dst, stage_tall, sem, rows=STAGE_SQUARE_ROWS)
        _stage_ffn_weights(wg_hbm, wu_hbm, wd_hbm, wg_ref, wu_ref, wd_ref,
                           stage_wide, stage_tall, sem)

    ya = jnp.dot(a_ref[...], wa_ref[...], preferred_element_type=jnp.float32)
    yb = jnp.dot(b_ref[...], wb_ref[...], preferred_element_type=jnp.float32)
    gate = gate_ref[...].astype(jnp.float32)
    y = (gate[:, :D_MODEL] * ya + gate[:, D_MODEL:] * yb).astype(jnp.bfloat16)
    h2 = h_ref[...] + jnp.dot(y, wout_ref[...], preferred_element_type=jnp.float32)
    h3 = _swiglu_half_step(h2, norm2_ref[...], wg_ref, wu_ref, wd_ref)
    o_ref[...] = _rms(h3, normf_ref[...])


def _const_spec(shape):
    return pl.BlockSpec(shape, lambda *_: (0,) * len(shape), pipeline_mode=pl.Buffered(1))


ATTN_FLAGS = None


def _tc_params(n_axes, flags=None):
    return pltpu.CompilerParams(dimension_semantics=("arbitrary",) * n_axes,
                                vmem_limit_bytes=VMEM_LIMIT_BYTES, flags=flags)


def kernel(x, ffn1_norm, ffn1_w_gate, ffn1_w_up, ffn1_w_down, mix_norm, w_in, b_gate, lambda_q1, lambda_k1, lambda_q2, lambda_k2, diff_subln, w_branch_diff, w_branch_sb, w_out, ffn2_norm, ffn2_w_gate, ffn2_w_up, ffn2_w_down, final_norm):
    B, S, D = x.shape
    T = B * S
    f32, bf16 = jnp.float32, jnp.bfloat16
    xt = x.reshape(T, D)
    row = lambda v: v.reshape(1, -1).astype(f32)

    tok_spec = pl.BlockSpec((FFN_TM, D), lambda t: (t, 0))
    hbm_spec = pl.BlockSpec(memory_space=pl.ANY)
    ffn_weight_scratch = [pltpu.VMEM((D, D_FF), bf16), pltpu.VMEM((D, D_FF), bf16),
                          pltpu.VMEM((D_FF, D), bf16)]
    ffn_stage_scratch = [pltpu.VMEM((STAGE_SLOTS, STAGE_WIDE_ROWS, D_FF), f32),
                         pltpu.VMEM((STAGE_SLOTS, STAGE_TALL_ROWS, D), f32),
                         pltpu.SemaphoreType.DMA((STAGE_SLOTS,))]
    h1 = pl.pallas_call(
        _ffn1_kernel,
        grid=(T // FFN_TM,),
        in_specs=[tok_spec, _const_spec((1, D)), hbm_spec, hbm_spec, hbm_spec],
        out_specs=tok_spec,
        out_shape=jax.ShapeDtypeStruct((T, D), f32),
        scratch_shapes=ffn_weight_scratch + ffn_stage_scratch,
        compiler_params=_tc_params(1),
        name="ffn1",
    )(xt, row(ffn1_norm[0]), ffn1_w_gate[0], ffn1_w_up[0], ffn1_w_down[0])

    rowscale = jnp.ones((QVT_W,), f32)
    rowscale = rowscale.at[0:ATT_W].set(DA_QK_DIM ** -0.5 * LOG2E)
    rowscale = rowscale.at[ATT_W:2 * ATT_W].set(HEAD_W ** -0.5 * LOG2E)
    k, qvt, gates = pl.pallas_call(
        _in_proj_kernel,
        grid=(T // PROJ_TM,),
        in_specs=[pl.BlockSpec((PROJ_TM, D), lambda t: (t, 0)), _const_spec((1, D)), hbm_spec,
                  _const_spec((QVT_W, 1)), _const_spec((1, GATE_W))],
        out_specs=[pl.BlockSpec((PROJ_TM, K_W), lambda t: (t, 0)),
                   pl.BlockSpec((QVT_W, PROJ_TM), lambda t: (0, t)),
                   pl.BlockSpec((PROJ_TM, GATE_W), lambda t: (t, 0))],
        out_shape=[jax.ShapeDtypeStruct((T, K_W), bf16),
                   jax.ShapeDtypeStruct((QVT_W, T), bf16),
                   jax.ShapeDtypeStruct((T, GATE_W), bf16)],
        scratch_shapes=[pltpu.VMEM((D, K_W), bf16), pltpu.VMEM((QVT_W, D), bf16),
                        pltpu.VMEM((D, GATE_W), bf16),
                        pltpu.VMEM((STAGE_SLOTS, STAGE_IN_ROWS, GATE_W), f32),
                        pltpu.SemaphoreType.DMA((STAGE_SLOTS,))],
        compiler_params=_tc_params(1),
        name="in_proj",
    )(h1, row(mix_norm[0]), w_in[0], rowscale.reshape(-1, 1), row(b_gate[0]))
    k3 = k.reshape(B, S, K_W)

    def q_spec(slab, tq):
        return pl.BlockSpec((HEAD_W, tq),
                            lambda b, h, i: (slab * N_HEADS + h, b * (S // tq) + i))

    def k_spec(slab):
        return pl.BlockSpec((1, S, HEAD_W), lambda b, h, i: (b, 0, slab * N_HEADS + h))

    def vt_spec(slab):
        return pl.BlockSpec((HEAD_W, S), lambda b, h, i: (slab * N_HEADS + h, b))

    def att_out_spec(tq):
        return pl.BlockSpec((1, tq, HEAD_W), lambda b, h, i: (b, i, h))

    att_out_shape = jax.ShapeDtypeStruct((B, S, ATT_W), bf16)
    smem_spec = pl.BlockSpec(memory_space=pltpu.SMEM)

    lam = (jnp.exp(jnp.sum(lambda_q1[0].astype(f32) * lambda_k1[0].astype(f32)))
           - jnp.exp(jnp.sum(lambda_q2[0].astype(f32) * lambda_k2[0].astype(f32)))
           + LAMBDA_INIT).reshape(1)
    slopes = jnp.exp2(-8.0 * jnp.arange(1, N_HEADS + 1, dtype=f32) / N_HEADS)

    a = pl.pallas_call(
        _diff_attn_kernel,
        grid=(B, N_HEADS, S // ATT_TQ),
        in_specs=[smem_spec, smem_spec, q_spec(0, ATT_TQ), k_spec(0), vt_spec(2),
                  _const_spec((HEAD_W, 1))],
        out_specs=att_out_spec(ATT_TQ),
        out_shape=att_out_shape,
        scratch_shapes=[pltpu.VMEM((ATT_TK, ATT_TK), f32), pltpu.VMEM((ATT_TK, HEAD_W), bf16),
                        pltpu.VMEM((2, 2 * ATT_TQ // CHAIN_W, ATT_TK, CHAIN_W), f32),
                        pltpu.VMEM((1, 2 * ATT_TQ), f32), pltpu.VMEM((1, 2 * ATT_TQ), f32),
                        pltpu.VMEM((HEAD_W, 2 * ATT_TQ), f32)],
        compiler_params=_tc_params(3, ATTN_FLAGS),
        name="diff_attn",
    )(slopes, lam, qvt, k3, qvt, diff_subln[0].reshape(-1, 1).astype(f32))

    b = pl.pallas_call(
        _sb_attn_kernel,
        grid=(B, N_HEADS, S // SB_TQ),
        in_specs=[q_spec(1, SB_TQ), k_spec(1), vt_spec(3)],
        out_specs=att_out_spec(SB_TQ),
        out_shape=att_out_shape,
        scratch_shapes=[pltpu.VMEM((1, SB_TQ), f32), pltpu.VMEM((HEAD_W, SB_TQ), f32)],
        compiler_params=_tc_params(3, ATTN_FLAGS),
        name="sb_attn",
    )(qvt, k3, qvt)

    out = pl.pallas_call(
        _mix_ffn2_kernel,
        grid=(T // FFN_TM,),
        in_specs=[tok_spec,
                  pl.BlockSpec((FFN_TM, ATT_W), lambda t: (t, 0)),
                  pl.BlockSpec((FFN_TM, ATT_W), lambda t: (t, 0)),
                  pl.BlockSpec((FFN_TM, GATE_W), lambda t: (t, 0)),
                  hbm_spec, hbm_spec, hbm_spec, _const_spec((1, D)),
                  hbm_spec, hbm_spec, hbm_spec, _const_spec((1, D))],
        out_specs=tok_spec,
        out_shape=jax.ShapeDtypeStruct((T, D), f32),
        scratch_shapes=[pltpu.VMEM((ATT_W, D), bf16), pltpu.VMEM((ATT_W, D), bf16),
                        pltpu.VMEM((D, D), bf16)] + ffn_weight_scratch + ffn_stage_scratch,
        compiler_params=_tc_params(1),
        name="mix_ffn2",
    )(h1, a.reshape(T, -1), b.reshape(T, -1), gates,
      w_branch_diff[0], w_branch_sb[0], w_out[0],
      row(ffn2_norm[0]), ffn2_w_gate[0], ffn2_w_up[0], ffn2_w_down[0], row(final_norm))
    return out.reshape(B, S, D)
```

```python
import math

import jax
import jax.numpy as jnp
from jax import lax
from jax.experimental import pallas as pl
from jax.experimental.pallas import tpu as pltpu

D_MODEL = 1024
D_FF = 2816
N_HEADS = 4
HEAD_W = 128
DA_QK_DIM = 64
ATT_W = N_HEADS * HEAD_W
K_W = 2 * ATT_W
QVT_W = 4 * ATT_W
GATE_W = 2 * D_MODEL
NORM_EPS = 1e-5
LAMBDA_INIT = 0.8 - 0.6 * math.exp(-0.3 * 0)
LOG2E = 1.0 / math.log(2.0)

VMEM_LIMIT_BYTES = 56 * 1024 * 1024

STAGE_SLOTS = 4
STAGE_WIDE_ROWS = 64
STAGE_TALL_ROWS = 176
STAGE_SQUARE_ROWS = 128
STAGE_IN_ROWS = 128

FFN_TM = 512
PROJ_TM = 512
ATT_TQ = 2048
ATT_TK = 512
SB_TQ = 2048
CHAIN_W = 256
SB_DONE_LOG2 = 160.0
SUM_ROWS = 16

_NT = (((1,), (1,)), ((), ()))


def _rms(x, g):
    ms = jnp.mean(x * x, axis=-1, keepdims=True)
    return x * lax.rsqrt(ms + NORM_EPS) * g


def _swiglu_half_step(x, norm_g, wg_ref, wu_ref, wd_ref):
    halves = jnp.split(x, 2, axis=0)
    xn = [_rms(h, norm_g).astype(jnp.bfloat16) for h in halves]
    gu = [(jnp.dot(n, wg_ref[...], preferred_element_type=jnp.float32),
           jnp.dot(n, wu_ref[...], preferred_element_type=jnp.float32)) for n in xn]
    out = []
    for h, (g, u) in zip(halves, gu):
        hact = (g * jax.nn.sigmoid(g) * u).astype(jnp.bfloat16)
        out.append(h + 0.5 * jnp.dot(hact, wd_ref[...], preferred_element_type=jnp.float32))
    return jnp.concatenate(out, axis=0)


def _stage_weight(src_hbm, dst_ref, stage_ref, sem_ref, *, rows, col0=0, ncols=None,
                  dst_row0=0, dst_col0=0, transpose=False):
    n_rows = src_hbm.shape[0]
    ncols = src_hbm.shape[1] - col0 if ncols is None else ncols
    n_slots = stage_ref.shape[0]
    assert n_rows % rows == 0 and rows <= stage_ref.shape[1] and ncols <= stage_ref.shape[2]
    n_slabs = n_rows // rows

    def slab_copy(c):
        return pltpu.make_async_copy(
            src_hbm.at[pl.ds(c * rows, rows), pl.ds(col0, ncols)],
            stage_ref.at[c % n_slots, pl.ds(0, rows), pl.ds(0, ncols)],
            sem_ref.at[c % n_slots])

    for c in range(min(n_slots - 1, n_slabs)):
        slab_copy(c).start()
    for c in range(n_slabs):
        if c + n_slots - 1 < n_slabs:
            slab_copy(c + n_slots - 1).start()
        slab_copy(c).wait()
        slab = stage_ref[c % n_slots, :rows, :ncols]
        if transpose:
            dst_ref[dst_row0:dst_row0 + ncols,
                    dst_col0 + c * rows:dst_col0 + (c + 1) * rows] = slab.T.astype(dst_ref.dtype)
        else:
            dst_ref[dst_row0 + c * rows:dst_row0 + (c + 1) * rows,
                    dst_col0:dst_col0 + ncols] = slab.astype(dst_ref.dtype)


def _stage_ffn_weights(wg_hbm, wu_hbm, wd_hbm, wg_ref, wu_ref, wd_ref, stage_wide, stage_tall, sem):
    _stage_weight(wg_hbm, wg_ref, stage_wide, sem, rows=STAGE_WIDE_ROWS)
    _stage_weight(wu_hbm, wu_ref, stage_wide, sem, rows=STAGE_WIDE_ROWS)
    _stage_weight(wd_hbm, wd_ref, stage_tall, sem, rows=STAGE_TALL_ROWS)


def _ffn1_kernel(x_ref, norm_ref, wg_hbm, wu_hbm, wd_hbm, o_ref,
                 wg_ref, wu_ref, wd_ref, stage_wide, stage_tall, sem):
    @pl.when(pl.program_id(0) == 0)
    def _():
        _stage_ffn_weights(wg_hbm, wu_hbm, wd_hbm, wg_ref, wu_ref, wd_ref,
                           stage_wide, stage_tall, sem)

    o_ref[...] = _swiglu_half_step(x_ref[...], norm_ref[...], wg_ref, wu_ref, wd_ref)


def _in_proj_kernel(h_ref, norm_ref, win_hbm, rowscale_ref, bgate_ref,
                    k_ref, qvt_ref, gate_ref, wk_ref, wqvt_ref, wgate_ref, stage, sem):
    @pl.when(pl.program_id(0) == 0)
    def _():
        w = ATT_W
        for piece, src_block in enumerate((1, 4)):
            _stage_weight(win_hbm, wk_ref, stage, sem, rows=STAGE_IN_ROWS,
                          col0=src_block * w, ncols=w, dst_col0=piece * w)
        _stage_weight(win_hbm, wgate_ref, stage, sem, rows=STAGE_IN_ROWS, col0=6 * w, ncols=GATE_W)
        for piece, src_block in enumerate((0, 3, 2, 5)):
            _stage_weight(win_hbm, wqvt_ref, stage, sem, rows=STAGE_IN_ROWS,
                          col0=src_block * w, ncols=w, dst_row0=piece * w, transpose=True)

    half = h_ref.shape[0] // 2
    rows = [slice(0, half), slice(half, 2 * half)]
    n = [_rms(h_ref[r, :], norm_ref[...]).astype(jnp.bfloat16) for r in rows]
    g = [jnp.dot(nh, wgate_ref[...], preferred_element_type=jnp.float32) for nh in n]
    for r, gh in zip(rows, g):
        gate_ref[r, :] = jax.nn.sigmoid(gh + bgate_ref[...]).astype(jnp.bfloat16)
    for r, nh in zip(rows, n):
        k_ref[r, :] = jnp.dot(nh, wk_ref[...],
                              preferred_element_type=jnp.float32).astype(jnp.bfloat16)
    for r, nh in zip(rows, n):
        qvt = lax.dot_general(wqvt_ref[...], nh, _NT, preferred_element_type=jnp.float32)
        qvt_ref[:, r] = (qvt * rowscale_ref[...]).astype(jnp.bfloat16)


def _emit_pipelined(stages, n):
    state = [dict() for _ in range(n)]
    for step in range(n + len(stages) - 1):
        for s, stage in enumerate(stages):
            t = step - s
            if 0 <= t < n:
                stage(t, state[t], state[t + 1] if t + 1 < n else None)


def _ordered_after(x, token):
    zero = lax.shift_right_logical(
        lax.shift_right_logical(pltpu.bitcast(token, jnp.uint32), jnp.uint32(16)), jnp.uint32(16))
    return pltpu.bitcast(pltpu.bitcast(x, jnp.uint32) + zero, jnp.float32)


def _diff_attn_kernel(slope_ref, lam_ref, q_ref, k_ref, vt_ref, subln_ref, o_ref,
                      mask_ref, kfeat_ref, s_ref, m_ref, l_ref, acc_ref):
    tq, tk, cw = ATT_TQ, ATT_TK, CHAIN_W
    per_map = tq // cw
    n_chains = 2 * per_map
    n_diag = tq // tk
    assert n_diag % 2 == 0 and tk == 2 * cw
    h = pl.program_id(1)
    i = pl.program_id(2)
    slope = slope_ref[h] * LOG2E
    lam = lam_ref[0]

    @pl.when(jnp.logical_and(pl.program_id(0) == 0, jnp.logical_and(h == 0, i == 0)))
    def _():
        krow = lax.broadcasted_iota(jnp.int32, (tk, tk), 0)
        qcol = lax.broadcasted_iota(jnp.int32, (tk, tk), 1)
        mask_ref[...] = jnp.where(qcol >= krow, 0.0, -jnp.inf)
        kpos = lax.broadcasted_iota(jnp.int32, (tk, HEAD_W), 0)
        klane = lax.broadcasted_iota(jnp.int32, (tk, HEAD_W), 1)
        k_hi = jnp.where(kpos >= 256, 256, 0)
        kfeat_ref[...] = jnp.where(klane < 3, k_hi, jnp.where(klane < 6, kpos - k_hi, 0)
                                   ).astype(jnp.float32).astype(jnp.bfloat16)

    qt = q_ref[...]
    chan = lax.broadcasted_iota(jnp.int32, (HEAD_W, tq), 0)
    zero = jnp.zeros_like(qt)
    q_maps = (jnp.where(chan < DA_QK_DIM, qt, zero), jnp.where(chan >= DA_QK_DIM, qt, zero))

    sl = jnp.full((HEAD_W, cw), slope, jnp.float32)
    hi = sl.astype(jnp.bfloat16).astype(jnp.float32)
    mid = (sl - hi).astype(jnp.bfloat16).astype(jnp.float32)
    lo = sl - hi - mid
    frow = lax.broadcasted_iota(jnp.int32, (HEAD_W, cw), 0)
    part = frow % 3
    q_feat = jnp.where(frow < 6, jnp.where(part == 0, hi, jnp.where(part == 1, mid, lo)),
                       0.0).astype(jnp.bfloat16)
    q_chain = [jnp.concatenate(
        [q_maps[c // per_map][:, (c % per_map) * cw:(c % per_map + 1) * cw], q_feat], axis=0)
        for c in range(n_chains)]

    m_ref[...] = jnp.full_like(m_ref, -jnp.inf)
    l_ref[...] = jnp.zeros_like(l_ref)
    acc_ref[...] = jnp.zeros_like(acc_ref)

    def chain_mode(c, d):
        if d is None:
            return "full"
        q_lo = (c % per_map) * cw
        if q_lo + cw <= d * tk:
            return "skip"
        if q_lo >= (d + 1) * tk:
            return "full"
        return q_lo - d * tk

    def scores_to(slot, j, d=None):
        kb = k_ref[0, pl.ds(pl.multiple_of(j * tk, tk), tk), :]
        kb = jnp.concatenate([kb, kfeat_ref[...]], axis=1)
        for c in range(n_chains):
            if chain_mode(c, d) != "skip":
                s_ref[slot, c] = jnp.dot(kb, q_chain[c], preferred_element_type=jnp.float32)

    def consume(slot, j, d=None):
        chains = [c for c in range(n_chains) if chain_mode(c, d) != "skip"]
        vtb = vt_ref[:, pl.ds(pl.multiple_of(j * tk, tk), tk)]
        vtb = jnp.concatenate([vtb, jnp.ones((SUM_ROWS, tk), vtb.dtype)], axis=0)
        shift = -slope * (i * tq - j * tk).astype(jnp.float32)

        def column_max(t, st, nxt):
            c = chains[t]
            s = s_ref[slot, c]
            mode = chain_mode(c, d)
            if mode != "full":
                s = s + mask_ref[:, mode:mode + cw]
                s_ref[slot, c] = s
            st["cmax"] = jnp.max(s, axis=0, keepdims=True) + shift

        def softmax_pv(t, st, nxt):
            c = chains[t]
            lanes = slice(c * cw, (c + 1) * cw)
            m_prev = m_ref[:, lanes]
            m_new = jnp.maximum(m_prev, st.pop("cmax"))
            st["alpha"] = jnp.exp2(m_prev - m_new)
            p = jnp.exp2(s_ref[slot, c] - (m_new - shift))
            m_ref[:, lanes] = m_new
            st["pv"] = jnp.dot(vtb, p.astype(jnp.bfloat16),
                               preferred_element_type=jnp.float32)

        def accumulate(t, st, nxt):
            c = chains[t]
            lanes = slice(c * cw, (c + 1) * cw)
            alpha, pv = st.pop("alpha"), st.pop("pv")
            acc_ref[:, lanes] = alpha * acc_ref[:, lanes] + pv[:HEAD_W]
            l_ref[:, lanes] = alpha * l_ref[:, lanes] + pv[HEAD_W:HEAD_W + 1]

        _emit_pipelined((column_max, softmax_pv, accumulate), len(chains))

    def step(slot, j):
        scores_to(1 - slot, j + 1)
        consume(slot, j)

    scores_to(0, 0)

    def pair(jj, carry):
        step(0, 2 * jj)
        step(1, 2 * jj + 1)
        return carry

    first_diag = n_diag * i
    lax.fori_loop(0, first_diag // 2, pair, 0)
    for d in range(n_diag):
        if d + 1 < n_diag:
            scores_to((d + 1) % 2, first_diag + d + 1, d + 1)
        consume(d % 2, first_diag + d, d)

    o = acc_ref[...] / l_ref[...]
    a = o[:, :tq] - lam * o[:, tq:]
    ms = jnp.mean(a * a, axis=0, keepdims=True)
    a = a * lax.rsqrt(ms + NORM_EPS) * subln_ref[...] * (1.0 - LAMBDA_INIT)
    o_ref[0] = a.T.astype(o_ref.dtype)


def _sb_attn_kernel(q_ref, k_ref, vt_ref, o_ref, c_ref, acc_ref):
    tq, cw = SB_TQ, CHAIN_W
    n_chains = tq // cw
    i = pl.program_id(2)
    qt = q_ref[...]
    q_chain = [qt[:, c * cw:(c + 1) * cw] for c in range(n_chains)]

    krow = lax.broadcasted_iota(jnp.int32, (cw, cw), 0)
    qcol = lax.broadcasted_iota(jnp.int32, (cw, cw), 1)
    strict = krow < qcol
    lrow = lax.broadcasted_iota(jnp.int32, (cw + SUM_ROWS, cw), 0)
    lcol = lax.broadcasted_iota(jnp.int32, (cw + SUM_ROWS, cw), 1)
    later = jnp.where(jnp.logical_or(lcol > lrow, lrow >= cw), 1.0, 0.0).astype(jnp.bfloat16)

    c_ref[...] = jnp.zeros_like(c_ref)
    acc_ref[...] = jnp.zeros_like(acc_ref)

    def run_pieces(pieces):
        def scores(t, st, nxt):
            sub, c, _ = pieces[t]
            start = pl.multiple_of(sub * cw, cw)
            st["z"] = jnp.dot(k_ref[0, pl.ds(start, cw), :], q_chain[c],
                              preferred_element_type=jnp.float32)

        def suffix(t, st, nxt):
            _, _, triangular = pieces[t]
            z = st.pop("z")
            u = jnp.maximum(z, 0.0) + jnp.log2(1.0 + jnp.exp2(-jnp.abs(z)))
            st["log_sig"] = z - u
            if triangular:
                u = jnp.where(strict, u, 0.0)
            st["tail"] = jnp.dot(later, u.astype(jnp.bfloat16),
                                 preferred_element_type=jnp.float32)

        def weights_pv(t, st, nxt):
            sub, _, triangular = pieces[t]
            start = pl.multiple_of(sub * cw, cw)
            tail = st.pop("tail")
            st["usum"] = tail[cw:cw + 1]
            a = jnp.exp2(st.pop("log_sig") - tail[:cw])
            if triangular:
                a = jnp.where(strict, a, 0.0)
            st["pv"] = jnp.dot(vt_ref[:, pl.ds(start, cw)], a.astype(jnp.bfloat16),
                               preferred_element_type=jnp.float32)

        def accumulate(t, st, nxt):
            _, c, _ = pieces[t]
            lanes = slice(c * cw, (c + 1) * cw)
            carry = c_ref[:, lanes]
            acc_ref[:, lanes] += st.pop("pv") * jnp.exp2(-carry)
            c_ref[:, lanes] = carry + st.pop("usum")

        _emit_pipelined((scores, suffix, weights_pv, accumulate), len(pieces))

    diag = [n_chains * i + c for c in range(n_chains)]
    head = [(diag[c], c, True) for c in reversed(range(n_chains))]
    second = [(diag[c] - 1, c, False) for c in reversed(range(n_chains))]

    @pl.when(i == 0)
    def _():
        run_pieces(head + [p for p in second if p[1] > 0])

    @pl.when(i > 0)
    def _():
        run_pieces(head + second)

    def unfinished(c, depth):
        lanes = slice(c * cw, (c + 1) * cw)
        return jnp.logical_and(diag[c] - depth >= 0,
                               jnp.min(c_ref[:, lanes]) < SB_DONE_LOG2)

    def any_unfinished(depth):
        go = unfinished(0, depth)
        for c in range(1, n_chains):
            go = jnp.logical_or(go, unfinished(c, depth))
        return go

    def body(carry):
        depth, _ = carry
        for c in range(n_chains):
            @pl.when(unfinished(c, depth))
            def _():
                run_pieces([(diag[c] - depth, c, False)])
        return depth + 1, any_unfinished(depth + 1)

    lax.while_loop(lambda carry: carry[1], body, (jnp.int32(2), any_unfinished(2)))
    o_ref[0] = acc_ref[...].T.astype(o_ref.dtype)


def _mix_ffn2_kernel(h_ref, a_ref, b_ref, gate_ref, wa_hbm, wb_hbm, wout_hbm,
                     norm2_ref, wg_hbm, wu_hbm, wd_hbm, normf_ref, o_ref,
                     wa_ref, wb_ref, wout_ref, wg_ref, wu_ref, wd_ref,
                     stage_wide, stage_tall, sem):
    @pl.when(pl.program_id(0) == 0)
    def _():
        for src, dst in ((wa_hbm, wa_ref), (wb_hbm, wb_ref), (wout_hbm, wout_ref)):
            _stage_weight(src, dst, stage_tall, sem, rows=STAGE_SQUARE_ROWS)
        _stage_ffn_weights(wg_hbm, wu_hbm, wd_hbm, wg_ref, wu_ref, wd_ref,
                           stage_wide, stage_tall, sem)

    half = h_ref.shape[0] // 2
    rows = [slice(0, half), slice(half, 2 * half)]
    yab = [(jnp.dot(a_ref[r, :], wa_ref[...], preferred_element_type=jnp.float32),
            jnp.dot(b_ref[r, :], wb_ref[...], preferred_element_type=jnp.float32)) for r in rows]
    h2 = []
    for r, (ya, yb) in zip(rows, yab):
        gate = gate_ref[r, :].astype(jnp.float32)
        y = (gate[:, :D_MODEL] * ya + gate[:, D_MODEL:] * yb).astype(jnp.bfloat16)
        h2.append(h_ref[r, :] + jnp.dot(y, wout_ref[...], preferred_element_type=jnp.float32))
    h2 = jnp.concatenate(h2, axis=0)
    h3 = _swiglu_half_step(h2, norm2_ref[...], wg_ref, wu_ref, wd_ref)
    o_ref[...] = _rms(h3, normf_ref[...])


def _const_spec(shape):
    return pl.BlockSpec(shape, lambda *_: (0,) * len(shape), pipeline_mode=pl.Buffered(1))


ATTN_FLAGS = None


def _tc_params(n_axes, flags=None):
    return pltpu.CompilerParams(dimension_semantics=("arbitrary",) * n_axes,
                                vmem_limit_bytes=VMEM_LIMIT_BYTES, flags=flags)


def kernel(x, ffn1_norm, ffn1_w_gate, ffn1_w_up, ffn1_w_down, mix_norm, w_in, b_gate, lambda_q1, lambda_k1, lambda_q2, lambda_k2, diff_subln, w_branch_diff, w_branch_sb, w_out, ffn2_norm, ffn2_w_gate, ffn2_w_up, ffn2_w_down, final_norm):
    B, S, D = x.shape
    T = B * S
    f32, bf16 = jnp.float32, jnp.bfloat16
    xt = x.reshape(T, D)
    row = lambda v: v.reshape(1, -1).astype(f32)

    tok_spec = pl.BlockSpec((FFN_TM, D), lambda t: (t, 0))
    hbm_spec = pl.BlockSpec(memory_space=pl.ANY)
    ffn_weight_scratch = [pltpu.VMEM((D, D_FF), bf16), pltpu.VMEM((D, D_FF), bf16),
                          pltpu.VMEM((D_FF, D), bf16)]
    ffn_stage_scratch = [pltpu.VMEM((STAGE_SLOTS, STAGE_WIDE_ROWS, D_FF), f32),
                         pltpu.VMEM((STAGE_SLOTS, STAGE_TALL_ROWS, D), f32),
                         pltpu.SemaphoreType.DMA((STAGE_SLOTS,))]
    h1 = pl.pallas_call(
        _ffn1_kernel,
        grid=(T // FFN_TM,),
        in_specs=[tok_spec, _const_spec((1, D)), hbm_spec, hbm_spec, hbm_spec],
        out_specs=tok_spec,
        out_shape=jax.ShapeDtypeStruct((T, D), f32),
        scratch_shapes=ffn_weight_scratch + ffn_stage_scratch,
        compiler_params=_tc_params(1),
        name="ffn1",
    )(xt, row(ffn1_norm[0]), ffn1_w_gate[0], ffn1_w_up[0], ffn1_w_down[0])

    rowscale = jnp.ones((QVT_W,), f32)
    rowscale = rowscale.at[0:ATT_W].set(DA_QK_DIM ** -0.5 * LOG2E)
    rowscale = rowscale.at[ATT_W:2 * ATT_W].set(HEAD_W ** -0.5 * LOG2E)
    k, qvt, gates = pl.pallas_call(
        _in_proj_kernel,
        grid=(T // PROJ_TM,),
        in_specs=[pl.BlockSpec((PROJ_TM, D), lambda t: (t, 0)), _const_spec((1, D)), hbm_spec,
                  _const_spec((QVT_W, 1)), _const_spec((1, GATE_W))],
        out_specs=[pl.BlockSpec((PROJ_TM, K_W), lambda t: (t, 0)),
                   pl.BlockSpec((QVT_W, PROJ_TM), lambda t: (0, t)),
                   pl.BlockSpec((PROJ_TM, GATE_W), lambda t: (t, 0))],
        out_shape=[jax.ShapeDtypeStruct((T, K_W), bf16),
                   jax.ShapeDtypeStruct((QVT_W, T), bf16),
                   jax.ShapeDtypeStruct((T, GATE_W), bf16)],
        scratch_shapes=[pltpu.VMEM((D, K_W), bf16), pltpu.VMEM((QVT_W, D), bf16),
                        pltpu.VMEM((D, GATE_W), bf16),
                        pltpu.VMEM((STAGE_SLOTS, STAGE_IN_ROWS, GATE_W), f32),
                        pltpu.SemaphoreType.DMA((STAGE_SLOTS,))],
        compiler_params=_tc_params(1),
        name="in_proj",
    )(h1, row(mix_norm[0]), w_in[0], rowscale.reshape(-1, 1), row(b_gate[0]))
    k3 = k.reshape(B, S, K_W)

    def q_spec(slab, tq):
        return pl.BlockSpec((HEAD_W, tq),
                            lambda b, h, i: (slab * N_HEADS + h, b * (S // tq) + i))

    def k_spec(slab):
        return pl.BlockSpec((1, S, HEAD_W), lambda b, h, i: (b, 0, slab * N_HEADS + h))

    def vt_spec(slab):
        return pl.BlockSpec((HEAD_W, S), lambda b, h, i: (slab * N_HEADS + h, b))

    def att_out_spec(tq):
        return pl.BlockSpec((1, tq, HEAD_W), lambda b, h, i: (b, i, h))

    att_out_shape = jax.ShapeDtypeStruct((B, S, ATT_W), bf16)
    smem_spec = pl.BlockSpec(memory_space=pltpu.SMEM)

    lam = (jnp.exp(jnp.sum(lambda_q1[0].astype(f32) * lambda_k1[0].astype(f32)))
           - jnp.exp(jnp.sum(lambda_q2[0].astype(f32) * lambda_k2[0].astype(f32)))
           + LAMBDA_INIT).reshape(1)
    slopes = jnp.exp2(-8.0 * jnp.arange(1, N_HEADS + 1, dtype=f32) / N_HEADS)

    a = pl.pallas_call(
        _diff_attn_kernel,
        grid=(B, N_HEADS, S // ATT_TQ),
        in_specs=[smem_spec, smem_spec, q_spec(0, ATT_TQ), k_spec(0), vt_spec(2),
                  _const_spec((HEAD_W, 1))],
        out_specs=att_out_spec(ATT_TQ),
        out_shape=att_out_shape,
        scratch_shapes=[pltpu.VMEM((ATT_TK, ATT_TK), f32), pltpu.VMEM((ATT_TK, HEAD_W), bf16),
                        pltpu.VMEM((2, 2 * ATT_TQ // CHAIN_W, ATT_TK, CHAIN_W), f32),
                        pltpu.VMEM((1, 2 * ATT_TQ), f32), pltpu.VMEM((1, 2 * ATT_TQ), f32),
                        pltpu.VMEM((HEAD_W, 2 * ATT_TQ), f32)],
        compiler_params=_tc_params(3, ATTN_FLAGS),
        name="diff_attn",
    )(slopes, lam, qvt, k3, qvt, diff_subln[0].reshape(-1, 1).astype(f32))

    b = pl.pallas_call(
        _sb_attn_kernel,
        grid=(B, N_HEADS, S // SB_TQ),
        in_specs=[q_spec(1, SB_TQ), k_spec(1), vt_spec(3)],
        out_specs=att_out_spec(SB_TQ),
        out_shape=att_out_shape,
        scratch_shapes=[pltpu.VMEM((1, SB_TQ), f32), pltpu.VMEM((HEAD_W, SB_TQ), f32)],
        compiler_params=_tc_params(3, ATTN_FLAGS),
        name="sb_attn",
    )(qvt, k3, qvt)

    out = pl.pallas_call(
        _mix_ffn2_kernel,
        grid=(T // FFN_TM,),
        in_specs=[tok_spec,
                  pl.BlockSpec((FFN_TM, ATT_W), lambda t: (t, 0)),
                  pl.BlockSpec((FFN_TM, ATT_W), lambda t: (t, 0)),
                  pl.BlockSpec((FFN_TM, GATE_W), lambda t: (t, 0)),
                  hbm_spec, hbm_spec, hbm_spec, _const_spec((1, D)),
                  hbm_spec, hbm_spec, hbm_spec, _const_spec((1, D))],
        out_specs=tok_spec,
        out_shape=jax.ShapeDtypeStruct((T, D), f32),
        scratch_shapes=[pltpu.VMEM((ATT_W, D), bf16), pltpu.VMEM((ATT_W, D), bf16),
                        pltpu.VMEM((D, D), bf16)] + ffn_weight_scratch + ffn_stage_scratch,
        compiler_params=_tc_params(1),
        name="mix_ffn2",
    )(h1, a.reshape(T, -1), b.reshape(T, -1), gates,
      w_branch_diff[0], w_branch_sb[0], w_out[0],
      row(ffn2_norm[0]), ffn2_w_gate[0], ffn2_w_up[0], ffn2_w_down[0], row(final_norm))
    return out.reshape(B, S, D)
```

```python
import math

import jax
import jax.numpy as jnp
from jax import lax
from jax.experimental import pallas as pl
from jax.experimental.pallas import tpu as pltpu

D_MODEL = 1024
D_FF = 2816
N_HEADS = 4
HEAD_W = 128
DA_QK_DIM = 64
ATT_W = N_HEADS * HEAD_W
K_W = 2 * ATT_W
QVT_W = 4 * ATT_W
GATE_W = 2 * D_MODEL
NORM_EPS = 1e-5
LAMBDA_INIT = 0.8 - 0.6 * math.exp(-0.3 * 0)
LOG2E = 1.0 / math.log(2.0)

VMEM_LIMIT_BYTES = 56 * 1024 * 1024

STAGE_SLOTS = 4
STAGE_WIDE_ROWS = 64
STAGE_TALL_ROWS = 176
STAGE_SQUARE_ROWS = 128
STAGE_IN_ROWS = 128

FFN_TM = 512
PROJ_TM = 512
ATT_TQ = 2048
ATT_TK = 512
SB_TQ = 2048
CHAIN_W = 256
SB_DONE_LOG2 = 160.0
SUM_ROWS = 16

_NT = (((1,), (1,)), ((), ()))


def _rms(x, g):
    ms = jnp.mean(x * x, axis=-1, keepdims=True)
    return x * lax.rsqrt(ms + NORM_EPS) * g


def _swiglu_half_step(x, norm_g, wg_ref, wu_ref, wd_ref):
    halves = jnp.split(x, 2, axis=0)
    xn = [_rms(h, norm_g).astype(jnp.bfloat16) for h in halves]
    gu = [(jnp.dot(n, wg_ref[...], preferred_element_type=jnp.float32),
           jnp.dot(n, wu_ref[...], preferred_element_type=jnp.float32)) for n in xn]
    out = []
    for h, (g, u) in zip(halves, gu):
        hact = (g * jax.nn.sigmoid(g) * u).astype(jnp.bfloat16)
        out.append(h + 0.5 * jnp.dot(hact, wd_ref[...], preferred_element_type=jnp.float32))
    return jnp.concatenate(out, axis=0)


def _stage_weight(src_hbm, dst_ref, stage_ref, sem_ref, *, rows, col0=0, ncols=None,
                  dst_row0=0, dst_col0=0, transpose=False):
    n_rows = src_hbm.shape[0]
    ncols = src_hbm.shape[1] - col0 if ncols is None else ncols
    n_slots = stage_ref.shape[0]
    assert n_rows % rows == 0 and rows <= stage_ref.shape[1] and ncols <= stage_ref.shape[2]
    n_slabs = n_rows // rows

    def slab_copy(c):
        return pltpu.make_async_copy(
            src_hbm.at[pl.ds(c * rows, rows), pl.ds(col0, ncols)],
            stage_ref.at[c % n_slots, pl.ds(0, rows), pl.ds(0, ncols)],
            sem_ref.at[c % n_slots])

    for c in range(min(n_slots - 1, n_slabs)):
        slab_copy(c).start()
    for c in range(n_slabs):
        if c + n_slots - 1 < n_slabs:
            slab_copy(c + n_slots - 1).start()
        slab_copy(c).wait()
        slab = stage_ref[c % n_slots, :rows, :ncols]
        if transpose:
            dst_ref[dst_row0:dst_row0 + ncols,
                    dst_col0 + c * rows:dst_col0 + (c + 1) * rows] = slab.T.astype(dst_ref.dtype)
        else:
            dst_ref[dst_row0 + c * rows:dst_row0 + (c + 1) * rows,
                    dst_col0:dst_col0 + ncols] = slab.astype(dst_ref.dtype)


def _stage_ffn_weights(wg_hbm, wu_hbm, wd_hbm, wg_ref, wu_ref, wd_ref, stage_wide, stage_tall, sem):
    _stage_weight(wg_hbm, wg_ref, stage_wide, sem, rows=STAGE_WIDE_ROWS)
    _stage_weight(wu_hbm, wu_ref, stage_wide, sem, rows=STAGE_WIDE_ROWS)
    _stage_weight(wd_hbm, wd_ref, stage_tall, sem, rows=STAGE_TALL_ROWS)


def _ffn1_kernel(x_ref, norm_ref, wg_hbm, wu_hbm, wd_hbm, o_ref,
                 wg_ref, wu_ref, wd_ref, stage_wide, stage_tall, sem):
    @pl.when(pl.program_id(0) == 0)
    def _():
        _stage_ffn_weights(wg_hbm, wu_hbm, wd_hbm, wg_ref, wu_ref, wd_ref,
                           stage_wide, stage_tall, sem)

    o_ref[...] = _swiglu_half_step(x_ref[...], norm_ref[...], wg_ref, wu_ref, wd_ref)


def _in_proj_kernel(h_ref, norm_ref, win_hbm, rowscale_ref, bgate_ref,
                    k_ref, qvt_ref, gate_ref, wk_ref, wqvt_ref, wgate_ref, stage, sem):
    @pl.when(pl.program_id(0) == 0)
    def _():
        w = ATT_W
        for piece, src_block in enumerate((1, 4)):
            _stage_weight(win_hbm, wk_ref, stage, sem, rows=STAGE_IN_ROWS,
                          col0=src_block * w, ncols=w, dst_col0=piece * w)
        _stage_weight(win_hbm, wgate_ref, stage, sem, rows=STAGE_IN_ROWS, col0=6 * w, ncols=GATE_W)
        for piece, src_block in enumerate((0, 3, 2, 5)):
            _stage_weight(win_hbm, wqvt_ref, stage, sem, rows=STAGE_IN_ROWS,
                          col0=src_block * w, ncols=w, dst_row0=piece * w, transpose=True)

    half = h_ref.shape[0] // 2
    rows = [slice(0, half), slice(half, 2 * half)]
    n = [_rms(h_ref[r, :], norm_ref[...]).astype(jnp.bfloat16) for r in rows]
    g = [jnp.dot(nh, wgate_ref[...], preferred_element_type=jnp.float32) for nh in n]
    for r, gh in zip(rows, g):
        gate_ref[r, :] = jax.nn.sigmoid(gh + bgate_ref[...]).astype(jnp.bfloat16)
    for r, nh in zip(rows, n):
        k_ref[r, :] = jnp.dot(nh, wk_ref[...],
                              preferred_element_type=jnp.float32).astype(jnp.bfloat16)
    for r, nh in zip(rows, n):
        qvt = lax.dot_general(wqvt_ref[...], nh, _NT, preferred_element_type=jnp.float32)
        qvt_ref[:, r] = (qvt * rowscale_ref[...]).astype(jnp.bfloat16)


def _emit_pipelined(stages, n):
    state = [dict() for _ in range(n)]
    for step in range(n + len(stages) - 1):
        for s, stage in enumerate(stages):
            t = step - s
            if 0 <= t < n:
                stage(t, state[t], state[t + 1] if t + 1 < n else None)


def _ordered_after(x, token):
    zero = lax.shift_right_logical(
        lax.shift_right_logical(pltpu.bitcast(token, jnp.uint32), jnp.uint32(16)), jnp.uint32(16))
    return pltpu.bitcast(pltpu.bitcast(x, jnp.uint32) + zero, jnp.float32)


def _diff_attn_kernel(slope_ref, lam_ref, q_ref, k_ref, vt_ref, subln_ref, o_ref,
                      mask_ref, kfeat_ref, s_ref, m_ref, l_ref, acc_ref):
    tq, tk, cw = ATT_TQ, ATT_TK, CHAIN_W
    per_map = tq // cw
    n_chains = 2 * per_map
    n_diag = tq // tk
    assert n_diag % 2 == 0 and tk == 2 * cw
    h = pl.program_id(1)
    i = pl.program_id(2)
    slope = slope_ref[h] * LOG2E
    lam = lam_ref[0]

    @pl.when(jnp.logical_and(pl.program_id(0) == 0, jnp.logical_and(h == 0, i == 0)))
    def _():
        krow = lax.broadcasted_iota(jnp.int32, (tk, tk), 0)
        qcol = lax.broadcasted_iota(jnp.int32, (tk, tk), 1)
        mask_ref[...] = jnp.where(qcol >= krow, 0.0, -jnp.inf)
        kpos = lax.broadcasted_iota(jnp.int32, (tk, HEAD_W), 0)
        klane = lax.broadcasted_iota(jnp.int32, (tk, HEAD_W), 1)
        k_hi = jnp.where(kpos >= 256, 256, 0)
        kfeat_ref[...] = jnp.where(klane < 3, k_hi, jnp.where(klane < 6, kpos - k_hi, 0)
                                   ).astype(jnp.float32).astype(jnp.bfloat16)

    qt = q_ref[...]
    chan = lax.broadcasted_iota(jnp.int32, (HEAD_W, tq), 0)
    zero = jnp.zeros_like(qt)
    q_maps = (jnp.where(chan < DA_QK_DIM, qt, zero), jnp.where(chan >= DA_QK_DIM, qt, zero))

    sl = jnp.full((HEAD_W, cw), slope, jnp.float32)
    hi = sl.astype(jnp.bfloat16).astype(jnp.float32)
    mid = (sl - hi).astype(jnp.bfloat16).astype(jnp.float32)
    lo = sl - hi - mid
    frow = lax.broadcasted_iota(jnp.int32, (HEAD_W, cw), 0)
    part = frow % 3
    q_feat = jnp.where(frow < 6, jnp.where(part == 0, hi, jnp.where(part == 1, mid, lo)),
                       0.0).astype(jnp.bfloat16)
    q_chain = [jnp.concatenate(
        [q_maps[c // per_map][:, (c % per_map) * cw:(c % per_map + 1) * cw], q_feat], axis=0)
        for c in range(n_chains)]

    m_ref[...] = jnp.full_like(m_ref, -jnp.inf)
    l_ref[...] = jnp.zeros_like(l_ref)
    acc_ref[...] = jnp.zeros_like(acc_ref)

    def chain_mode(c, d):
        if d is None:
            return "full"
        q_lo = (c % per_map) * cw
        if q_lo + cw <= d * tk:
            return "skip"
        if q_lo >= (d + 1) * tk:
            return "full"
        return q_lo - d * tk

    def visible_keys(c, d):
        return cw if chain_mode(c, d) == 0 else tk

    def scores_to(slot, j, d=None):
        kb = k_ref[0, pl.ds(pl.multiple_of(j * tk, tk), tk), :]
        kb = jnp.concatenate([kb, kfeat_ref[...]], axis=1)
        for c in range(n_chains):
            if chain_mode(c, d) != "skip":
                nk = visible_keys(c, d)
                s_ref[slot, c, :nk] = jnp.dot(kb[:nk], q_chain[c],
                                              preferred_element_type=jnp.float32)

    def consume(slot, j, d=None):
        chains = [c for c in range(n_chains) if chain_mode(c, d) != "skip"]
        vtb = vt_ref[:, pl.ds(pl.multiple_of(j * tk, tk), tk)]
        vtb = jnp.concatenate([vtb, jnp.ones((SUM_ROWS, tk), vtb.dtype)], axis=0)
        shift = -slope * (i * tq - j * tk).astype(jnp.float32)

        def column_max(t, st, nxt):
            c = chains[t]
            nk = visible_keys(c, d)
            s = s_ref[slot, c, :nk]
            mode = chain_mode(c, d)
            if mode != "full":
                s = s + mask_ref[:nk, mode:mode + cw]
                s_ref[slot, c, :nk] = s
            st["cmax"] = jnp.max(s, axis=0, keepdims=True) + shift

        def softmax_pv(t, st, nxt):
            c = chains[t]
            nk = visible_keys(c, d)
            lanes = slice(c * cw, (c + 1) * cw)
            m_prev = m_ref[:, lanes]
            m_new = jnp.maximum(m_prev, st.pop("cmax"))
            st["alpha"] = jnp.exp2(m_prev - m_new)
            p = jnp.exp2(s_ref[slot, c, :nk] - (m_new - shift))
            m_ref[:, lanes] = m_new
            st["pv"] = jnp.dot(vtb[:, :nk], p.astype(jnp.bfloat16),
                               preferred_element_type=jnp.float32)

        def accumulate(t, st, nxt):
            c = chains[t]
            lanes = slice(c * cw, (c + 1) * cw)
            alpha, pv = st.pop("alpha"), st.pop("pv")
            acc_ref[:, lanes] = alpha * acc_ref[:, lanes] + pv[:HEAD_W]
            l_ref[:, lanes] = alpha * l_ref[:, lanes] + pv[HEAD_W:HEAD_W + 1]

        _emit_pipelined((column_max, softmax_pv, accumulate), len(chains))

    def step(slot, j):
        scores_to(1 - slot, j + 1)
        consume(slot, j)

    scores_to(0, 0)

    def pair(jj, carry):
        step(0, 2 * jj)
        step(1, 2 * jj + 1)
        return carry

    first_diag = n_diag * i
    lax.fori_loop(0, first_diag // 2, pair, 0)
    for d in range(n_diag):
        if d + 1 < n_diag:
            scores_to((d + 1) % 2, first_diag + d + 1, d + 1)
        consume(d % 2, first_diag + d, d)

    o = acc_ref[...] / l_ref[...]
    a = o[:, :tq] - lam * o[:, tq:]
    ms = jnp.mean(a * a, axis=0, keepdims=True)
    a = a * lax.rsqrt(ms + NORM_EPS) * subln_ref[...] * (1.0 - LAMBDA_INIT)
    o_ref[0] = a.T.astype(o_ref.dtype)


def _sb_attn_kernel(q_ref, k_ref, vt_ref, o_ref, c_ref, acc_ref):
    tq, cw = SB_TQ, CHAIN_W
    n_chains = tq // cw
    i = pl.program_id(2)
    qt = q_ref[...]
    q_chain = [qt[:, c * cw:(c + 1) * cw] for c in range(n_chains)]

    krow = lax.broadcasted_iota(jnp.int32, (cw, cw), 0)
    qcol = lax.broadcasted_iota(jnp.int32, (cw, cw), 1)
    strict = krow < qcol
    lrow = lax.broadcasted_iota(jnp.int32, (cw + SUM_ROWS, cw), 0)
    lcol = lax.broadcasted_iota(jnp.int32, (cw + SUM_ROWS, cw), 1)
    later = jnp.where(jnp.logical_or(lcol > lrow, lrow >= cw), 1.0, 0.0).astype(jnp.bfloat16)

    c_ref[...] = jnp.zeros_like(c_ref)
    acc_ref[...] = jnp.zeros_like(acc_ref)

    def run_pieces(pieces):
        def scores(t, st, nxt):
            sub, c, _ = pieces[t]
            start = pl.multiple_of(sub * cw, cw)
            st["z"] = jnp.dot(k_ref[0, pl.ds(start, cw), :], q_chain[c],
                              preferred_element_type=jnp.float32)

        def suffix(t, st, nxt):
            _, _, triangular = pieces[t]
            z = st.pop("z")
            u = jnp.maximum(z, 0.0) + jnp.log2(1.0 + jnp.exp2(-jnp.abs(z)))
            st["log_sig"] = z - u
            if triangular:
                u = jnp.where(strict, u, 0.0)
            st["tail"] = jnp.dot(later, u.astype(jnp.bfloat16),
                                 preferred_element_type=jnp.float32)

        def weights_pv(t, st, nxt):
            sub, _, triangular = pieces[t]
            start = pl.multiple_of(sub * cw, cw)
            tail = st.pop("tail")
            st["usum"] = tail[cw:cw + 1]
            a = jnp.exp2(st.pop("log_sig") - tail[:cw])
            if triangular:
                a = jnp.where(strict, a, 0.0)
            st["pv"] = jnp.dot(vt_ref[:, pl.ds(start, cw)], a.astype(jnp.bfloat16),
                               preferred_element_type=jnp.float32)

        def accumulate(t, st, nxt):
            _, c, _ = pieces[t]
            lanes = slice(c * cw, (c + 1) * cw)
            carry = c_ref[:, lanes]
            acc_ref[:, lanes] += st.pop("pv") * jnp.exp2(-carry)
            c_ref[:, lanes] = carry + st.pop("usum")

        _emit_pipelined((scores, suffix, weights_pv, accumulate), len(pieces))

    diag = [n_chains * i + c for c in range(n_chains)]
    head = [(diag[c], c, True) for c in reversed(range(n_chains))]
    second = [(diag[c] - 1, c, False) for c in reversed(range(n_chains))]

    @pl.when(i == 0)
    def _():
        run_pieces(head + [p for p in second if p[1] > 0])

    @pl.when(i > 0)
    def _():
        run_pieces(head + second)

    def unfinished(c, depth):
        lanes = slice(c * cw, (c + 1) * cw)
        return jnp.logical_and(diag[c] - depth >= 0,
                               jnp.min(c_ref[:, lanes]) < SB_DONE_LOG2)

    def any_unfinished(depth):
        go = unfinished(0, depth)
        for c in range(1, n_chains):
            go = jnp.logical_or(go, unfinished(c, depth))
        return go

    def body(carry):
        depth, _ = carry
        for c in range(n_chains):
            @pl.when(unfinished(c, depth))
            def _():
                run_pieces([(diag[c] - depth, c, False)])
        return depth + 1, any_unfinished(depth + 1)

    lax.while_loop(lambda carry: carry[1], body, (jnp.int32(2), any_unfinished(2)))
    o_ref[0] = acc_ref[...].T.astype(o_ref.dtype)


def _mix_ffn2_kernel(h_ref, a_ref, b_ref, gate_ref, wa_hbm, wb_hbm, wout_hbm,
                     norm2_ref, wg_hbm, wu_hbm, wd_hbm, normf_ref, o_ref,
                     wa_ref, wb_ref, wout_ref, wg_ref, wu_ref, wd_ref,
                     stage_wide, stage_tall, sem):
    @pl.when(pl.program_id(0) == 0)
    def _():
        for src, dst in ((wa_hbm, wa_ref), (wb_hbm, wb_ref), (wout_hbm, wout_ref)):
            _stage_weight(src, dst, stage_tall, sem, rows=STAGE_SQUARE_ROWS)
        _stage_ffn_weights(wg_hbm, wu_hbm, wd_hbm, wg_ref, wu_ref, wd_ref,
                           stage_wide, stage_tall, sem)

    half = h_ref.shape[0] // 2
    rows = [slice(0, half), slice(half, 2 * half)]
    yab = [(jnp.dot(a_ref[r, :], wa_ref[...], preferred_element_type=jnp.float32),
            jnp.dot(b_ref[r, :], wb_ref[...], preferred_element_type=jnp.float32)) for r in rows]
    h2 = []
    for r, (ya, yb) in zip(rows, yab):
        gate = gate_ref[r, :].astype(jnp.float32)
        y = (gate[:, :D_MODEL] * ya + gate[:, D_MODEL:] * yb).astype(jnp.bfloat16)
        h2.append(h_ref[r, :] + jnp.dot(y, wout_ref[...], preferred_element_type=jnp.float32))
    h2 = jnp.concatenate(h2, axis=0)
    h3 = _swiglu_half_step(h2, norm2_ref[...], wg_ref, wu_ref, wd_ref)
    o_ref[...] = _rms(h3, normf_ref[...])


def _const_spec(shape):
    return pl.BlockSpec(shape, lambda *_: (0,) * len(shape), pipeline_mode=pl.Buffered(1))


ATTN_FLAGS = None


def _tc_params(n_axes, flags=None):
    return pltpu.CompilerParams(dimension_semantics=("arbitrary",) * n_axes,
                                vmem_limit_bytes=VMEM_LIMIT_BYTES, flags=flags)


def kernel(x, ffn1_norm, ffn1_w_gate, ffn1_w_up, ffn1_w_down, mix_norm, w_in, b_gate, lambda_q1, lambda_k1, lambda_q2, lambda_k2, diff_subln, w_branch_diff, w_branch_sb, w_out, ffn2_norm, ffn2_w_gate, ffn2_w_up, ffn2_w_down, final_norm):
    B, S, D = x.shape
    T = B * S
    f32, bf16 = jnp.float32, jnp.bfloat16
    xt = x.reshape(T, D)
    row = lambda v: v.reshape(1, -1).astype(f32)

    tok_spec = pl.BlockSpec((FFN_TM, D), lambda t: (t, 0))
    hbm_spec = pl.BlockSpec(memory_space=pl.ANY)
    ffn_weight_scratch = [pltpu.VMEM((D, D_FF), bf16), pltpu.VMEM((D, D_FF), bf16),
                          pltpu.VMEM((D_FF, D), bf16)]
    ffn_stage_scratch = [pltpu.VMEM((STAGE_SLOTS, STAGE_WIDE_ROWS, D_FF), f32),
                         pltpu.VMEM((STAGE_SLOTS, STAGE_TALL_ROWS, D), f32),
                         pltpu.SemaphoreType.DMA((STAGE_SLOTS,))]
    h1 = pl.pallas_call(
        _ffn1_kernel,
        grid=(T // FFN_TM,),
        in_specs=[tok_spec, _const_spec((1, D)), hbm_spec, hbm_spec, hbm_spec],
        out_specs=tok_spec,
        out_shape=jax.ShapeDtypeStruct((T, D), f32),
        scratch_shapes=ffn_weight_scratch + ffn_stage_scratch,
        compiler_params=_tc_params(1),
        name="ffn1",
    )(xt, row(ffn1_norm[0]), ffn1_w_gate[0], ffn1_w_up[0], ffn1_w_down[0])

    rowscale = jnp.ones((QVT_W,), f32)
    rowscale = rowscale.at[0:ATT_W].set(DA_QK_DIM ** -0.5 * LOG2E)
    rowscale = rowscale.at[ATT_W:2 * ATT_W].set(HEAD_W ** -0.5 * LOG2E)
    k, qvt, gates = pl.pallas_call(
        _in_proj_kernel,
        grid=(T // PROJ_TM,),
        in_specs=[pl.BlockSpec((PROJ_TM, D), lambda t: (t, 0)), _const_spec((1, D)), hbm_spec,
                  _const_spec((QVT_W, 1)), _const_spec((1, GATE_W))],
        out_specs=[pl.BlockSpec((PROJ_TM, K_W), lambda t: (t, 0)),
                   pl.BlockSpec((QVT_W, PROJ_TM), lambda t: (0, t)),
                   pl.BlockSpec((PROJ_TM, GATE_W), lambda t: (t, 0))],
        out_shape=[jax.ShapeDtypeStruct((T, K_W), bf16),
                   jax.ShapeDtypeStruct((QVT_W, T), bf16),
                   jax.ShapeDtypeStruct((T, GATE_W), bf16)],
        scratch_shapes=[pltpu.VMEM((D, K_W), bf16), pltpu.VMEM((QVT_W, D), bf16),
                        pltpu.VMEM((D, GATE_W), bf16),
                        pltpu.VMEM((STAGE_SLOTS, STAGE_IN_ROWS, GATE_W), f32),
                        pltpu.SemaphoreType.DMA((STAGE_SLOTS,))],
        compiler_params=_tc_params(1),
        name="in_proj",
    )(h1, row(mix_norm[0]), w_in[0], rowscale.reshape(-1, 1), row(b_gate[0]))
    k3 = k.reshape(B, S, K_W)

    def q_spec(slab, tq):
        return pl.BlockSpec((HEAD_W, tq),
                            lambda b, h, i: (slab * N_HEADS + h, b * (S // tq) + i))

    def k_spec(slab):
        return pl.BlockSpec((1, S, HEAD_W), lambda b, h, i: (b, 0, slab * N_HEADS + h))

    def vt_spec(slab):
        return pl.BlockSpec((HEAD_W, S), lambda b, h, i: (slab * N_HEADS + h, b))

    def att_out_spec(tq):
        return pl.BlockSpec((1, tq, HEAD_W), lambda b, h, i: (b, i, h))

    att_out_shape = jax.ShapeDtypeStruct((B, S, ATT_W), bf16)
    smem_spec = pl.BlockSpec(memory_space=pltpu.SMEM)

    lam = (jnp.exp(jnp.sum(lambda_q1[0].astype(f32) * lambda_k1[0].astype(f32)))
           - jnp.exp(jnp.sum(lambda_q2[0].astype(f32) * lambda_k2[0].astype(f32)))
           + LAMBDA_INIT).reshape(1)
    slopes = jnp.exp2(-8.0 * jnp.arange(1, N_HEADS + 1, dtype=f32) / N_HEADS)

    a = pl.pallas_call(
        _diff_attn_kernel,
        grid=(B, N_HEADS, S // ATT_TQ),
        in_specs=[smem_spec, smem_spec, q_spec(0, ATT_TQ), k_spec(0), vt_spec(2),
                  _const_spec((HEAD_W, 1))],
        out_specs=att_out_spec(ATT_TQ),
        out_shape=att_out_shape,
        scratch_shapes=[pltpu.VMEM((ATT_TK, ATT_TK), f32), pltpu.VMEM((ATT_TK, HEAD_W), bf16),
                        pltpu.VMEM((2, 2 * ATT_TQ // CHAIN_W, ATT_TK, CHAIN_W), f32),
                        pltpu.VMEM((1, 2 * ATT_TQ), f32), pltpu.VMEM((1, 2 * ATT_TQ), f32),
                        pltpu.VMEM((HEAD_W, 2 * ATT_TQ), f32)],
        compiler_params=_tc_params(3, ATTN_FLAGS),
        name="diff_attn",
    )(slopes, lam, qvt, k3, qvt, diff_subln[0].reshape(-1, 1).astype(f32))

    b = pl.pallas_call(
        _sb_attn_kernel,
        grid=(B, N_HEADS, S // SB_TQ),
        in_specs=[q_spec(1, SB_TQ), k_spec(1), vt_spec(3)],
        out_specs=att_out_spec(SB_TQ),
        out_shape=att_out_shape,
        scratch_shapes=[pltpu.VMEM((1, SB_TQ), f32), pltpu.VMEM((HEAD_W, SB_TQ), f32)],
        compiler_params=_tc_params(3, ATTN_FLAGS),
        name="sb_attn",
    )(qvt, k3, qvt)

    out = pl.pallas_call(
        _mix_ffn2_kernel,
        grid=(T // FFN_TM,),
        in_specs=[tok_spec,
                  pl.BlockSpec((FFN_TM, ATT_W), lambda t: (t, 0)),
                  pl.BlockSpec((FFN_TM, ATT_W), lambda t: (t, 0)),
                  pl.BlockSpec((FFN_TM, GATE_W), lambda t: (t, 0)),
                  hbm_spec, hbm_spec, hbm_spec, _const_spec((1, D)),
                  hbm_spec, hbm_spec, hbm_spec, _const_spec((1, D))],
        out_specs=tok_spec,
        out_shape=jax.ShapeDtypeStruct((T, D), f32),
        scratch_shapes=[pltpu.VMEM((ATT_W, D), bf16), pltpu.VMEM((ATT_W, D), bf16),
                        pltpu.VMEM((D, D), bf16)] + ffn_weight_scratch + ffn_stage_scratch,
        compiler_params=_tc_params(1),
        name="mix_ffn2",
    )(h1, a.reshape(T, -1), b.reshape(T, -1), gates,
      w_branch_diff[0], w_branch_sb[0], w_out[0],
      row(ffn2_norm[0]), ffn2_w_gate[0], ffn2_w_up[0], ffn2_w_down[0], row(final_norm))
    return out.reshape(B, S, D)
```

```python
import math

import jax
import jax.numpy as jnp
from jax import lax
from jax.experimental import pallas as pl
from jax.experimental.pallas import tpu as pltpu

D_MODEL = 1024
D_FF = 2816
N_HEADS = 4
HEAD_W = 128
DA_QK_DIM = 64
ATT_W = N_HEADS * HEAD_W
K_W = 2 * ATT_W
QVT_W = 4 * ATT_W
GATE_W = 2 * D_MODEL
NORM_EPS = 1e-5
LAMBDA_INIT = 0.8 - 0.6 * math.exp(-0.3 * 0)
LOG2E = 1.0 / math.log(2.0)

VMEM_LIMIT_BYTES = 56 * 1024 * 1024

STAGE_SLOTS = 4
STAGE_WIDE_ROWS = 64
STAGE_TALL_ROWS = 176
STAGE_SQUARE_ROWS = 128
STAGE_IN_ROWS = 128

FFN_TM = 512
PROJ_TM = 512
ATT_TQ = 2048
ATT_TK = 512
SB_TQ = 4096
CHAIN_W = 256
SB_DONE_LOG2 = 160.0
SUM_ROWS = 16
SLOPE_TERMS = 3
BF16_EXACT_INT = 256

_NT = (((1,), (1,)), ((), ()))


def _rms(x, g):
    ms = jnp.mean(x * x, axis=-1, keepdims=True)
    return x * lax.rsqrt(ms + NORM_EPS) * g


def _swiglu_half_step(x, norm_g, wg_ref, wu_ref, wd_ref):
    halves = jnp.split(x, 2, axis=0)
    xn = [_rms(h, norm_g).astype(jnp.bfloat16) for h in halves]
    gu = [(jnp.dot(n, wg_ref[...], preferred_element_type=jnp.float32),
           jnp.dot(n, wu_ref[...], preferred_element_type=jnp.float32)) for n in xn]
    out = []
    for h, (g, u) in zip(halves, gu):
        hact = (g * jax.nn.sigmoid(g) * u).astype(jnp.bfloat16)
        out.append(h + 0.5 * jnp.dot(hact, wd_ref[...], preferred_element_type=jnp.float32))
    return jnp.concatenate(out, axis=0)


def _stage_weight(src_hbm, dst_ref, stage_ref, sem_ref, *, rows, col0=0, ncols=None,
                  dst_row0=0, dst_col0=0, transpose=False):
    n_rows = src_hbm.shape[0]
    ncols = src_hbm.shape[1] - col0 if ncols is None else ncols
    n_slots = stage_ref.shape[0]
    assert n_rows % rows == 0 and rows <= stage_ref.shape[1] and ncols <= stage_ref.shape[2]
    n_slabs = n_rows // rows

    def slab_copy(c):
        return pltpu.make_async_copy(
            src_hbm.at[pl.ds(c * rows, rows), pl.ds(col0, ncols)],
            stage_ref.at[c % n_slots, pl.ds(0, rows), pl.ds(0, ncols)],
            sem_ref.at[c % n_slots])

    for c in range(min(n_slots - 1, n_slabs)):
        slab_copy(c).start()
    for c in range(n_slabs):
        if c + n_slots - 1 < n_slabs:
            slab_copy(c + n_slots - 1).start()
        slab_copy(c).wait()
        slab = stage_ref[c % n_slots, :rows, :ncols]
        if transpose:
            dst_ref[dst_row0:dst_row0 + ncols,
                    dst_col0 + c * rows:dst_col0 + (c + 1) * rows] = slab.T.astype(dst_ref.dtype)
        else:
            dst_ref[dst_row0 + c * rows:dst_row0 + (c + 1) * rows,
                    dst_col0:dst_col0 + ncols] = slab.astype(dst_ref.dtype)


def _stage_ffn_weights(wg_hbm, wu_hbm, wd_hbm, wg_ref, wu_ref, wd_ref, stage_wide, stage_tall, sem):
    _stage_weight(wg_hbm, wg_ref, stage_wide, sem, rows=STAGE_WIDE_ROWS)
    _stage_weight(wu_hbm, wu_ref, stage_wide, sem, rows=STAGE_WIDE_ROWS)
    _stage_weight(wd_hbm, wd_ref, stage_tall, sem, rows=STAGE_TALL_ROWS)


def _ffn1_kernel(x_ref, norm_ref, wg_hbm, wu_hbm, wd_hbm, o_ref,
                 wg_ref, wu_ref, wd_ref, stage_wide, stage_tall, sem):
    @pl.when(pl.program_id(0) == 0)
    def _():
        _stage_ffn_weights(wg_hbm, wu_hbm, wd_hbm, wg_ref, wu_ref, wd_ref,
                           stage_wide, stage_tall, sem)

    o_ref[...] = _swiglu_half_step(x_ref[...], norm_ref[...], wg_ref, wu_ref, wd_ref)


def _in_proj_kernel(h_ref, norm_ref, win_hbm, rowscale_ref, bgate_ref,
                    k_ref, qvt_ref, gate_ref, wk_ref, wqvt_ref, wgate_ref, stage, sem):
    @pl.when(pl.program_id(0) == 0)
    def _():
        w = ATT_W
        for piece, src_block in enumerate((1, 4)):
            _stage_weight(win_hbm, wk_ref, stage, sem, rows=STAGE_IN_ROWS,
                          col0=src_block * w, ncols=w, dst_col0=piece * w)
        _stage_weight(win_hbm, wgate_ref, stage, sem, rows=STAGE_IN_ROWS, col0=6 * w, ncols=GATE_W)
        for piece, src_block in enumerate((0, 3, 2, 5)):
            _stage_weight(win_hbm, wqvt_ref, stage, sem, rows=STAGE_IN_ROWS,
                          col0=src_block * w, ncols=w, dst_row0=piece * w, transpose=True)

    half = h_ref.shape[0] // 2
    rows = [slice(0, half), slice(half, 2 * half)]
    n = [_rms(h_ref[r, :], norm_ref[...]).astype(jnp.bfloat16) for r in rows]
    g = [jnp.dot(nh, wgate_ref[...], preferred_element_type=jnp.float32) for nh in n]
    for r, gh in zip(rows, g):
        gate_ref[r, :] = jax.nn.sigmoid(gh + bgate_ref[...]).astype(jnp.bfloat16)
    for r, nh in zip(rows, n):
        k_ref[r, :] = jnp.dot(nh, wk_ref[...],
                              preferred_element_type=jnp.float32).astype(jnp.bfloat16)
    for r, nh in zip(rows, n):
        qvt = lax.dot_general(wqvt_ref[...], nh, _NT, preferred_element_type=jnp.float32)
        qvt_ref[:, r] = (qvt * rowscale_ref[...]).astype(jnp.bfloat16)


def _emit_pipelined(stages, n):
    state = [dict() for _ in range(n)]
    for step in range(n + len(stages) - 1):
        for s, stage in enumerate(stages):
            t = step - s
            if 0 <= t < n:
                stage(t, state[t])


def _diff_attn_kernel(slope_ref, lam_ref, q_ref, k_ref, vt_ref, subln_ref, o_ref,
                      mask_ref, kfeat_ref, s_ref, m_ref, l_ref, acc_ref):
    tq, tk, cw = ATT_TQ, ATT_TK, CHAIN_W
    per_map = tq // cw
    n_chains = 2 * per_map
    n_diag = tq // tk
    assert n_diag % 2 == 0 and tk == 2 * cw and tk <= 2 * BF16_EXACT_INT
    h = pl.program_id(1)
    i = pl.program_id(2)
    slope = slope_ref[h] * LOG2E
    lam = lam_ref[0]

    @pl.when(jnp.logical_and(pl.program_id(0) == 0, jnp.logical_and(h == 0, i == 0)))
    def _():
        krow = lax.broadcasted_iota(jnp.int32, (tk, tk), 0)
        qcol = lax.broadcasted_iota(jnp.int32, (tk, tk), 1)
        mask_ref[...] = jnp.where(qcol >= krow, 0.0, -jnp.inf)
        kpos = lax.broadcasted_iota(jnp.int32, (tk, HEAD_W), 0)
        klane = lax.broadcasted_iota(jnp.int32, (tk, HEAD_W), 1)
        k_hi = jnp.where(kpos >= BF16_EXACT_INT, BF16_EXACT_INT, 0)
        kfeat_ref[...] = jnp.where(klane < SLOPE_TERMS, k_hi,
                                   jnp.where(klane < 2 * SLOPE_TERMS, kpos - k_hi, 0)
                                   ).astype(jnp.float32).astype(jnp.bfloat16)

    qt = q_ref[...]
    chan = lax.broadcasted_iota(jnp.int32, (HEAD_W, tq), 0)
    zero = jnp.zeros_like(qt)
    q_maps = (jnp.where(chan < DA_QK_DIM, qt, zero), jnp.where(chan >= DA_QK_DIM, qt, zero))

    sl = jnp.full((HEAD_W, cw), slope, jnp.float32)
    hi = sl.astype(jnp.bfloat16).astype(jnp.float32)
    mid = (sl - hi).astype(jnp.bfloat16).astype(jnp.float32)
    lo = sl - hi - mid
    frow = lax.broadcasted_iota(jnp.int32, (HEAD_W, cw), 0)
    part = frow % SLOPE_TERMS
    q_feat = jnp.where(frow < 2 * SLOPE_TERMS,
                       jnp.where(part == 0, hi, jnp.where(part == 1, mid, lo)),
                       0.0).astype(jnp.bfloat16)
    q_chain = [jnp.concatenate(
        [q_maps[c // per_map][:, (c % per_map) * cw:(c % per_map + 1) * cw], q_feat], axis=0)
        for c in range(n_chains)]

    m_ref[...] = jnp.full_like(m_ref, -jnp.inf)
    l_ref[...] = jnp.zeros_like(l_ref)
    acc_ref[...] = jnp.zeros_like(acc_ref)

    def chain_mode(c, d):
        if d is None:
            return "full"
        q_lo = (c % per_map) * cw
        if q_lo + cw <= d * tk:
            return "skip"
        if q_lo >= (d + 1) * tk:
            return "full"
        return q_lo - d * tk

    def visible_keys(c, d):
        return cw if chain_mode(c, d) == 0 else tk

    def scores_to(slot, j, d=None):
        kb = k_ref[0, pl.ds(pl.multiple_of(j * tk, tk), tk), :]
        kb = jnp.concatenate([kb, kfeat_ref[...]], axis=1)
        for c in range(n_chains):
            if chain_mode(c, d) != "skip":
                nk = visible_keys(c, d)
                s_ref[slot, c, :nk] = jnp.dot(kb[:nk], q_chain[c],
                                              preferred_element_type=jnp.float32)

    def consume(slot, j, d=None):
        chains = [c for c in range(n_chains) if chain_mode(c, d) != "skip"]
        vtb = vt_ref[:, pl.ds(pl.multiple_of(j * tk, tk), tk)]
        vtb = jnp.concatenate([vtb, jnp.ones((SUM_ROWS, tk), vtb.dtype)], axis=0)
        shift = -slope * (i * tq - j * tk).astype(jnp.float32)

        def column_max(t, st):
            c = chains[t]
            nk = visible_keys(c, d)
            s = s_ref[slot, c, :nk]
            mode = chain_mode(c, d)
            if mode != "full":
                s = s + mask_ref[:nk, mode:mode + cw]
                s_ref[slot, c, :nk] = s
            st["cmax"] = jnp.max(s, axis=0, keepdims=True) + shift

        def softmax_pv(t, st):
            c = chains[t]
            nk = visible_keys(c, d)
            lanes = slice(c * cw, (c + 1) * cw)
            m_prev = m_ref[:, lanes]
            m_new = jnp.maximum(m_prev, st.pop("cmax"))
            st["alpha"] = jnp.exp2(m_prev - m_new)
            p = jnp.exp2(s_ref[slot, c, :nk] - (m_new - shift))
            m_ref[:, lanes] = m_new
            st["pv"] = jnp.dot(vtb[:, :nk], p.astype(jnp.bfloat16),
                               preferred_element_type=jnp.float32)

        def accumulate(t, st):
            c = chains[t]
            lanes = slice(c * cw, (c + 1) * cw)
            alpha, pv = st.pop("alpha"), st.pop("pv")
            acc_ref[:, lanes] = alpha * acc_ref[:, lanes] + pv[:HEAD_W]
            l_ref[:, lanes] = alpha * l_ref[:, lanes] + pv[HEAD_W:HEAD_W + 1]

        _emit_pipelined((column_max, softmax_pv, accumulate), len(chains))

    def step(slot, j):
        scores_to(1 - slot, j + 1)
        consume(slot, j)

    scores_to(0, 0)

    def pair(jj, carry):
        step(0, 2 * jj)
        step(1, 2 * jj + 1)
        return carry

    first_diag = n_diag * i
    lax.fori_loop(0, first_diag // 2, pair, 0)
    for d in range(n_diag):
        if d + 1 < n_diag:
            scores_to((d + 1) % 2, first_diag + d + 1, d + 1)
        consume(d % 2, first_diag + d, d)

    o = acc_ref[...] / l_ref[...]
    a = o[:, :tq] - lam * o[:, tq:]
    ms = jnp.mean(a * a, axis=0, keepdims=True)
    a = a * lax.rsqrt(ms + NORM_EPS) * subln_ref[...] * (1.0 - LAMBDA_INIT)
    o_ref[0] = a.T.astype(o_ref.dtype)


def _sb_attn_kernel(q_ref, k_ref, vt_ref, o_ref, c_ref, acc_ref):
    tq, cw = SB_TQ, CHAIN_W
    n_chains = tq // cw
    i = pl.program_id(2)
    qt = q_ref[...]
    q_chain = [qt[:, c * cw:(c + 1) * cw] for c in range(n_chains)]

    krow = lax.broadcasted_iota(jnp.int32, (cw, cw), 0)
    qcol = lax.broadcasted_iota(jnp.int32, (cw, cw), 1)
    strict = krow < qcol
    lrow = lax.broadcasted_iota(jnp.int32, (cw + SUM_ROWS, cw), 0)
    lcol = lax.broadcasted_iota(jnp.int32, (cw + SUM_ROWS, cw), 1)
    later = jnp.where(jnp.logical_or(lcol > lrow, lrow >= cw), 1.0, 0.0).astype(jnp.bfloat16)

    c_ref[...] = jnp.zeros_like(c_ref)
    acc_ref[...] = jnp.zeros_like(acc_ref)

    def run_pieces(pieces):
        def scores(t, st):
            sub, c, _ = pieces[t]
            start = pl.multiple_of(sub * cw, cw)
            st["z"] = jnp.dot(k_ref[0, pl.ds(start, cw), :], q_chain[c],
                              preferred_element_type=jnp.float32)

        def suffix(t, st):
            _, _, triangular = pieces[t]
            z = st.pop("z")
            u = jnp.maximum(z, 0.0) + jnp.log2(1.0 + jnp.exp2(-jnp.abs(z)))
            st["log_sig"] = z - u
            if triangular:
                u = jnp.where(strict, u, 0.0)
            st["tail"] = jnp.dot(later, u.astype(jnp.bfloat16),
                                 preferred_element_type=jnp.float32)

        def weights_pv(t, st):
            sub, _, triangular = pieces[t]
            start = pl.multiple_of(sub * cw, cw)
            tail = st.pop("tail")
            st["usum"] = tail[cw:cw + 1]
            a = jnp.exp2(st.pop("log_sig") - tail[:cw])
            if triangular:
                a = jnp.where(strict, a, 0.0)
            st["pv"] = jnp.dot(vt_ref[:, pl.ds(start, cw)], a.astype(jnp.bfloat16),
                               preferred_element_type=jnp.float32)

        def accumulate(t, st):
            _, c, _ = pieces[t]
            lanes = slice(c * cw, (c + 1) * cw)
            carry = c_ref[:, lanes]
            acc_ref[:, lanes] += st.pop("pv") * jnp.exp2(-carry)
            c_ref[:, lanes] = carry + st.pop("usum")

        _emit_pipelined((scores, suffix, weights_pv, accumulate), len(pieces))

    diag = [n_chains * i + c for c in range(n_chains)]
    head = [(diag[c], c, True) for c in reversed(range(n_chains))]
    second = [(diag[c] - 1, c, False) for c in reversed(range(n_chains))]

    @pl.when(i == 0)
    def _():
        run_pieces(head + [p for p in second if p[1] > 0])

    @pl.when(i > 0)
    def _():
        run_pieces(head + second)

    def unfinished(c, depth):
        lanes = slice(c * cw, (c + 1) * cw)
        return jnp.logical_and(diag[c] - depth >= 0,
                               jnp.min(c_ref[:, lanes]) < SB_DONE_LOG2)

    def any_unfinished(depth):
        go = unfinished(0, depth)
        for c in range(1, n_chains):
            go = jnp.logical_or(go, unfinished(c, depth))
        return go

    def body(carry):
        depth, _ = carry
        for c in range(n_chains):
            @pl.when(unfinished(c, depth))
            def _():
                run_pieces([(diag[c] - depth, c, False)])
        return depth + 1, any_unfinished(depth + 1)

    lax.while_loop(lambda carry: carry[1], body, (jnp.int32(2), any_unfinished(2)))
    o_ref[0] = acc_ref[...].T.astype(o_ref.dtype)


def _mix_ffn2_kernel(h_ref, a_ref, b_ref, gate_ref, wa_hbm, wb_hbm, wout_hbm,
                     norm2_ref, wg_hbm, wu_hbm, wd_hbm, normf_ref, o_ref,
                     wa_ref, wb_ref, wout_ref, wg_ref, wu_ref, wd_ref,
                     stage_wide, stage_tall, sem):
    @pl.when(pl.program_id(0) == 0)
    def _():
        for src, dst in ((wa_hbm, wa_ref), (wb_hbm, wb_ref), (wout_hbm, wout_ref)):
            _stage_weight(src, dst, stage_tall, sem, rows=STAGE_SQUARE_ROWS)
        _stage_ffn_weights(wg_hbm, wu_hbm, wd_hbm, wg_ref, wu_ref, wd_ref,
                           stage_wide, stage_tall, sem)

    half = h_ref.shape[0] // 2
    rows = [slice(0, half), slice(half, 2 * half)]
    yab = [(jnp.dot(a_ref[r, :], wa_ref[...], preferred_element_type=jnp.float32),
            jnp.dot(b_ref[r, :], wb_ref[...], preferred_element_type=jnp.float32)) for r in rows]
    h2 = []
    for r, (ya, yb) in zip(rows, yab):
        gate = gate_ref[r, :].astype(jnp.float32)
        y = (gate[:, :D_MODEL] * ya + gate[:, D_MODEL:] * yb).astype(jnp.bfloat16)
        h2.append(h_ref[r, :] + jnp.dot(y, wout_ref[...], preferred_element_type=jnp.float32))
    h2 = jnp.concatenate(h2, axis=0)
    h3 = _swiglu_half_step(h2, norm2_ref[...], wg_ref, wu_ref, wd_ref)
    o_ref[...] = _rms(h3, normf_ref[...])


def _const_spec(shape):
    return pl.BlockSpec(shape, lambda *_: (0,) * len(shape), pipeline_mode=pl.Buffered(1))


def _tc_params(n_axes):
    return pltpu.CompilerParams(dimension_semantics=("arbitrary",) * n_axes,
                                vmem_limit_bytes=VMEM_LIMIT_BYTES)


def kernel(x, ffn1_norm, ffn1_w_gate, ffn1_w_up, ffn1_w_down, mix_norm, w_in, b_gate, lambda_q1, lambda_k1, lambda_q2, lambda_k2, diff_subln, w_branch_diff, w_branch_sb, w_out, ffn2_norm, ffn2_w_gate, ffn2_w_up, ffn2_w_down, final_norm):
    B, S, D = x.shape
    T = B * S
    f32, bf16 = jnp.float32, jnp.bfloat16
    xt = x.reshape(T, D)
    row = lambda v: v.reshape(1, -1).astype(f32)

    tok_spec = pl.BlockSpec((FFN_TM, D), lambda t: (t, 0))
    hbm_spec = pl.BlockSpec(memory_space=pl.ANY)
    ffn_weight_scratch = [pltpu.VMEM((D, D_FF), bf16), pltpu.VMEM((D, D_FF), bf16),
                          pltpu.VMEM((D_FF, D), bf16)]
    ffn_stage_scratch = [pltpu.VMEM((STAGE_SLOTS, STAGE_WIDE_ROWS, D_FF), f32),
                         pltpu.VMEM((STAGE_SLOTS, STAGE_TALL_ROWS, D), f32),
                         pltpu.SemaphoreType.DMA((STAGE_SLOTS,))]
    h1 = pl.pallas_call(
        _ffn1_kernel,
        grid=(T // FFN_TM,),
        in_specs=[tok_spec, _const_spec((1, D)), hbm_spec, hbm_spec, hbm_spec],
        out_specs=tok_spec,
        out_shape=jax.ShapeDtypeStruct((T, D), f32),
        scratch_shapes=ffn_weight_scratch + ffn_stage_scratch,
        compiler_params=_tc_params(1),
        name="ffn1",
    )(xt, row(ffn1_norm[0]), ffn1_w_gate[0], ffn1_w_up[0], ffn1_w_down[0])

    rowscale = jnp.ones((QVT_W,), f32)
    rowscale = rowscale.at[0:ATT_W].set(DA_QK_DIM ** -0.5 * LOG2E)
    rowscale = rowscale.at[ATT_W:2 * ATT_W].set(HEAD_W ** -0.5 * LOG2E)
    k, qvt, gates = pl.pallas_call(
        _in_proj_kernel,
        grid=(T // PROJ_TM,),
        in_specs=[pl.BlockSpec((PROJ_TM, D), lambda t: (t, 0)), _const_spec((1, D)), hbm_spec,
                  _const_spec((QVT_W, 1)), _const_spec((1, GATE_W))],
        out_specs=[pl.BlockSpec((PROJ_TM, K_W), lambda t: (t, 0)),
                   pl.BlockSpec((QVT_W, PROJ_TM), lambda t: (0, t)),
                   pl.BlockSpec((PROJ_TM, GATE_W), lambda t: (t, 0))],
        out_shape=[jax.ShapeDtypeStruct((T, K_W), bf16),
                   jax.ShapeDtypeStruct((QVT_W, T), bf16),
                   jax.ShapeDtypeStruct((T, GATE_W), bf16)],
        scratch_shapes=[pltpu.VMEM((D, K_W), bf16), pltpu.VMEM((QVT_W, D), bf16),
                        pltpu.VMEM((D, GATE_W), bf16),
                        pltpu.VMEM((STAGE_SLOTS, STAGE_IN_ROWS, GATE_W), f32),
                        pltpu.SemaphoreType.DMA((STAGE_SLOTS,))],
        compiler_params=_tc_params(1),
        name="in_proj",
    )(h1, row(mix_norm[0]), w_in[0], rowscale.reshape(-1, 1), row(b_gate[0]))
    k3 = k.reshape(B, S, K_W)

    def q_spec(slab, tq):
        return pl.BlockSpec((HEAD_W, tq),
                            lambda b, h, i: (slab * N_HEADS + h, b * (S // tq) + i))

    def k_spec(slab):
        return pl.BlockSpec((1, S, HEAD_W), lambda b, h, i: (b, 0, slab * N_HEADS + h))

    def vt_spec(slab):
        return pl.BlockSpec((HEAD_W, S), lambda b, h, i: (slab * N_HEADS + h, b))

    def att_out_spec(tq):
        return pl.BlockSpec((1, tq, HEAD_W), lambda b, h, i: (b, i, h))

    att_out_shape = jax.ShapeDtypeStruct((B, S, ATT_W), bf16)
    smem_spec = pl.BlockSpec(memory_space=pltpu.SMEM)

    lam = (jnp.exp(jnp.sum(lambda_q1[0].astype(f32) * lambda_k1[0].astype(f32)))
           - jnp.exp(jnp.sum(lambda_q2[0].astype(f32) * lambda_k2[0].astype(f32)))
           + LAMBDA_INIT).reshape(1)
    slopes = jnp.exp2(-8.0 * jnp.arange(1, N_HEADS + 1, dtype=f32) / N_HEADS)

    a = pl.pallas_call(
        _diff_attn_kernel,
        grid=(B, N_HEADS, S // ATT_TQ),
        in_specs=[smem_spec, smem_spec, q_spec(0, ATT_TQ), k_spec(0), vt_spec(2),
                  _const_spec((HEAD_W, 1))],
        out_specs=att_out_spec(ATT_TQ),
        out_shape=att_out_shape,
        scratch_shapes=[pltpu.VMEM((ATT_TK, ATT_TK), f32), pltpu.VMEM((ATT_TK, HEAD_W), bf16),
                        pltpu.VMEM((2, 2 * ATT_TQ // CHAIN_W, ATT_TK, CHAIN_W), f32),
                        pltpu.VMEM((1, 2 * ATT_TQ), f32), pltpu.VMEM((1, 2 * ATT_TQ), f32),
                        pltpu.VMEM((HEAD_W, 2 * ATT_TQ), f32)],
        compiler_params=_tc_params(3),
        name="diff_attn",
    )(slopes, lam, qvt, k3, qvt, diff_subln[0].reshape(-1, 1).astype(f32))

    b = pl.pallas_call(
        _sb_attn_kernel,
        grid=(B, N_HEADS, S // SB_TQ),
        in_specs=[q_spec(1, SB_TQ), k_spec(1), vt_spec(3)],
        out_specs=att_out_spec(SB_TQ),
        out_shape=att_out_shape,
        scratch_shapes=[pltpu.VMEM((1, SB_TQ), f32), pltpu.VMEM((HEAD_W, SB_TQ), f32)],
        compiler_params=_tc_params(3),
        name="sb_attn",
    )(qvt, k3, qvt)

    out = pl.pallas_call(
        _mix_ffn2_kernel,
        grid=(T // FFN_TM,),
        in_specs=[tok_spec,
                  pl.BlockSpec((FFN_TM, ATT_W), lambda t: (t, 0)),
                  pl.BlockSpec((FFN_TM, ATT_W), lambda t: (t, 0)),
                  pl.BlockSpec((FFN_TM, GATE_W), lambda t: (t, 0)),
                  hbm_spec, hbm_spec, hbm_spec, _const_spec((1, D)),
                  hbm_spec, hbm_spec, hbm_spec, _const_spec((1, D))],
        out_specs=tok_spec,
        out_shape=jax.ShapeDtypeStruct((T, D), f32),
        scratch_shapes=[pltpu.VMEM((ATT_W, D), bf16), pltpu.VMEM((ATT_W, D), bf16),
                        pltpu.VMEM((D, D), bf16)] + ffn_weight_scratch + ffn_stage_scratch,
        compiler_params=_tc_params(1),
        name="mix_ffn2",
    )(h1, a.reshape(T, -1), b.reshape(T, -1), gates,
      w_branch_diff[0], w_branch_sb[0], w_out[0],
      row(ffn2_norm[0]), ffn2_w_gate[0], ffn2_w_up[0], ffn2_w_down[0], row(final_norm))
    return out.reshape(B, S, D)
```

```python
import math

import jax
import jax.numpy as jnp
from jax import lax
from jax.experimental import pallas as pl
from jax.experimental.pallas import tpu as pltpu

D_MODEL = 1024
D_FF = 2816
N_HEADS = 4
HEAD_W = 128
DA_QK_DIM = 64
ATT_W = N_HEADS * HEAD_W
K_W = 2 * ATT_W
QVT_W = 4 * ATT_W
GATE_W = 2 * D_MODEL
NORM_EPS = 1e-5
LAMBDA_INIT = 0.8 - 0.6 * math.exp(-0.3 * 0)
LOG2E = 1.0 / math.log(2.0)

VMEM_LIMIT_BYTES = 56 * 1024 * 1024

STAGE_SLOTS = 4
STAGE_WIDE_ROWS = 64
STAGE_TALL_ROWS = 176
STAGE_SQUARE_ROWS = 128
STAGE_IN_ROWS = 128

FFN_TM = 512
PROJ_TM = 512
ATT_TQ = 2048
ATT_TK = 512
SB_TQ = 4096
CHAIN_W = 256
SB_DONE_LOG2 = 160.0
SUM_ROWS = 16
SLOPE_TERMS = 3
BF16_EXACT_INT = 256

_NT = (((1,), (1,)), ((), ()))


def _rms(x, g):
    ms = jnp.mean(x * x, axis=-1, keepdims=True)
    return x * lax.rsqrt(ms + NORM_EPS) * g


def _swiglu_half_step(x, norm_g, wg_ref, wu_ref, wd_ref):
    halves = jnp.split(x, 2, axis=0)
    xn = [_rms(h, norm_g).astype(jnp.bfloat16) for h in halves]
    gu = [(jnp.dot(n, wg_ref[...], preferred_element_type=jnp.float32),
           jnp.dot(n, wu_ref[...], preferred_element_type=jnp.float32)) for n in xn]
    out = []
    for h, (g, u) in zip(halves, gu):
        hact = (g * jax.nn.sigmoid(g) * u).astype(jnp.bfloat16)
        out.append(h + 0.5 * jnp.dot(hact, wd_ref[...], preferred_element_type=jnp.float32))
    return jnp.concatenate(out, axis=0)


def _stage_weight(src_hbm, dst_ref, stage_ref, sem_ref, *, rows, col0=0, ncols=None,
                  dst_row0=0, dst_col0=0, transpose=False):
    n_rows = src_hbm.shape[0]
    ncols = src_hbm.shape[1] - col0 if ncols is None else ncols
    n_slots = stage_ref.shape[0]
    assert n_rows % rows == 0 and rows <= stage_ref.shape[1] and ncols <= stage_ref.shape[2]
    n_slabs = n_rows // rows

    def slab_copy(c):
        return pltpu.make_async_copy(
            src_hbm.at[pl.ds(c * rows, rows), pl.ds(col0, ncols)],
            stage_ref.at[c % n_slots, pl.ds(0, rows), pl.ds(0, ncols)],
            sem_ref.at[c % n_slots])

    for c in range(min(n_slots - 1, n_slabs)):
        slab_copy(c).start(priority=c % 2)
    for c in range(n_slabs):
        if c + n_slots - 1 < n_slabs:
            slab_copy(c + n_slots - 1).start(priority=(c + n_slots - 1) % 2)
        slab_copy(c).wait()
        slab = stage_ref[c % n_slots, :rows, :ncols]
        if transpose:
            dst_ref[dst_row0:dst_row0 + ncols,
                    dst_col0 + c * rows:dst_col0 + (c + 1) * rows] = slab.T.astype(dst_ref.dtype)
        else:
            dst_ref[dst_row0 + c * rows:dst_row0 + (c + 1) * rows,
                    dst_col0:dst_col0 + ncols] = slab.astype(dst_ref.dtype)


def _stage_ffn_weights(wg_hbm, wu_hbm, wd_hbm, wg_ref, wu_ref, wd_ref, stage_wide, stage_tall, sem):
    _stage_weight(wg_hbm, wg_ref, stage_wide, sem, rows=STAGE_WIDE_ROWS)
    _stage_weight(wu_hbm, wu_ref, stage_wide, sem, rows=STAGE_WIDE_ROWS)
    _stage_weight(wd_hbm, wd_ref, stage_tall, sem, rows=STAGE_TALL_ROWS)


def _ffn1_kernel(x_ref, norm_ref, wg_hbm, wu_hbm, wd_hbm, o_ref,
                 wg_ref, wu_ref, wd_ref, stage_wide, stage_tall, sem):
    @pl.when(pl.program_id(0) == 0)
    def _():
        _stage_ffn_weights(wg_hbm, wu_hbm, wd_hbm, wg_ref, wu_ref, wd_ref,
                           stage_wide, stage_tall, sem)

    o_ref[...] = _swiglu_half_step(x_ref[...], norm_ref[...], wg_ref, wu_ref, wd_ref)


def _in_proj_kernel(h_ref, norm_ref, win_hbm, rowscale_ref, bgate_ref,
                    k_ref, qvt_ref, gate_ref, wk_ref, wqvt_ref, wgate_ref, stage, sem):
    @pl.when(pl.program_id(0) == 0)
    def _():
        w = ATT_W
        for piece, src_block in enumerate((1, 4)):
            _stage_weight(win_hbm, wk_ref, stage, sem, rows=STAGE_IN_ROWS,
                          col0=src_block * w, ncols=w, dst_col0=piece * w)
        _stage_weight(win_hbm, wgate_ref, stage, sem, rows=STAGE_IN_ROWS, col0=6 * w, ncols=GATE_W)
        for piece, src_block in enumerate((0, 3, 2, 5)):
            _stage_weight(win_hbm, wqvt_ref, stage, sem, rows=STAGE_IN_ROWS,
                          col0=src_block * w, ncols=w, dst_row0=piece * w, transpose=True)

    half = h_ref.shape[0] // 2
    rows = [slice(0, half), slice(half, 2 * half)]
    n = [_rms(h_ref[r, :], norm_ref[...]).astype(jnp.bfloat16) for r in rows]
    g = [jnp.dot(nh, wgate_ref[...], preferred_element_type=jnp.float32) for nh in n]
    for r, gh in zip(rows, g):
        gate_ref[r, :] = jax.nn.sigmoid(gh + bgate_ref[...]).astype(jnp.bfloat16)
    for r, nh in zip(rows, n):
        k_ref[r, :] = jnp.dot(nh, wk_ref[...],
                              preferred_element_type=jnp.float32).astype(jnp.bfloat16)
    for r, nh in zip(rows, n):
        qvt = lax.dot_general(wqvt_ref[...], nh, _NT, preferred_element_type=jnp.float32)
        qvt_ref[:, r] = (qvt * rowscale_ref[...]).astype(jnp.bfloat16)


def _emit_pipelined(stages, n):
    state = [dict() for _ in range(n)]
    for step in range(n + len(stages) - 1):
        for s, stage in enumerate(stages):
            t = step - s
            if 0 <= t < n:
                stage(t, state[t])


def _diff_attn_kernel(slope_ref, lam_ref, q_ref, k_ref, vt_ref, subln_ref, o_ref,
                      mask_ref, kfeat_ref, s_ref, m_ref, l_ref, acc_ref):
    tq, tk, cw = ATT_TQ, ATT_TK, CHAIN_W
    per_map = tq // cw
    n_chains = 2 * per_map
    n_diag = tq // tk
    assert n_diag % 2 == 0 and tk == 2 * cw and tk <= 2 * BF16_EXACT_INT
    h = pl.program_id(1)
    i = pl.program_id(2)
    slope = slope_ref[h] * LOG2E
    lam = lam_ref[0]

    @pl.when(jnp.logical_and(pl.program_id(0) == 0, jnp.logical_and(h == 0, i == 0)))
    def _():
        krow = lax.broadcasted_iota(jnp.int32, (tk, tk), 0)
        qcol = lax.broadcasted_iota(jnp.int32, (tk, tk), 1)
        mask_ref[...] = jnp.where(qcol >= krow, 0.0, -jnp.inf)
        kpos = lax.broadcasted_iota(jnp.int32, (tk, HEAD_W), 0)
        klane = lax.broadcasted_iota(jnp.int32, (tk, HEAD_W), 1)
        k_hi = jnp.where(kpos >= BF16_EXACT_INT, BF16_EXACT_INT, 0)
        kfeat_ref[...] = jnp.where(klane < SLOPE_TERMS, k_hi,
                                   jnp.where(klane < 2 * SLOPE_TERMS, kpos - k_hi, 0)
                                   ).astype(jnp.float32).astype(jnp.bfloat16)

    qt = q_ref[...]
    chan = lax.broadcasted_iota(jnp.int32, (HEAD_W, tq), 0)
    zero = jnp.zeros_like(qt)
    q_maps = (jnp.where(chan < DA_QK_DIM, qt, zero), jnp.where(chan >= DA_QK_DIM, qt, zero))

    sl = jnp.full((HEAD_W, cw), slope, jnp.float32)
    hi = sl.astype(jnp.bfloat16).astype(jnp.float32)
    mid = (sl - hi).astype(jnp.bfloat16).astype(jnp.float32)
    lo = sl - hi - mid
    frow = lax.broadcasted_iota(jnp.int32, (HEAD_W, cw), 0)
    part = frow % SLOPE_TERMS
    q_feat = jnp.where(frow < 2 * SLOPE_TERMS,
                       jnp.where(part == 0, hi, jnp.where(part == 1, mid, lo)),
                       0.0).astype(jnp.bfloat16)
    q_chain = [jnp.concatenate(
        [q_maps[c // per_map][:, (c % per_map) * cw:(c % per_map + 1) * cw], q_feat], axis=0)
        for c in range(n_chains)]

    m_ref[...] = jnp.full_like(m_ref, -jnp.inf)
    l_ref[...] = jnp.zeros_like(l_ref)
    acc_ref[...] = jnp.zeros_like(acc_ref)

    def chain_mode(c, d):
        if d is None:
            return "full"
        q_lo = (c % per_map) * cw
        if q_lo + cw <= d * tk:
            return "skip"
        if q_lo >= (d + 1) * tk:
            return "full"
        return q_lo - d * tk

    def visible_keys(c, d):
        return cw if chain_mode(c, d) == 0 else tk

    def scores_to(slot, j, d=None):
        kb = k_ref[0, pl.ds(pl.multiple_of(j * tk, tk), tk), :]
        kb = jnp.concatenate([kb, kfeat_ref[...]], axis=1)
        for c in range(n_chains):
            if chain_mode(c, d) != "skip":
                nk = visible_keys(c, d)
                s_ref[slot, c, :nk] = jnp.dot(kb[:nk], q_chain[c],
                                              preferred_element_type=jnp.float32)

    def consume(slot, j, d=None):
        chains = [c for c in range(n_chains) if chain_mode(c, d) != "skip"]
        vtb = vt_ref[:, pl.ds(pl.multiple_of(j * tk, tk), tk)]
        vtb = jnp.concatenate([vtb, jnp.ones((SUM_ROWS, tk), vtb.dtype)], axis=0)
        shift = -slope * (i * tq - j * tk).astype(jnp.float32)

        def column_max(t, st):
            c = chains[t]
            nk = visible_keys(c, d)
            s = s_ref[slot, c, :nk]
            mode = chain_mode(c, d)
            if mode != "full":
                s = s + mask_ref[:nk, mode:mode + cw]
                s_ref[slot, c, :nk] = s
            st["cmax"] = jnp.max(s, axis=0, keepdims=True) + shift

        def softmax_pv(t, st):
            c = chains[t]
            nk = visible_keys(c, d)
            lanes = slice(c * cw, (c + 1) * cw)
            m_prev = m_ref[:, lanes]
            m_new = jnp.maximum(m_prev, st.pop("cmax"))
            st["alpha"] = jnp.exp2(m_prev - m_new)
            p = jnp.exp2(s_ref[slot, c, :nk] - (m_new - shift))
            m_ref[:, lanes] = m_new
            st["pv"] = jnp.dot(vtb[:, :nk], p.astype(jnp.bfloat16),
                               preferred_element_type=jnp.float32)

        def accumulate(t, st):
            c = chains[t]
            lanes = slice(c * cw, (c + 1) * cw)
            alpha, pv = st.pop("alpha"), st.pop("pv")
            acc_ref[:, lanes] = alpha * acc_ref[:, lanes] + pv[:HEAD_W]
            l_ref[:, lanes] = alpha * l_ref[:, lanes] + pv[HEAD_W:HEAD_W + 1]

        _emit_pipelined((column_max, softmax_pv, accumulate), len(chains))

    def step(slot, j):
        scores_to(1 - slot, j + 1)
        consume(slot, j)

    scores_to(0, 0)

    def pair(jj, carry):
        step(0, 2 * jj)
        step(1, 2 * jj + 1)
        return carry

    first_diag = n_diag * i
    lax.fori_loop(0, first_diag // 2, pair, 0)
    for d in range(n_diag):
        if d + 1 < n_diag:
            scores_to((d + 1) % 2, first_diag + d + 1, d + 1)
        consume(d % 2, first_diag + d, d)

    o = acc_ref[...] / l_ref[...]
    a = o[:, :tq] - lam * o[:, tq:]
    ms = jnp.mean(a * a, axis=0, keepdims=True)
    a = a * lax.rsqrt(ms + NORM_EPS) * subln_ref[...] * (1.0 - LAMBDA_INIT)
    o_ref[0] = a.T.astype(o_ref.dtype)


def _sb_attn_kernel(q_ref, k_ref, vt_ref, o_ref, c_ref, acc_ref):
    tq, cw = SB_TQ, CHAIN_W
    n_chains = tq // cw
    i = pl.program_id(2)
    qt = q_ref[...]
    q_chain = [qt[:, c * cw:(c + 1) * cw] for c in range(n_chains)]

    krow = lax.broadcasted_iota(jnp.int32, (cw, cw), 0)
    qcol = lax.broadcasted_iota(jnp.int32, (cw, cw), 1)
    strict = krow < qcol
    lrow = lax.broadcasted_iota(jnp.int32, (cw + SUM_ROWS, cw), 0)
    lcol = lax.broadcasted_iota(jnp.int32, (cw + SUM_ROWS, cw), 1)
    later = jnp.where(jnp.logical_or(lcol > lrow, lrow >= cw), 1.0, 0.0).astype(jnp.bfloat16)

    c_ref[...] = jnp.zeros_like(c_ref)
    acc_ref[...] = jnp.zeros_like(acc_ref)

    def run_pieces(pieces):
        def scores(t, st):
            sub, c, _ = pieces[t]
            start = pl.multiple_of(sub * cw, cw)
            st["z"] = jnp.dot(k_ref[0, pl.ds(start, cw), :], q_chain[c],
                              preferred_element_type=jnp.float32)

        def suffix(t, st):
            _, _, triangular = pieces[t]
            z = st.pop("z")
            u = jnp.maximum(z, 0.0) + jnp.log2(1.0 + jnp.exp2(-jnp.abs(z)))
            st["log_sig"] = z - u
            if triangular:
                u = jnp.where(strict, u, 0.0)
            st["tail"] = jnp.dot(later, u.astype(jnp.bfloat16),
                                 preferred_element_type=jnp.float32)

        def weights_pv(t, st):
            sub, _, triangular = pieces[t]
            start = pl.multiple_of(sub * cw, cw)
            tail = st.pop("tail")
            st["usum"] = tail[cw:cw + 1]
            a = jnp.exp2(st.pop("log_sig") - tail[:cw])
            if triangular:
                a = jnp.where(strict, a, 0.0)
            st["pv"] = jnp.dot(vt_ref[:, pl.ds(start, cw)], a.astype(jnp.bfloat16),
                               preferred_element_type=jnp.float32)

        def accumulate(t, st):
            _, c, _ = pieces[t]
            lanes = slice(c * cw, (c + 1) * cw)
            carry = c_ref[:, lanes]
            acc_ref[:, lanes] += st.pop("pv") * jnp.exp2(-carry)
            c_ref[:, lanes] = carry + st.pop("usum")

        _emit_pipelined((scores, suffix, weights_pv, accumulate), len(pieces))

    diag = [n_chains * i + c for c in range(n_chains)]
    head = [(diag[c], c, True) for c in reversed(range(n_chains))]
    second = [(diag[c] - 1, c, False) for c in reversed(range(n_chains))]

    @pl.when(i == 0)
    def _():
        run_pieces(head + [p for p in second if p[1] > 0])

    @pl.when(i > 0)
    def _():
        run_pieces(head + second)

    def unfinished(c, depth):
        lanes = slice(c * cw, (c + 1) * cw)
        return jnp.logical_and(diag[c] - depth >= 0,
                               jnp.min(c_ref[:, lanes]) < SB_DONE_LOG2)

    def any_unfinished(depth):
        go = unfinished(0, depth)
        for c in range(1, n_chains):
            go = jnp.logical_or(go, unfinished(c, depth))
        return go

    def body(carry):
        depth, _ = carry
        for c in range(n_chains):
            @pl.when(unfinished(c, depth))
            def _():
                run_pieces([(diag[c] - depth, c, False)])
        return depth + 1, any_unfinished(depth + 1)

    lax.while_loop(lambda carry: carry[1], body, (jnp.int32(2), any_unfinished(2)))
    o_ref[0] = acc_ref[...].T.astype(o_ref.dtype)


def _mix_ffn2_kernel(h_ref, a_ref, b_ref, gate_ref, wa_hbm, wb_hbm, wout_hbm,
                     norm2_ref, wg_hbm, wu_hbm, wd_hbm, normf_ref, o_ref,
                     wa_ref, wb_ref, wout_ref, wg_ref, wu_ref, wd_ref,
                     stage_wide, stage_tall, sem):
    @pl.when(pl.program_id(0) == 0)
    def _():
        for src, dst in ((wa_hbm, wa_ref), (wb_hbm, wb_ref), (wout_hbm, wout_ref)):
            _stage_weight(src, dst, stage_tall, sem, rows=STAGE_SQUARE_ROWS)
        _stage_ffn_weights(wg_hbm, wu_hbm, wd_hbm, wg_ref, wu_ref, wd_ref,
                           stage_wide, stage_tall, sem)

    half = h_ref.shape[0] // 2
    rows = [slice(0, half), slice(half, 2 * half)]
    yab = [(jnp.dot(a_ref[r, :], wa_ref[...], preferred_element_type=jnp.float32),
            jnp.dot(b_ref[r, :], wb_ref[...], preferred_element_type=jnp.float32)) for r in rows]
    h2 = []
    for r, (ya, yb) in zip(rows, yab):
        gate = gate_ref[r, :].astype(jnp.float32)
        y = (gate[:, :D_MODEL] * ya + gate[:, D_MODEL:] * yb).astype(jnp.bfloat16)
        h2.append(h_ref[r, :] + jnp.dot(y, wout_ref[...], preferred_element_type=jnp.float32))
    h2 = jnp.concatenate(h2, axis=0)
    h3 = _swiglu_half_step(h2, norm2_ref[...], wg_ref, wu_ref, wd_ref)
    o_ref[...] = _rms(h3, normf_ref[...])


def _const_spec(shape):
    return pl.BlockSpec(shape, lambda *_: (0,) * len(shape), pipeline_mode=pl.Buffered(1))


def _tc_params(n_axes):
    return pltpu.CompilerParams(dimension_semantics=("arbitrary",) * n_axes,
                                vmem_limit_bytes=VMEM_LIMIT_BYTES)


def kernel(x, ffn1_norm, ffn1_w_gate, ffn1_w_up, ffn1_w_down, mix_norm, w_in, b_gate, lambda_q1, lambda_k1, lambda_q2, lambda_k2, diff_subln, w_branch_diff, w_branch_sb, w_out, ffn2_norm, ffn2_w_gate, ffn2_w_up, ffn2_w_down, final_norm):
    B, S, D = x.shape
    T = B * S
    f32, bf16 = jnp.float32, jnp.bfloat16
    xt = x.reshape(T, D)
    row = lambda v: v.reshape(1, -1).astype(f32)

    tok_spec = pl.BlockSpec((FFN_TM, D), lambda t: (t, 0))
    hbm_spec = pl.BlockSpec(memory_space=pl.ANY)
    ffn_weight_scratch = [pltpu.VMEM((D, D_FF), bf16), pltpu.VMEM((D, D_FF), bf16),
                          pltpu.VMEM((D_FF, D), bf16)]
    ffn_stage_scratch = [pltpu.VMEM((STAGE_SLOTS, STAGE_WIDE_ROWS, D_FF), f32),
                         pltpu.VMEM((STAGE_SLOTS, STAGE_TALL_ROWS, D), f32),
                         pltpu.SemaphoreType.DMA((STAGE_SLOTS,))]
    h1 = pl.pallas_call(
        _ffn1_kernel,
        grid=(T // FFN_TM,),
        in_specs=[tok_spec, _const_spec((1, D)), hbm_spec, hbm_spec, hbm_spec],
        out_specs=tok_spec,
        out_shape=jax.ShapeDtypeStruct((T, D), f32),
        scratch_shapes=ffn_weight_scratch + ffn_stage_scratch,
        compiler_params=_tc_params(1),
        name="ffn1",
    )(xt, row(ffn1_norm[0]), ffn1_w_gate[0], ffn1_w_up[0], ffn1_w_down[0])

    rowscale = jnp.ones((QVT_W,), f32)
    rowscale = rowscale.at[0:ATT_W].set(DA_QK_DIM ** -0.5 * LOG2E)
    rowscale = rowscale.at[ATT_W:2 * ATT_W].set(HEAD_W ** -0.5 * LOG2E)
    k, qvt, gates = pl.pallas_call(
        _in_proj_kernel,
        grid=(T // PROJ_TM,),
        in_specs=[pl.BlockSpec((PROJ_TM, D), lambda t: (t, 0)), _const_spec((1, D)), hbm_spec,
                  _const_spec((QVT_W, 1)), _const_spec((1, GATE_W))],
        out_specs=[pl.BlockSpec((PROJ_TM, K_W), lambda t: (t, 0)),
                   pl.BlockSpec((QVT_W, PROJ_TM), lambda t: (0, t)),
                   pl.BlockSpec((PROJ_TM, GATE_W), lambda t: (t, 0))],
        out_shape=[jax.ShapeDtypeStruct((T, K_W), bf16),
                   jax.ShapeDtypeStruct((QVT_W, T), bf16),
                   jax.ShapeDtypeStruct((T, GATE_W), bf16)],
        scratch_shapes=[pltpu.VMEM((D, K_W), bf16), pltpu.VMEM((QVT_W, D), bf16),
                        pltpu.VMEM((D, GATE_W), bf16),
                        pltpu.VMEM((STAGE_SLOTS, STAGE_IN_ROWS, GATE_W), f32),
                        pltpu.SemaphoreType.DMA((STAGE_SLOTS,))],
        compiler_params=_tc_params(1),
        name="in_proj",
    )(h1, row(mix_norm[0]), w_in[0], rowscale.reshape(-1, 1), row(b_gate[0]))
    k3 = k.reshape(B, S, K_W)

    def q_spec(slab, tq):
        return pl.BlockSpec((HEAD_W, tq),
                            lambda b, h, i: (slab * N_HEADS + h, b * (S // tq) + i))

    def k_spec(slab):
        return pl.BlockSpec((1, S, HEAD_W), lambda b, h, i: (b, 0, slab * N_HEADS + h))

    def vt_spec(slab):
        return pl.BlockSpec((HEAD_W, S), lambda b, h, i: (slab * N_HEADS + h, b))

    def att_out_spec(tq):
        return pl.BlockSpec((1, tq, HEAD_W), lambda b, h, i: (b, i, h))

    att_out_shape = jax.ShapeDtypeStruct((B, S, ATT_W), bf16)
    smem_spec = pl.BlockSpec(memory_space=pltpu.SMEM)

    lam = (jnp.exp(jnp.sum(lambda_q1[0].astype(f32) * lambda_k1[0].astype(f32)))
           - jnp.exp(jnp.sum(lambda_q2[0].astype(f32) * lambda_k2[0].astype(f32)))
           + LAMBDA_INIT).reshape(1)
    slopes = jnp.exp2(-8.0 * jnp.arange(1, N_HEADS + 1, dtype=f32) / N_HEADS)

    a = pl.pallas_call(
        _diff_attn_kernel,
        grid=(B, N_HEADS, S // ATT_TQ),
        in_specs=[smem_spec, smem_spec, q_spec(0, ATT_TQ), k_spec(0), vt_spec(2),
                  _const_spec((HEAD_W, 1))],
        out_specs=att_out_spec(ATT_TQ),
        out_shape=att_out_shape,
        scratch_shapes=[pltpu.VMEM((ATT_TK, ATT_TK), f32), pltpu.VMEM((ATT_TK, HEAD_W), bf16),
                        pltpu.VMEM((2, 2 * ATT_TQ // CHAIN_W, ATT_TK, CHAIN_W), f32),
                        pltpu.VMEM((1, 2 * ATT_TQ), f32), pltpu.VMEM((1, 2 * ATT_TQ), f32),
                        pltpu.VMEM((HEAD_W, 2 * ATT_TQ), f32)],
        compiler_params=_tc_params(3),
        name="diff_attn",
    )(slopes, lam, qvt, k3, qvt, diff_subln[0].reshape(-1, 1).astype(f32))

    b = pl.pallas_call(
        _sb_attn_kernel,
        grid=(B, N_HEADS, S // SB_TQ),
        in_specs=[q_spec(1, SB_TQ), k_spec(1), vt_spec(3)],
        out_specs=att_out_spec(SB_TQ),
        out_shape=att_out_shape,
        scratch_shapes=[pltpu.VMEM((1, SB_TQ), f32), pltpu.VMEM((HEAD_W, SB_TQ), f32)],
        compiler_params=_tc_params(3),
        name="sb_attn",
    )(qvt, k3, qvt)

    out = pl.pallas_call(
        _mix_ffn2_kernel,
        grid=(T // FFN_TM,),
        in_specs=[tok_spec,
                  pl.BlockSpec((FFN_TM, ATT_W), lambda t: (t, 0)),
                  pl.BlockSpec((FFN_TM, ATT_W), lambda t: (t, 0)),
                  pl.BlockSpec((FFN_TM, GATE_W), lambda t: (t, 0)),
                  hbm_spec, hbm_spec, hbm_spec, _const_spec((1, D)),
                  hbm_spec, hbm_spec, hbm_spec, _const_spec((1, D))],
        out_specs=tok_spec,
        out_shape=jax.ShapeDtypeStruct((T, D), f32),
        scratch_shapes=[pltpu.VMEM((ATT_W, D), bf16), pltpu.VMEM((ATT_W, D), bf16),
                        pltpu.VMEM((D, D), bf16)] + ffn_weight_scratch + ffn_stage_scratch,
        compiler_params=_tc_params(1),
        name="mix_ffn2",
    )(h1, a.reshape(T, -1), b.reshape(T, -1), gates,
      w_branch_diff[0], w_branch_sb[0], w_out[0],
      row(ffn2_norm[0]), ffn2_w_gate[0], ffn2_w_up[0], ffn2_w_down[0], row(final_norm))
    return out.reshape(B, S, D)
```

```python
import math

import jax
import jax.numpy as jnp
from jax import lax
from jax.experimental import pallas as pl
from jax.experimental.pallas import tpu as pltpu

D_MODEL = 1024
D_FF = 2816
N_HEADS = 4
HEAD_W = 128
DA_QK_DIM = 64
ATT_W = N_HEADS * HEAD_W
K_W = 2 * ATT_W
QVT_W = 4 * ATT_W
GATE_W = 2 * D_MODEL
NORM_EPS = 1e-5
LAMBDA_INIT = 0.8 - 0.6 * math.exp(-0.3 * 0)
LOG2E = 1.0 / math.log(2.0)

VMEM_LIMIT_BYTES = 56 * 1024 * 1024

STAGE_SLOTS = 3
STAGE_WIDE_ROWS = 128
STAGE_TALL_ROWS = 352
STAGE_SQUARE_ROWS = 256
STAGE_IN_ROWS = 256

FFN_TM = 512
PROJ_TM = 512
ATT_TQ = 2048
ATT_TK = 512
SB_TQ = 4096
CHAIN_W = 256
SB_DONE_LOG2 = 160.0
SUM_ROWS = 16
SLOPE_TERMS = 3
BF16_EXACT_INT = 256

_NT = (((1,), (1,)), ((), ()))


def _rms(x, g):
    ms = jnp.mean(x * x, axis=-1, keepdims=True)
    return x * lax.rsqrt(ms + NORM_EPS) * g


def _swiglu_half_step(x, norm_g, wg_ref, wu_ref, wd_ref):
    halves = jnp.split(x, 2, axis=0)
    xn = [_rms(h, norm_g).astype(jnp.bfloat16) for h in halves]
    gu = [(jnp.dot(n, wg_ref[...], preferred_element_type=jnp.float32),
           jnp.dot(n, wu_ref[...], preferred_element_type=jnp.float32)) for n in xn]
    out = []
    for h, (g, u) in zip(halves, gu):
        hact = (g * jax.nn.sigmoid(g) * u).astype(jnp.bfloat16)
        out.append(h + 0.5 * jnp.dot(hact, wd_ref[...], preferred_element_type=jnp.float32))
    return jnp.concatenate(out, axis=0)


def _stage_weight(src_hbm, dst_ref, stage_ref, sem_ref, *, rows, col0=0, ncols=None,
                  dst_row0=0, dst_col0=0, transpose=False):
    n_rows = src_hbm.shape[0]
    ncols = src_hbm.shape[1] - col0 if ncols is None else ncols
    n_slots = stage_ref.shape[0]
    assert n_rows % rows == 0 and rows <= stage_ref.shape[1] and ncols <= stage_ref.shape[2]
    n_slabs = n_rows // rows

    def slab_copy(c):
        return pltpu.make_async_copy(
            src_hbm.at[pl.ds(c * rows, rows), pl.ds(col0, ncols)],
            stage_ref.at[c % n_slots, pl.ds(0, rows), pl.ds(0, ncols)],
            sem_ref.at[c % n_slots])

    for c in range(min(n_slots - 1, n_slabs)):
        slab_copy(c).start()
    for c in range(n_slabs):
        if c + n_slots - 1 < n_slabs:
            slab_copy(c + n_slots - 1).start()
        slab_copy(c).wait()
        slab = stage_ref[c % n_slots, :rows, :ncols]
        if transpose:
            dst_ref[dst_row0:dst_row0 + ncols,
                    dst_col0 + c * rows:dst_col0 + (c + 1) * rows] = slab.T.astype(dst_ref.dtype)
        else:
            dst_ref[dst_row0 + c * rows:dst_row0 + (c + 1) * rows,
                    dst_col0:dst_col0 + ncols] = slab.astype(dst_ref.dtype)


def _stage_ffn_weights(wg_hbm, wu_hbm, wd_hbm, wg_ref, wu_ref, wd_ref, stage_wide, stage_tall, sem):
    _stage_weight(wg_hbm, wg_ref, stage_wide, sem, rows=STAGE_WIDE_ROWS)
    _stage_weight(wu_hbm, wu_ref, stage_wide, sem, rows=STAGE_WIDE_ROWS)
    _stage_weight(wd_hbm, wd_ref, stage_tall, sem, rows=STAGE_TALL_ROWS)


def _ffn1_kernel(x_ref, norm_ref, wg_hbm, wu_hbm, wd_hbm, o_ref,
                 wg_ref, wu_ref, wd_ref, stage_wide, stage_tall, sem):
    @pl.when(pl.program_id(0) == 0)
    def _():
        _stage_ffn_weights(wg_hbm, wu_hbm, wd_hbm, wg_ref, wu_ref, wd_ref,
                           stage_wide, stage_tall, sem)

    o_ref[...] = _swiglu_half_step(x_ref[...], norm_ref[...], wg_ref, wu_ref, wd_ref)


def _in_proj_kernel(h_ref, norm_ref, win_hbm, rowscale_ref, bgate_ref,
                    k_ref, qvt_ref, gate_ref, wk_ref, wqvt_ref, wgate_ref, stage, sem):
    @pl.when(pl.program_id(0) == 0)
    def _():
        w = ATT_W
        for piece, src_block in enumerate((1, 4)):
            _stage_weight(win_hbm, wk_ref, stage, sem, rows=STAGE_IN_ROWS,
                          col0=src_block * w, ncols=w, dst_col0=piece * w)
        _stage_weight(win_hbm, wgate_ref, stage, sem, rows=STAGE_IN_ROWS, col0=6 * w, ncols=GATE_W)
        for piece, src_block in enumerate((0, 3, 2, 5)):
            _stage_weight(win_hbm, wqvt_ref, stage, sem, rows=STAGE_IN_ROWS,
                          col0=src_block * w, ncols=w, dst_row0=piece * w, transpose=True)

    half = h_ref.shape[0] // 2
    rows = [slice(0, half), slice(half, 2 * half)]
    n = [_rms(h_ref[r, :], norm_ref[...]).astype(jnp.bfloat16) for r in rows]
    g = [jnp.dot(nh, wgate_ref[...], preferred_element_type=jnp.float32) for nh in n]
    for r, gh in zip(rows, g):
        gate_ref[r, :] = jax.nn.sigmoid(gh + bgate_ref[...]).astype(jnp.bfloat16)
    for r, nh in zip(rows, n):
        k_ref[r, :] = jnp.dot(nh, wk_ref[...],
                              preferred_element_type=jnp.float32).astype(jnp.bfloat16)
    for r, nh in zip(rows, n):
        qvt = lax.dot_general(wqvt_ref[...], nh, _NT, preferred_element_type=jnp.float32)
        qvt_ref[:, r] = (qvt * rowscale_ref[...]).astype(jnp.bfloat16)


def _emit_pipelined(stages, n):
    state = [dict() for _ in range(n)]
    for step in range(n + len(stages) - 1):
        for s, stage in enumerate(stages):
            t = step - s
            if 0 <= t < n:
                stage(t, state[t])


def _diff_attn_kernel(slope_ref, lam_ref, q_ref, k_ref, vt_ref, subln_ref, o_ref,
                      mask_ref, kfeat_ref, s_ref, m_ref, l_ref, acc_ref):
    tq, tk, cw = ATT_TQ, ATT_TK, CHAIN_W
    per_map = tq // cw
    n_chains = 2 * per_map
    n_diag = tq // tk
    assert n_diag % 2 == 0 and tk == 2 * cw and tk <= 2 * BF16_EXACT_INT
    h = pl.program_id(1)
    i = pl.program_id(2)
    slope = slope_ref[h] * LOG2E
    lam = lam_ref[0]

    @pl.when(jnp.logical_and(pl.program_id(0) == 0, jnp.logical_and(h == 0, i == 0)))
    def _():
        krow = lax.broadcasted_iota(jnp.int32, (tk, tk), 0)
        qcol = lax.broadcasted_iota(jnp.int32, (tk, tk), 1)
        mask_ref[...] = jnp.where(qcol >= krow, 0.0, -jnp.inf)
        kpos = lax.broadcasted_iota(jnp.int32, (tk, HEAD_W), 0)
        klane = lax.broadcasted_iota(jnp.int32, (tk, HEAD_W), 1)
        k_hi = jnp.where(kpos >= BF16_EXACT_INT, BF16_EXACT_INT, 0)
        kfeat_ref[...] = jnp.where(klane < SLOPE_TERMS, k_hi,
                                   jnp.where(klane < 2 * SLOPE_TERMS, kpos - k_hi, 0)
                                   ).astype(jnp.float32).astype(jnp.bfloat16)

    qt = q_ref[...]
    chan = lax.broadcasted_iota(jnp.int32, (HEAD_W, tq), 0)
    zero = jnp.zeros_like(qt)
    q_maps = (jnp.where(chan < DA_QK_DIM, qt, zero), jnp.where(chan >= DA_QK_DIM, qt, zero))

    sl = jnp.full((HEAD_W, cw), slope, jnp.float32)
    hi = sl.astype(jnp.bfloat16).astype(jnp.float32)
    mid = (sl - hi).astype(jnp.bfloat16).astype(jnp.float32)
    lo = sl - hi - mid
    frow = lax.broadcasted_iota(jnp.int32, (HEAD_W, cw), 0)
    part = frow % SLOPE_TERMS
    q_feat = jnp.where(frow < 2 * SLOPE_TERMS,
                       jnp.where(part == 0, hi, jnp.where(part == 1, mid, lo)),
                       0.0).astype(jnp.bfloat16)
    q_chain = [jnp.concatenate(
        [q_maps[c // per_map][:, (c % per_map) * cw:(c % per_map + 1) * cw], q_feat], axis=0)
        for c in range(n_chains)]

    m_ref[...] = jnp.full_like(m_ref, -jnp.inf)
    l_ref[...] = jnp.zeros_like(l_ref)
    acc_ref[...] = jnp.zeros_like(acc_ref)

    def chain_mode(c, d):
        if d is None:
            return "full"
        q_lo = (c % per_map) * cw
        if q_lo + cw <= d * tk:
            return "skip"
        if q_lo >= (d + 1) * tk:
            return "full"
        return q_lo - d * tk

    def visible_keys(c, d):
        return cw if chain_mode(c, d) == 0 else tk

    def scores_to(slot, j, d=None):
        kb = k_ref[0, pl.ds(pl.multiple_of(j * tk, tk), tk), :]
        kb = jnp.concatenate([kb, kfeat_ref[...]], axis=1)
        for c in range(n_chains):
            if chain_mode(c, d) != "skip":
                nk = visible_keys(c, d)
                s_ref[slot, c, :nk] = jnp.dot(kb[:nk], q_chain[c],
                                              preferred_element_type=jnp.float32)

    def consume(slot, j, d=None):
        chains = [c for c in range(n_chains) if chain_mode(c, d) != "skip"]
        vtb = vt_ref[:, pl.ds(pl.multiple_of(j * tk, tk), tk)]
        vtb = jnp.concatenate([vtb, jnp.ones((SUM_ROWS, tk), vtb.dtype)], axis=0)
        shift = -slope * (i * tq - j * tk).astype(jnp.float32)

        def column_max(t, st):
            c = chains[t]
            nk = visible_keys(c, d)
            s = s_ref[slot, c, :nk]
            mode = chain_mode(c, d)
            if mode != "full":
                s = s + mask_ref[:nk, mode:mode + cw]
                s_ref[slot, c, :nk] = s
            st["cmax"] = jnp.max(s, axis=0, keepdims=True) + shift

        def softmax_pv(t, st):
            c = chains[t]
            nk = visible_keys(c, d)
            lanes = slice(c * cw, (c + 1) * cw)
            m_prev = m_ref[:, lanes]
            m_new = jnp.maximum(m_prev, st.pop("cmax"))
            st["alpha"] = jnp.exp2(m_prev - m_new)
            p = jnp.exp2(s_ref[slot, c, :nk] - (m_new - shift))
            m_ref[:, lanes] = m_new
            st["pv"] = jnp.dot(vtb[:, :nk], p.astype(jnp.bfloat16),
                               preferred_element_type=jnp.float32)

        def accumulate(t, st):
            c = chains[t]
            lanes = slice(c * cw, (c + 1) * cw)
            alpha, pv = st.pop("alpha"), st.pop("pv")
            acc_ref[:, lanes] = alpha * acc_ref[:, lanes] + pv[:HEAD_W]
            l_ref[:, lanes] = alpha * l_ref[:, lanes] + pv[HEAD_W:HEAD_W + 1]

        _emit_pipelined((column_max, softmax_pv, accumulate), len(chains))

    def step(slot, j):
        scores_to(1 - slot, j + 1)
        consume(slot, j)

    scores_to(0, 0)

    def pair(jj, carry):
        step(0, 2 * jj)
        step(1, 2 * jj + 1)
        return carry

    first_diag = n_diag * i
    lax.fori_loop(0, first_diag // 2, pair, 0)
    for d in range(n_diag):
        if d + 1 < n_diag:
            scores_to((d + 1) % 2, first_diag + d + 1, d + 1)
        consume(d % 2, first_diag + d, d)

    o = acc_ref[...] / l_ref[...]
    a = o[:, :tq] - lam * o[:, tq:]
    ms = jnp.mean(a * a, axis=0, keepdims=True)
    a = a * lax.rsqrt(ms + NORM_EPS) * subln_ref[...] * (1.0 - LAMBDA_INIT)
    o_ref[0] = a.T.astype(o_ref.dtype)


def _sb_attn_kernel(q_ref, k_ref, vt_ref, o_ref, c_ref, acc_ref):
    tq, cw = SB_TQ, CHAIN_W
    n_chains = tq // cw
    i = pl.program_id(2)
    qt = q_ref[...]
    q_chain = [qt[:, c * cw:(c + 1) * cw] for c in range(n_chains)]

    krow = lax.broadcasted_iota(jnp.int32, (cw, cw), 0)
    qcol = lax.broadcasted_iota(jnp.int32, (cw, cw), 1)
    strict = krow < qcol
    lrow = lax.broadcasted_iota(jnp.int32, (cw + SUM_ROWS, cw), 0)
    lcol = lax.broadcasted_iota(jnp.int32, (cw + SUM_ROWS, cw), 1)
    later = jnp.where(jnp.logical_or(lcol > lrow, lrow >= cw), 1.0, 0.0).astype(jnp.bfloat16)

    c_ref[...] = jnp.zeros_like(c_ref)
    acc_ref[...] = jnp.zeros_like(acc_ref)

    def run_pieces(pieces):
        def scores(t, st):
            sub, c, _ = pieces[t]
            start = pl.multiple_of(sub * cw, cw)
            st["z"] = jnp.dot(k_ref[0, pl.ds(start, cw), :], q_chain[c],
                              preferred_element_type=jnp.float32)

        def suffix(t, st):
            _, _, triangular = pieces[t]
            z = st.pop("z")
            u = jnp.maximum(z, 0.0) + jnp.log2(1.0 + jnp.exp2(-jnp.abs(z)))
            st["log_sig"] = z - u
            if triangular:
                u = jnp.where(strict, u, 0.0)
            st["tail"] = jnp.dot(later, u.astype(jnp.bfloat16),
                                 preferred_element_type=jnp.float32)

        def weights_pv(t, st):
            sub, _, triangular = pieces[t]
            start = pl.multiple_of(sub * cw, cw)
            tail = st.pop("tail")
            st["usum"] = tail[cw:cw + 1]
            a = jnp.exp2(st.pop("log_sig") - tail[:cw])
            if triangular:
                a = jnp.where(strict, a, 0.0)
            st["pv"] = jnp.dot(vt_ref[:, pl.ds(start, cw)], a.astype(jnp.bfloat16),
                               preferred_element_type=jnp.float32)

        def accumulate(t, st):
            _, c, _ = pieces[t]
            lanes = slice(c * cw, (c + 1) * cw)
            carry = c_ref[:, lanes]
            acc_ref[:, lanes] += st.pop("pv") * jnp.exp2(-carry)
            c_ref[:, lanes] = carry + st.pop("usum")

        _emit_pipelined((scores, suffix, weights_pv, accumulate), len(pieces))

    diag = [n_chains * i + c for c in range(n_chains)]
    head = [(diag[c], c, True) for c in reversed(range(n_chains))]
    second = [(diag[c] - 1, c, False) for c in reversed(range(n_chains))]

    @pl.when(i == 0)
    def _():
        run_pieces(head + [p for p in second if p[1] > 0])

    @pl.when(i > 0)
    def _():
        run_pieces(head + second)

    def unfinished(c, depth):
        lanes = slice(c * cw, (c + 1) * cw)
        return jnp.logical_and(diag[c] - depth >= 0,
                               jnp.min(c_ref[:, lanes]) < SB_DONE_LOG2)

    def any_unfinished(depth):
        go = unfinished(0, depth)
        for c in range(1, n_chains):
            go = jnp.logical_or(go, unfinished(c, depth))
        return go

    def body(carry):
        depth, _ = carry
        for c in range(n_chains):
            @pl.when(unfinished(c, depth))
            def _():
                run_pieces([(diag[c] - depth, c, False)])
        return depth + 1, any_unfinished(depth + 1)

    lax.while_loop(lambda carry: carry[1], body, (jnp.int32(2), any_unfinished(2)))
    o_ref[0] = acc_ref[...].T.astype(o_ref.dtype)


def _mix_ffn2_kernel(h_ref, a_ref, b_ref, gate_ref, wa_hbm, wb_hbm, wout_hbm,
                     norm2_ref, wg_hbm, wu_hbm, wd_hbm, normf_ref, o_ref,
                     wa_ref, wb_ref, wout_ref, wg_ref, wu_ref, wd_ref,
                     stage_wide, stage_tall, sem):
    @pl.when(pl.program_id(0) == 0)
    def _():
        for src, dst in ((wa_hbm, wa_ref), (wb_hbm, wb_ref), (wout_hbm, wout_ref)):
            _stage_weight(src, dst, stage_tall, sem, rows=STAGE_SQUARE_ROWS)
        _stage_ffn_weights(wg_hbm, wu_hbm, wd_hbm, wg_ref, wu_ref, wd_ref,
                           stage_wide, stage_tall, sem)

    half = h_ref.shape[0] // 2
    rows = [slice(0, half), slice(half, 2 * half)]
    yab = [(jnp.dot(a_ref[r, :], wa_ref[...], preferred_element_type=jnp.float32),
            jnp.dot(b_ref[r, :], wb_ref[...], preferred_element_type=jnp.float32)) for r in rows]
    h2 = []
    for r, (ya, yb) in zip(rows, yab):
        gate = gate_ref[r, :].astype(jnp.float32)
        y = (gate[:, :D_MODEL] * ya + gate[:, D_MODEL:] * yb).astype(jnp.bfloat16)
        h2.append(h_ref[r, :] + jnp.dot(y, wout_ref[...], preferred_element_type=jnp.float32))
    h2 = jnp.concatenate(h2, axis=0)
    h3 = _swiglu_half_step(h2, norm2_ref[...], wg_ref, wu_ref, wd_ref)
    o_ref[...] = _rms(h3, normf_ref[...])


def _const_spec(shape):
    return pl.BlockSpec(shape, lambda *_: (0,) * len(shape), pipeline_mode=pl.Buffered(1))


def _tc_params(n_axes):
    return pltpu.CompilerParams(dimension_semantics=("arbitrary",) * n_axes,
                                vmem_limit_bytes=VMEM_LIMIT_BYTES)


def kernel(x, ffn1_norm, ffn1_w_gate, ffn1_w_up, ffn1_w_down, mix_norm, w_in, b_gate, lambda_q1, lambda_k1, lambda_q2, lambda_k2, diff_subln, w_branch_diff, w_branch_sb, w_out, ffn2_norm, ffn2_w_gate, ffn2_w_up, ffn2_w_down, final_norm):
    B, S, D = x.shape
    T = B * S
    f32, bf16 = jnp.float32, jnp.bfloat16
    xt = x.reshape(T, D)
    row = lambda v: v.reshape(1, -1).astype(f32)

    tok_spec = pl.BlockSpec((FFN_TM, D), lambda t: (t, 0))
    hbm_spec = pl.BlockSpec(memory_space=pl.ANY)
    ffn_weight_scratch = [pltpu.VMEM((D, D_FF), bf16), pltpu.VMEM((D, D_FF), bf16),
                          pltpu.VMEM((D_FF, D), bf16)]
    ffn_stage_scratch = [pltpu.VMEM((STAGE_SLOTS, STAGE_WIDE_ROWS, D_FF), f32),
                         pltpu.VMEM((STAGE_SLOTS, STAGE_TALL_ROWS, D), f32),
                         pltpu.SemaphoreType.DMA((STAGE_SLOTS,))]
    h1 = pl.pallas_call(
        _ffn1_kernel,
        grid=(T // FFN_TM,),
        in_specs=[tok_spec, _const_spec((1, D)), hbm_spec, hbm_spec, hbm_spec],
        out_specs=tok_spec,
        out_shape=jax.ShapeDtypeStruct((T, D), f32),
        scratch_shapes=ffn_weight_scratch + ffn_stage_scratch,
        compiler_params=_tc_params(1),
        name="ffn1",
    )(xt, row(ffn1_norm[0]), ffn1_w_gate[0], ffn1_w_up[0], ffn1_w_down[0])

    rowscale = jnp.ones((QVT_W,), f32)
    rowscale = rowscale.at[0:ATT_W].set(DA_QK_DIM ** -0.5 * LOG2E)
    rowscale = rowscale.at[ATT_W:2 * ATT_W].set(HEAD_W ** -0.5 * LOG2E)
    k, qvt, gates = pl.pallas_call(
        _in_proj_kernel,
        grid=(T // PROJ_TM,),
        in_specs=[pl.BlockSpec((PROJ_TM, D), lambda t: (t, 0)), _const_spec((1, D)), hbm_spec,
                  _const_spec((QVT_W, 1)), _const_spec((1, GATE_W))],
        out_specs=[pl.BlockSpec((PROJ_TM, K_W), lambda t: (t, 0)),
                   pl.BlockSpec((QVT_W, PROJ_TM), lambda t: (0, t)),
                   pl.BlockSpec((PROJ_TM, GATE_W), lambda t: (t, 0))],
        out_shape=[jax.ShapeDtypeStruct((T, K_W), bf16),
                   jax.ShapeDtypeStruct((QVT_W, T), bf16),
                   jax.ShapeDtypeStruct((T, GATE_W), bf16)],
        scratch_shapes=[pltpu.VMEM((D, K_W), bf16), pltpu.VMEM((QVT_W, D), bf16),
                        pltpu.VMEM((D, GATE_W), bf16),
                        pltpu.VMEM((STAGE_SLOTS, STAGE_IN_ROWS, GATE_W), f32),
                        pltpu.SemaphoreType.DMA((STAGE_SLOTS,))],
        compiler_params=_tc_params(1),
        name="in_proj",
    )(h1, row(mix_norm[0]), w_in[0], rowscale.reshape(-1, 1), row(b_gate[0]))
    k3 = k.reshape(B, S, K_W)

    def q_spec(slab, tq):
        return pl.BlockSpec((HEAD_W, tq),
                            lambda b, h, i: (slab * N_HEADS + h, b * (S // tq) + i))

    def k_spec(slab):
        return pl.BlockSpec((1, S, HEAD_W), lambda b, h, i: (b, 0, slab * N_HEADS + h))

    def vt_spec(slab):
        return pl.BlockSpec((HEAD_W, S), lambda b, h, i: (slab * N_HEADS + h, b))

    def att_out_spec(tq):
        return pl.BlockSpec((1, tq, HEAD_W), lambda b, h, i: (b, i, h))

    att_out_shape = jax.ShapeDtypeStruct((B, S, ATT_W), bf16)
    smem_spec = pl.BlockSpec(memory_space=pltpu.SMEM)

    lam = (jnp.exp(jnp.sum(lambda_q1[0].astype(f32) * lambda_k1[0].astype(f32)))
           - jnp.exp(jnp.sum(lambda_q2[0].astype(f32) * lambda_k2[0].astype(f32)))
           + LAMBDA_INIT).reshape(1)
    slopes = jnp.exp2(-8.0 * jnp.arange(1, N_HEADS + 1, dtype=f32) / N_HEADS)

    a = pl.pallas_call(
        _diff_attn_kernel,
        grid=(B, N_HEADS, S // ATT_TQ),
        in_specs=[smem_spec, smem_spec, q_spec(0, ATT_TQ), k_spec(0), vt_spec(2),
                  _const_spec((HEAD_W, 1))],
        out_specs=att_out_spec(ATT_TQ),
        out_shape=att_out_shape,
        scratch_shapes=[pltpu.VMEM((ATT_TK, ATT_TK), f32), pltpu.VMEM((ATT_TK, HEAD_W), bf16),
                        pltpu.VMEM((2, 2 * ATT_TQ // CHAIN_W, ATT_TK, CHAIN_W), f32),
                        pltpu.VMEM((1, 2 * ATT_TQ), f32), pltpu.VMEM((1, 2 * ATT_TQ), f32),
                        pltpu.VMEM((HEAD_W, 2 * ATT_TQ), f32)],
        compiler_params=_tc_params(3),
        name="diff_attn",
    )(slopes, lam, qvt, k3, qvt, diff_subln[0].reshape(-1, 1).astype(f32))

    b = pl.pallas_call(
        _sb_attn_kernel,
        grid=(B, N_HEADS, S // SB_TQ),
        in_specs=[q_spec(1, SB_TQ), k_spec(1), vt_spec(3)],
        out_specs=att_out_spec(SB_TQ),
        out_shape=att_out_shape,
        scratch_shapes=[pltpu.VMEM((1, SB_TQ), f32), pltpu.VMEM((HEAD_W, SB_TQ), f32)],
        compiler_params=_tc_params(3),
        name="sb_attn",
    )(qvt, k3, qvt)

    out = pl.pallas_call(
        _mix_ffn2_kernel,
        grid=(T // FFN_TM,),
        in_specs=[tok_spec,
                  pl.BlockSpec((FFN_TM, ATT_W), lambda t: (t, 0)),
                  pl.BlockSpec((FFN_TM, ATT_W), lambda t: (t, 0)),
                  pl.BlockSpec((FFN_TM, GATE_W), lambda t: (t, 0)),
                  hbm_spec, hbm_spec, hbm_spec, _const_spec((1, D)),
                  hbm_spec, hbm_spec, hbm_spec, _const_spec((1, D))],
        out_specs=tok_spec,
        out_shape=jax.ShapeDtypeStruct((T, D), f32),
        scratch_shapes=[pltpu.VMEM((ATT_W, D), bf16), pltpu.VMEM((ATT_W, D), bf16),
                        pltpu.VMEM((D, D), bf16)] + ffn_weight_scratch + ffn_stage_scratch,
        compiler_params=_tc_params(1),
        name="mix_ffn2",
    )(h1, a.reshape(T, -1), b.reshape(T, -1), gates,
      w_branch_diff[0], w_branch_sb[0], w_out[0],
      row(ffn2_norm[0]), ffn2_w_gate[0], ffn2_w_up[0], ffn2_w_down[0], row(final_norm))
    return out.reshape(B, S, D)
```

```python
import math

import jax
import jax.numpy as jnp
from jax import lax
from jax.experimental import pallas as pl
from jax.experimental.pallas import tpu as pltpu

D_MODEL = 1024
D_FF = 2816
N_HEADS = 4
HEAD_W = 128
DA_QK_DIM = 64
ATT_W = N_HEADS * HEAD_W
K_W = 2 * ATT_W
QVT_W = 4 * ATT_W
GATE_W = 2 * D_MODEL
NORM_EPS = 1e-5
LAMBDA_INIT = 0.8 - 0.6 * math.exp(-0.3 * 0)
LOG2E = 1.0 / math.log(2.0)

VMEM_LIMIT_BYTES = 56 * 1024 * 1024

STAGE_SLOTS = 3
STAGE_WIDE_ROWS = 128
STAGE_TALL_ROWS = 352
STAGE_SQUARE_ROWS = 256
STAGE_IN_ROWS = 256

FFN_TM = 512
PROJ_TM = 512
ATT_TQ = 2048
ATT_TK = 512
SB_TQ = 4096
CHAIN_W = 256
SB_DONE_LOG2 = 160.0
SUM_ROWS = 16
SLOPE_TERMS = 3
BF16_EXACT_INT = 256

_NT = (((1,), (1,)), ((), ()))


def _rms(x, g):
    ms = jnp.mean(x * x, axis=-1, keepdims=True)
    return x * lax.rsqrt(ms + NORM_EPS) * g


def _swiglu_half_step(x, norm_g, wg_ref, wu_ref, wd_ref):
    halves = jnp.split(x, 2, axis=0)
    xn = [_rms(h, norm_g).astype(jnp.bfloat16) for h in halves]
    gu = [(jnp.dot(n, wg_ref[...], preferred_element_type=jnp.float32),
           jnp.dot(n, wu_ref[...], preferred_element_type=jnp.float32)) for n in xn]
    out = []
    for h, (g, u) in zip(halves, gu):
        hact = (g * jax.nn.sigmoid(g) * u).astype(jnp.bfloat16)
        out.append(h + 0.5 * jnp.dot(hact, wd_ref[...], preferred_element_type=jnp.float32))
    return jnp.concatenate(out, axis=0)


def _stage_weight(src_hbm, dst_ref, stage_ref, sem_ref, *, rows, col0=0, ncols=None,
                  dst_row0=0, dst_col0=0, transpose=False):
    n_rows = src_hbm.shape[0]
    ncols = src_hbm.shape[1] - col0 if ncols is None else ncols
    n_slots = stage_ref.shape[0]
    assert n_rows % rows == 0 and rows <= stage_ref.shape[1] and ncols <= stage_ref.shape[2]
    n_slabs = n_rows // rows

    def slab_copy(c):
        return pltpu.make_async_copy(
            src_hbm.at[pl.ds(c * rows, rows), pl.ds(col0, ncols)],
            stage_ref.at[c % n_slots, pl.ds(0, rows), pl.ds(0, ncols)],
            sem_ref.at[c % n_slots])

    for c in range(min(n_slots - 1, n_slabs)):
        slab_copy(c).start()
    for c in range(n_slabs):
        if c + n_slots - 1 < n_slabs:
            slab_copy(c + n_slots - 1).start()
        slab_copy(c).wait()
        slab = stage_ref[c % n_slots, :rows, :ncols]
        if transpose:
            dst_ref[dst_row0:dst_row0 + ncols,
                    dst_col0 + c * rows:dst_col0 + (c + 1) * rows] = slab.T.astype(dst_ref.dtype)
        else:
            dst_ref[dst_row0 + c * rows:dst_row0 + (c + 1) * rows,
                    dst_col0:dst_col0 + ncols] = slab.astype(dst_ref.dtype)


def _stage_ffn_weights(wg_hbm, wu_hbm, wd_hbm, wg_ref, wu_ref, wd_ref, stage_wide, stage_tall, sem):
    _stage_weight(wg_hbm, wg_ref, stage_wide, sem, rows=STAGE_WIDE_ROWS)
    _stage_weight(wu_hbm, wu_ref, stage_wide, sem, rows=STAGE_WIDE_ROWS)
    _stage_weight(wd_hbm, wd_ref, stage_tall, sem, rows=STAGE_TALL_ROWS)


def _ffn1_kernel(x_ref, norm_ref, wg_hbm, wu_hbm, wd_hbm, o_ref,
                 wg_ref, wu_ref, wd_ref, stage_wide, stage_tall, sem):
    @pl.when(pl.program_id(0) == 0)
    def _():
        _stage_ffn_weights(wg_hbm, wu_hbm, wd_hbm, wg_ref, wu_ref, wd_ref,
                           stage_wide, stage_tall, sem)

    o_ref[...] = _swiglu_half_step(x_ref[...], norm_ref[...], wg_ref, wu_ref, wd_ref)


def _in_proj_kernel(h_ref, norm_ref, win_hbm, rowscale_ref, bgate_ref,
                    k_ref, qvt_ref, gate_ref, wk_ref, wqvt_ref, wgate_ref, stage, sem):
    @pl.when(pl.program_id(0) == 0)
    def _():
        w = ATT_W
        for piece, src_block in enumerate((1, 4)):
            _stage_weight(win_hbm, wk_ref, stage, sem, rows=STAGE_IN_ROWS,
                          col0=src_block * w, ncols=w, dst_col0=piece * w)
        _stage_weight(win_hbm, wgate_ref, stage, sem, rows=STAGE_IN_ROWS, col0=6 * w, ncols=GATE_W)
        for piece, src_block in enumerate((0, 3, 2, 5)):
            _stage_weight(win_hbm, wqvt_ref, stage, sem, rows=STAGE_IN_ROWS,
                          col0=src_block * w, ncols=w, dst_row0=piece * w, transpose=True)

    half = h_ref.shape[0] // 2
    rows = [slice(0, half), slice(half, 2 * half)]
    n = [_rms(h_ref[r, :], norm_ref[...]).astype(jnp.bfloat16) for r in rows]
    g = [jnp.dot(nh, wgate_ref[...], preferred_element_type=jnp.float32) for nh in n]
    for r, gh in zip(rows, g):
        gate_ref[r, :] = jax.nn.sigmoid(gh + bgate_ref[...]).astype(jnp.bfloat16)
    for r, nh in zip(rows, n):
        k_ref[r, :] = jnp.dot(nh, wk_ref[...],
                              preferred_element_type=jnp.float32).astype(jnp.bfloat16)
    for r, nh in zip(rows, n):
        qvt = lax.dot_general(wqvt_ref[...], nh, _NT, preferred_element_type=jnp.float32)
        qvt_ref[:, r] = (qvt * rowscale_ref[...]).astype(jnp.bfloat16)


def _emit_pipelined(stages, n):
    state = [dict() for _ in range(n)]
    for step in range(n + len(stages) - 1):
        for s, stage in enumerate(stages):
            t = step - s
            if 0 <= t < n:
                stage(t, state[t])


def _diff_attn_kernel(slope_ref, lam_ref, q_ref, k_ref, vt_ref, subln_ref, o_ref,
                      mask_ref, kfeat_ref, s_ref, m_ref, l_ref, acc_ref):
    tq, tk, cw = ATT_TQ, ATT_TK, CHAIN_W
    per_map = tq // cw
    n_chains = 2 * per_map
    n_diag = tq // tk
    assert n_diag % 2 == 0 and tk == 2 * cw and tk <= 2 * BF16_EXACT_INT
    h = pl.program_id(1)
    i = pl.program_id(2)
    slope = slope_ref[h] * LOG2E
    lam = lam_ref[0]

    @pl.when(jnp.logical_and(pl.program_id(0) == 0, jnp.logical_and(h == 0, i == 0)))
    def _():
        krow = lax.broadcasted_iota(jnp.int32, (tk, tk), 0)
        qcol = lax.broadcasted_iota(jnp.int32, (tk, tk), 1)
        mask_ref[...] = jnp.where(qcol >= krow, 0.0, -jnp.inf)
        kpos = lax.broadcasted_iota(jnp.int32, (tk, HEAD_W), 0)
        klane = lax.broadcasted_iota(jnp.int32, (tk, HEAD_W), 1)
        k_hi = jnp.where(kpos >= BF16_EXACT_INT, BF16_EXACT_INT, 0)
        kfeat_ref[...] = jnp.where(klane < SLOPE_TERMS, k_hi,
                                   jnp.where(klane < 2 * SLOPE_TERMS, kpos - k_hi, 0)
                                   ).astype(jnp.float32).astype(jnp.bfloat16)

    qt = q_ref[...]
    chan = lax.broadcasted_iota(jnp.int32, (HEAD_W, tq), 0)
    zero = jnp.zeros_like(qt)
    q_maps = (jnp.where(chan < DA_QK_DIM, qt, zero), jnp.where(chan >= DA_QK_DIM, qt, zero))

    sl = jnp.full((HEAD_W, cw), slope, jnp.float32)
    hi = sl.astype(jnp.bfloat16).astype(jnp.float32)
    mid = (sl - hi).astype(jnp.bfloat16).astype(jnp.float32)
    lo = sl - hi - mid
    frow = lax.broadcasted_iota(jnp.int32, (HEAD_W, cw), 0)
    part = frow % SLOPE_TERMS
    q_feat = jnp.where(frow < 2 * SLOPE_TERMS,
                       jnp.where(part == 0, hi, jnp.where(part == 1, mid, lo)),
                       0.0).astype(jnp.bfloat16)
    q_chain = [jnp.concatenate(
        [q_maps[c // per_map][:, (c % per_map) * cw:(c % per_map + 1) * cw], q_feat], axis=0)
        for c in range(n_chains)]

    m_ref[...] = jnp.full_like(m_ref, -jnp.inf)
    l_ref[...] = jnp.zeros_like(l_ref)
    acc_ref[...] = jnp.zeros_like(acc_ref)

    def chain_mode(c, d):
        if d is None:
            return "full"
        q_lo = (c % per_map) * cw
        if q_lo + cw <= d * tk:
            return "skip"
        if q_lo >= (d + 1) * tk:
            return "full"
        return q_lo - d * tk

    def visible_keys(c, d):
        return cw if chain_mode(c, d) == 0 else tk

    def score_stage(slot, j, d=None):
        chains = [c for c in range(n_chains) if chain_mode(c, d) != "skip"]
        kb = k_ref[0, pl.ds(pl.multiple_of(j * tk, tk), tk), :]
        kb = jnp.concatenate([kb, kfeat_ref[...]], axis=1)

        def scores(t, st):
            if t < len(chains):
                c = chains[t]
                nk = visible_keys(c, d)
                s_ref[slot, c, :nk] = jnp.dot(kb[:nk], q_chain[c],
                                              preferred_element_type=jnp.float32)

        return len(chains), scores

    def scores_to(slot, j, d=None):
        n, scores = score_stage(slot, j, d)
        for t in range(n):
            scores(t, None)

    def consume(slot, j, d=None, ahead=None):
        chains = [c for c in range(n_chains) if chain_mode(c, d) != "skip"]
        vtb = vt_ref[:, pl.ds(pl.multiple_of(j * tk, tk), tk)]
        vtb = jnp.concatenate([vtb, jnp.ones((SUM_ROWS, tk), vtb.dtype)], axis=0)
        shift = -slope * (i * tq - j * tk).astype(jnp.float32)

        def column_max(t, st):
            if t >= len(chains):
                return
            c = chains[t]
            nk = visible_keys(c, d)
            s = s_ref[slot, c, :nk]
            mode = chain_mode(c, d)
            if mode != "full":
                s = s + mask_ref[:nk, mode:mode + cw]
                s_ref[slot, c, :nk] = s
            st["cmax"] = jnp.max(s, axis=0, keepdims=True) + shift

        def softmax_pv(t, st):
            if t >= len(chains):
                return
            c = chains[t]
            nk = visible_keys(c, d)
            lanes = slice(c * cw, (c + 1) * cw)
            m_prev = m_ref[:, lanes]
            m_new = jnp.maximum(m_prev, st.pop("cmax"))
            st["alpha"] = jnp.exp2(m_prev - m_new)
            p = jnp.exp2(s_ref[slot, c, :nk] - (m_new - shift))
            m_ref[:, lanes] = m_new
            st["pv"] = jnp.dot(vtb[:, :nk], p.astype(jnp.bfloat16),
                               preferred_element_type=jnp.float32)

        def accumulate(t, st):
            if t >= len(chains):
                return
            c = chains[t]
            lanes = slice(c * cw, (c + 1) * cw)
            alpha, pv = st.pop("alpha"), st.pop("pv")
            acc_ref[:, lanes] = alpha * acc_ref[:, lanes] + pv[:HEAD_W]
            l_ref[:, lanes] = alpha * l_ref[:, lanes] + pv[HEAD_W:HEAD_W + 1]

        n_ahead, ahead_stage = ahead if ahead is not None else (0, None)
        stages = (column_max, softmax_pv, accumulate)
        if ahead is not None:
            stages = (ahead_stage,) + stages
        _emit_pipelined(stages, max(len(chains), n_ahead))

    def step(slot, j):
        consume(slot, j, ahead=score_stage(1 - slot, j + 1))

    scores_to(0, 0)

    def pair(jj, carry):
        step(0, 2 * jj)
        step(1, 2 * jj + 1)
        return carry

    first_diag = n_diag * i
    lax.fori_loop(0, first_diag // 2, pair, 0)
    for d in range(n_diag):
        ahead = score_stage((d + 1) % 2, first_diag + d + 1, d + 1) if d + 1 < n_diag else None
        consume(d % 2, first_diag + d, d, ahead=ahead)

    o = acc_ref[...] / l_ref[...]
    a = o[:, :tq] - lam * o[:, tq:]
    ms = jnp.mean(a * a, axis=0, keepdims=True)
    a = a * lax.rsqrt(ms + NORM_EPS) * subln_ref[...] * (1.0 - LAMBDA_INIT)
    o_ref[0] = a.T.astype(o_ref.dtype)


def _sb_attn_kernel(q_ref, k_ref, vt_ref, o_ref, c_ref, acc_ref):
    tq, cw = SB_TQ, CHAIN_W
    n_chains = tq // cw
    i = pl.program_id(2)
    qt = q_ref[...]
    q_chain = [qt[:, c * cw:(c + 1) * cw] for c in range(n_chains)]

    krow = lax.broadcasted_iota(jnp.int32, (cw, cw), 0)
    qcol = lax.broadcasted_iota(jnp.int32, (cw, cw), 1)
    strict = krow < qcol
    lrow = lax.broadcasted_iota(jnp.int32, (cw + SUM_ROWS, cw), 0)
    lcol = lax.broadcasted_iota(jnp.int32, (cw + SUM_ROWS, cw), 1)
    later = jnp.where(jnp.logical_or(lcol > lrow, lrow >= cw), 1.0, 0.0).astype(jnp.bfloat16)

    c_ref[...] = jnp.zeros_like(c_ref)
    acc_ref[...] = jnp.zeros_like(acc_ref)

    def run_pieces(pieces):
        def scores(t, st):
            sub, c, _ = pieces[t]
            start = pl.multiple_of(sub * cw, cw)
            st["z"] = jnp.dot(k_ref[0, pl.ds(start, cw), :], q_chain[c],
                              preferred_element_type=jnp.float32)

        def suffix(t, st):
            _, _, triangular = pieces[t]
            z = st.pop("z")
            u = jnp.maximum(z, 0.0) + jnp.log2(1.0 + jnp.exp2(-jnp.abs(z)))
            st["log_sig"] = z - u
            if triangular:
                u = jnp.where(strict, u, 0.0)
            st["tail"] = jnp.dot(later, u.astype(jnp.bfloat16),
                                 preferred_element_type=jnp.float32)

        def weights_pv(t, st):
            sub, _, triangular = pieces[t]
            start = pl.multiple_of(sub * cw, cw)
            tail = st.pop("tail")
            st["usum"] = tail[cw:cw + 1]
            a = jnp.exp2(st.pop("log_sig") - tail[:cw])
            if triangular:
                a = jnp.where(strict, a, 0.0)
            st["pv"] = jnp.dot(vt_ref[:, pl.ds(start, cw)], a.astype(jnp.bfloat16),
                               preferred_element_type=jnp.float32)

        def accumulate(t, st):
            _, c, _ = pieces[t]
            lanes = slice(c * cw, (c + 1) * cw)
            carry = c_ref[:, lanes]
            acc_ref[:, lanes] += st.pop("pv") * jnp.exp2(-carry)
            c_ref[:, lanes] = carry + st.pop("usum")

        _emit_pipelined((scores, suffix, weights_pv, accumulate), len(pieces))

    diag = [n_chains * i + c for c in range(n_chains)]
    head = [(diag[c], c, True) for c in reversed(range(n_chains))]
    second = [(diag[c] - 1, c, False) for c in reversed(range(n_chains))]

    @pl.when(i == 0)
    def _():
        run_pieces(head + [p for p in second if p[1] > 0])

    @pl.when(i > 0)
    def _():
        run_pieces(head + second)

    def unfinished(c, depth):
        lanes = slice(c * cw, (c + 1) * cw)
        return jnp.logical_and(diag[c] - depth >= 0,
                               jnp.min(c_ref[:, lanes]) < SB_DONE_LOG2)

    def any_unfinished(depth):
        go = unfinished(0, depth)
        for c in range(1, n_chains):
            go = jnp.logical_or(go, unfinished(c, depth))
        return go

    def body(carry):
        depth, _ = carry
        for c in range(n_chains):
            @pl.when(unfinished(c, depth))
            def _():
                run_pieces([(diag[c] - depth, c, False)])
        return depth + 1, any_unfinished(depth + 1)

    lax.while_loop(lambda carry: carry[1], body, (jnp.int32(2), any_unfinished(2)))
    o_ref[0] = acc_ref[...].T.astype(o_ref.dtype)


def _mix_ffn2_kernel(h_ref, a_ref, b_ref, gate_ref, wa_hbm, wb_hbm, wout_hbm,
                     norm2_ref, wg_hbm, wu_hbm, wd_hbm, normf_ref, o_ref,
                     wa_ref, wb_ref, wout_ref, wg_ref, wu_ref, wd_ref,
                     stage_wide, stage_tall, sem):
    @pl.when(pl.program_id(0) == 0)
    def _():
        for src, dst in ((wa_hbm, wa_ref), (wb_hbm, wb_ref), (wout_hbm, wout_ref)):
            _stage_weight(src, dst, stage_tall, sem, rows=STAGE_SQUARE_ROWS)
        _stage_ffn_weights(wg_hbm, wu_hbm, wd_hbm, wg_ref, wu_ref, wd_ref,
                           stage_wide, stage_tall, sem)

    half = h_ref.shape[0] // 2
    rows = [slice(0, half), slice(half, 2 * half)]
    yab = [(jnp.dot(a_ref[r, :], wa_ref[...], preferred_element_type=jnp.float32),
            jnp.dot(b_ref[r, :], wb_ref[...], preferred_element_type=jnp.float32)) for r in rows]
    h2 = []
    for r, (ya, yb) in zip(rows, yab):
        gate = gate_ref[r, :].astype(jnp.float32)
        y = (gate[:, :D_MODEL] * ya + gate[:, D_MODEL:] * yb).astype(jnp.bfloat16)
        h2.append(h_ref[r, :] + jnp.dot(y, wout_ref[...], preferred_element_type=jnp.float32))
    h2 = jnp.concatenate(h2, axis=0)
    h3 = _swiglu_half_step(h2, norm2_ref[...], wg_ref, wu_ref, wd_ref)
    o_ref[...] = _rms(h3, normf_ref[...])


def _const_spec(shape):
    return pl.BlockSpec(shape, lambda *_: (0,) * len(shape), pipeline_mode=pl.Buffered(1))


def _tc_params(n_axes):
    return pltpu.CompilerParams(dimension_semantics=("arbitrary",) * n_axes,
                                vmem_limit_bytes=VMEM_LIMIT_BYTES)


def kernel(x, ffn1_norm, ffn1_w_gate, ffn1_w_up, ffn1_w_down, mix_norm, w_in, b_gate, lambda_q1, lambda_k1, lambda_q2, lambda_k2, diff_subln, w_branch_diff, w_branch_sb, w_out, ffn2_norm, ffn2_w_gate, ffn2_w_up, ffn2_w_down, final_norm):
    B, S, D = x.shape
    T = B * S
    f32, bf16 = jnp.float32, jnp.bfloat16
    xt = x.reshape(T, D)
    row = lambda v: v.reshape(1, -1).astype(f32)

    tok_spec = pl.BlockSpec((FFN_TM, D), lambda t: (t, 0))
    hbm_spec = pl.BlockSpec(memory_space=pl.ANY)
    ffn_weight_scratch = [pltpu.VMEM((D, D_FF), bf16), pltpu.VMEM((D, D_FF), bf16),
                          pltpu.VMEM((D_FF, D), bf16)]
    ffn_stage_scratch = [pltpu.VMEM((STAGE_SLOTS, STAGE_WIDE_ROWS, D_FF), f32),
                         pltpu.VMEM((STAGE_SLOTS, STAGE_TALL_ROWS, D), f32),
                         pltpu.SemaphoreType.DMA((STAGE_SLOTS,))]
    h1 = pl.pallas_call(
        _ffn1_kernel,
        grid=(T // FFN_TM,),
        in_specs=[tok_spec, _const_spec((1, D)), hbm_spec, hbm_spec, hbm_spec],
        out_specs=tok_spec,
        out_shape=jax.ShapeDtypeStruct((T, D), f32),
        scratch_shapes=ffn_weight_scratch + ffn_stage_scratch,
        compiler_params=_tc_params(1),
        name="ffn1",
    )(xt, row(ffn1_norm[0]), ffn1_w_gate[0], ffn1_w_up[0], ffn1_w_down[0])

    rowscale = jnp.ones((QVT_W,), f32)
    rowscale = rowscale.at[0:ATT_W].set(DA_QK_DIM ** -0.5 * LOG2E)
    rowscale = rowscale.at[ATT_W:2 * ATT_W].set(HEAD_W ** -0.5 * LOG2E)
    k, qvt, gates = pl.pallas_call(
        _in_proj_kernel,
        grid=(T // PROJ_TM,),
        in_specs=[pl.BlockSpec((PROJ_TM, D), lambda t: (t, 0)), _const_spec((1, D)), hbm_spec,
                  _const_spec((QVT_W, 1)), _const_spec((1, GATE_W))],
        out_specs=[pl.BlockSpec((PROJ_TM, K_W), lambda t: (t, 0)),
                   pl.BlockSpec((QVT_W, PROJ_TM), lambda t: (0, t)),
                   pl.BlockSpec((PROJ_TM, GATE_W), lambda t: (t, 0))],
        out_shape=[jax.ShapeDtypeStruct((T, K_W), bf16),
                   jax.ShapeDtypeStruct((QVT_W, T), bf16),
                   jax.ShapeDtypeStruct((T, GATE_W), bf16)],
        scratch_shapes=[pltpu.VMEM((D, K_W), bf16), pltpu.VMEM((QVT_W, D), bf16),
                        pltpu.VMEM((D, GATE_W), bf16),
                        pltpu.VMEM((STAGE_SLOTS, STAGE_IN_ROWS, GATE_W), f32),
                        pltpu.SemaphoreType.DMA((STAGE_SLOTS,))],
        compiler_params=_tc_params(1),
        name="in_proj",
    )(h1, row(mix_norm[0]), w_in[0], rowscale.reshape(-1, 1), row(b_gate[0]))
    k3 = k.reshape(B, S, K_W)

    def q_spec(slab, tq):
        return pl.BlockSpec((HEAD_W, tq),
                            lambda b, h, i: (slab * N_HEADS + h, b * (S // tq) + i))

    def k_spec(slab):
        return pl.BlockSpec((1, S, HEAD_W), lambda b, h, i: (b, 0, slab * N_HEADS + h))

    def vt_spec(slab):
        return pl.BlockSpec((HEAD_W, S), lambda b, h, i: (slab * N_HEADS + h, b))

    def att_out_spec(tq):
        return pl.BlockSpec((1, tq, HEAD_W), lambda b, h, i: (b, i, h))

    att_out_shape = jax.ShapeDtypeStruct((B, S, ATT_W), bf16)
    smem_spec = pl.BlockSpec(memory_space=pltpu.SMEM)

    lam = (jnp.exp(jnp.sum(lambda_q1[0].astype(f32) * lambda_k1[0].astype(f32)))
           - jnp.exp(jnp.sum(lambda_q2[0].astype(f32) * lambda_k2[0].astype(f32)))
           + LAMBDA_INIT).reshape(1)
    slopes = jnp.exp2(-8.0 * jnp.arange(1, N_HEADS + 1, dtype=f32) / N_HEADS)

    a = pl.pallas_call(
        _diff_attn_kernel,
        grid=(B, N_HEADS, S // ATT_TQ),
        in_specs=[smem_spec, smem_spec, q_spec(0, ATT_TQ), k_spec(0), vt_spec(2),
                  _const_spec((HEAD_W, 1))],
        out_specs=att_out_spec(ATT_TQ),
        out_shape=att_out_shape,
        scratch_shapes=[pltpu.VMEM((ATT_TK, ATT_TK), f32), pltpu.VMEM((ATT_TK, HEAD_W), bf16),
                        pltpu.VMEM((2, 2 * ATT_TQ // CHAIN_W, ATT_TK, CHAIN_W), f32),
                        pltpu.VMEM((1, 2 * ATT_TQ), f32), pltpu.VMEM((1, 2 * ATT_TQ), f32),
                        pltpu.VMEM((HEAD_W, 2 * ATT_TQ), f32)],
        compiler_params=_tc_params(3),
        name="diff_attn",
    )(slopes, lam, qvt, k3, qvt, diff_subln[0].reshape(-1, 1).astype(f32))

    b = pl.pallas_call(
        _sb_attn_kernel,
        grid=(B, N_HEADS, S // SB_TQ),
        in_specs=[q_spec(1, SB_TQ), k_spec(1), vt_spec(3)],
        out_specs=att_out_spec(SB_TQ),
        out_shape=att_out_shape,
        scratch_shapes=[pltpu.VMEM((1, SB_TQ), f32), pltpu.VMEM((HEAD_W, SB_TQ), f32)],
        compiler_params=_tc_params(3),
        name="sb_attn",
    )(qvt, k3, qvt)

    out = pl.pallas_call(
        _mix_ffn2_kernel,
        grid=(T // FFN_TM,),
        in_specs=[tok_spec,
                  pl.BlockSpec((FFN_TM, ATT_W), lambda t: (t, 0)),
                  pl.BlockSpec((FFN_TM, ATT_W), lambda t: (t, 0)),
                  pl.BlockSpec((FFN_TM, GATE_W), lambda t: (t, 0)),
                  hbm_spec, hbm_spec, hbm_spec, _const_spec((1, D)),
                  hbm_spec, hbm_spec, hbm_spec, _const_spec((1, D))],
        out_specs=tok_spec,
        out_shape=jax.ShapeDtypeStruct((T, D), f32),
        scratch_shapes=[pltpu.VMEM((ATT_W, D), bf16), pltpu.VMEM((ATT_W, D), bf16),
                        pltpu.VMEM((D, D), bf16)] + ffn_weight_scratch + ffn_stage_scratch,
        compiler_params=_tc_params(1),
        name="mix_ffn2",
    )(h1, a.reshape(T, -1), b.reshape(T, -1), gates,
      w_branch_diff[0], w_branch_sb[0], w_out[0],
      row(ffn2_norm[0]), ffn2_w_gate[0], ffn2_w_up[0], ffn2_w_down[0], row(final_norm))
    return out.reshape(B, S, D)
```

```python
import math

import jax
import jax.numpy as jnp
from jax import lax
from jax.experimental import pallas as pl
from jax.experimental.pallas import tpu as pltpu

D_MODEL = 1024
D_FF = 2816
N_HEADS = 4
HEAD_W = 128
DA_QK_DIM = 64
ATT_W = N_HEADS * HEAD_W
K_W = 2 * ATT_W
QVT_W = 4 * ATT_W
GATE_W = 2 * D_MODEL
NORM_EPS = 1e-5
LAMBDA_INIT = 0.8 - 0.6 * math.exp(-0.3 * 0)
LOG2E = 1.0 / math.log(2.0)

VMEM_LIMIT_BYTES = 56 * 1024 * 1024

STAGE_SLOTS = 3
STAGE_WIDE_ROWS = 128
STAGE_TALL_ROWS = 352
STAGE_SQUARE_ROWS = 256
STAGE_IN_ROWS = 256

FFN_TM = 512
PROJ_TM = 512
ATT_TQ = 2048
ATT_TK = 512
SB_TQ = 4096
CHAIN_W = 256
SB_DONE_LOG2 = 160.0
SUM_ROWS = 16
SLOPE_TERMS = 3
BF16_EXACT_INT = 256

_NT = (((1,), (1,)), ((), ()))


def _rms(x, g):
    ms = jnp.mean(x * x, axis=-1, keepdims=True)
    return x * lax.rsqrt(ms + NORM_EPS) * g


def _swiglu_half_step(x, norm_g, wg_ref, wu_ref, wd_ref):
    halves = jnp.split(x, 2, axis=0)
    xn = [_rms(h, norm_g).astype(jnp.bfloat16) for h in halves]
    gu = [(jnp.dot(n, wg_ref[...], preferred_element_type=jnp.float32),
           jnp.dot(n, wu_ref[...], preferred_element_type=jnp.float32)) for n in xn]
    out = []
    for h, (g, u) in zip(halves, gu):
        hact = (g * jax.nn.sigmoid(g) * u).astype(jnp.bfloat16)
        out.append(h + 0.5 * jnp.dot(hact, wd_ref[...], preferred_element_type=jnp.float32))
    return jnp.concatenate(out, axis=0)


def _stage_weight(src_hbm, dst_ref, stage_ref, sem_ref, *, rows, col0=0, ncols=None,
                  dst_row0=0, dst_col0=0, transpose=False):
    n_rows = src_hbm.shape[0]
    ncols = src_hbm.shape[1] - col0 if ncols is None else ncols
    n_slots = stage_ref.shape[0]
    assert n_rows % rows == 0 and rows <= stage_ref.shape[1] and ncols <= stage_ref.shape[2]
    n_slabs = n_rows // rows

    def slab_copy(c):
        return pltpu.make_async_copy(
            src_hbm.at[pl.ds(c * rows, rows), pl.ds(col0, ncols)],
            stage_ref.at[c % n_slots, pl.ds(0, rows), pl.ds(0, ncols)],
            sem_ref.at[c % n_slots])

    for c in range(min(n_slots - 1, n_slabs)):
        slab_copy(c).start()
    for c in range(n_slabs):
        if c + n_slots - 1 < n_slabs:
            slab_copy(c + n_slots - 1).start()
        slab_copy(c).wait()
        slab = stage_ref[c % n_slots, :rows, :ncols]
        if transpose:
            dst_ref[dst_row0:dst_row0 + ncols,
                    dst_col0 + c * rows:dst_col0 + (c + 1) * rows] = slab.T.astype(dst_ref.dtype)
        else:
            dst_ref[dst_row0 + c * rows:dst_row0 + (c + 1) * rows,
                    dst_col0:dst_col0 + ncols] = slab.astype(dst_ref.dtype)


def _stage_ffn_weights(wg_hbm, wu_hbm, wd_hbm, wg_ref, wu_ref, wd_ref, stage_wide, stage_tall, sem):
    _stage_weight(wg_hbm, wg_ref, stage_wide, sem, rows=STAGE_WIDE_ROWS)
    _stage_weight(wu_hbm, wu_ref, stage_wide, sem, rows=STAGE_WIDE_ROWS)
    _stage_weight(wd_hbm, wd_ref, stage_tall, sem, rows=STAGE_TALL_ROWS)


def _ffn1_kernel(x_ref, norm_ref, wg_hbm, wu_hbm, wd_hbm, o_ref,
                 wg_ref, wu_ref, wd_ref, stage_wide, stage_tall, sem):
    @pl.when(pl.program_id(0) == 0)
    def _():
        _stage_ffn_weights(wg_hbm, wu_hbm, wd_hbm, wg_ref, wu_ref, wd_ref,
                           stage_wide, stage_tall, sem)

    o_ref[...] = _swiglu_half_step(x_ref[...], norm_ref[...], wg_ref, wu_ref, wd_ref)


def _in_proj_kernel(h_ref, norm_ref, win_hbm, rowscale_ref, bgate_ref,
                    k_ref, qvt_ref, gate_ref, wk_ref, wqvt_ref, wgate_ref, stage, sem):
    @pl.when(pl.program_id(0) == 0)
    def _():
        w = ATT_W
        for piece, src_block in enumerate((1, 4)):
            _stage_weight(win_hbm, wk_ref, stage, sem, rows=STAGE_IN_ROWS,
                          col0=src_block * w, ncols=w, dst_col0=piece * w)
        _stage_weight(win_hbm, wgate_ref, stage, sem, rows=STAGE_IN_ROWS, col0=6 * w, ncols=GATE_W)
        for piece, src_block in enumerate((0, 3, 2, 5)):
            _stage_weight(win_hbm, wqvt_ref, stage, sem, rows=STAGE_IN_ROWS,
                          col0=src_block * w, ncols=w, dst_row0=piece * w, transpose=True)

    half = h_ref.shape[0] // 2
    rows = [slice(0, half), slice(half, 2 * half)]
    n = [_rms(h_ref[r, :], norm_ref[...]).astype(jnp.bfloat16) for r in rows]
    g = [jnp.dot(nh, wgate_ref[...], preferred_element_type=jnp.float32) for nh in n]
    for r, gh in zip(rows, g):
        gate_ref[r, :] = jax.nn.sigmoid(gh + bgate_ref[...]).astype(jnp.bfloat16)
    for r, nh in zip(rows, n):
        k_ref[r, :] = jnp.dot(nh, wk_ref[...],
                              preferred_element_type=jnp.float32).astype(jnp.bfloat16)
    for r, nh in zip(rows, n):
        qvt = lax.dot_general(wqvt_ref[...], nh, _NT, preferred_element_type=jnp.float32)
        qvt_ref[:, r] = (qvt * rowscale_ref[...]).astype(jnp.bfloat16)


def _emit_pipelined(stages, n):
    state = [dict() for _ in range(n)]
    for step in range(n + len(stages) - 1):
        for s, stage in enumerate(stages):
            t = step - s
            if 0 <= t < n:
                stage(t, state[t])


def _diff_attn_kernel(slope_ref, lam_ref, q_ref, k_ref, vt_ref, subln_ref, *refs):
    tq, tk, cw = ATT_TQ, ATT_TK, CHAIN_W
    per_map = tq // cw
    n_chains = 2 * per_map
    n_diag = tq // tk
    assert n_diag % 2 == 0 and tk == 2 * cw and tk <= 2 * BF16_EXACT_INT
    h = pl.program_id(1)
    i = pl.program_id(2)
    slope = slope_ref[h] * LOG2E
    lam = lam_ref[0]

    n_side = (len(refs) - 7) // 2
    w32_refs, o_ref, w16_refs = refs[:n_side], refs[n_side], refs[n_side + 1:2 * n_side + 1]
    mask_ref, kfeat_ref, s_ref, m_ref, l_ref, acc_ref = refs[2 * n_side + 1:]
    for w32, w16 in zip(w32_refs, w16_refs):
        w16[...] = w32[...].astype(w16.dtype)

    @pl.when(jnp.logical_and(pl.program_id(0) == 0, jnp.logical_and(h == 0, i == 0)))
    def _():
        krow = lax.broadcasted_iota(jnp.int32, (tk, tk), 0)
        qcol = lax.broadcasted_iota(jnp.int32, (tk, tk), 1)
        mask_ref[...] = jnp.where(qcol >= krow, 0.0, -jnp.inf)
        kpos = lax.broadcasted_iota(jnp.int32, (tk, HEAD_W), 0)
        klane = lax.broadcasted_iota(jnp.int32, (tk, HEAD_W), 1)
        k_hi = jnp.where(kpos >= BF16_EXACT_INT, BF16_EXACT_INT, 0)
        kfeat_ref[...] = jnp.where(klane < SLOPE_TERMS, k_hi,
                                   jnp.where(klane < 2 * SLOPE_TERMS, kpos - k_hi, 0)
                                   ).astype(jnp.float32).astype(jnp.bfloat16)

    qt = q_ref[...]
    chan = lax.broadcasted_iota(jnp.int32, (HEAD_W, tq), 0)
    zero = jnp.zeros_like(qt)
    q_maps = (jnp.where(chan < DA_QK_DIM, qt, zero), jnp.where(chan >= DA_QK_DIM, qt, zero))

    sl = jnp.full((HEAD_W, cw), slope, jnp.float32)
    hi = sl.astype(jnp.bfloat16).astype(jnp.float32)
    mid = (sl - hi).astype(jnp.bfloat16).astype(jnp.float32)
    lo = sl - hi - mid
    frow = lax.broadcasted_iota(jnp.int32, (HEAD_W, cw), 0)
    part = frow % SLOPE_TERMS
    q_feat = jnp.where(frow < 2 * SLOPE_TERMS,
                       jnp.where(part == 0, hi, jnp.where(part == 1, mid, lo)),
                       0.0).astype(jnp.bfloat16)
    q_chain = [jnp.concatenate(
        [q_maps[c // per_map][:, (c % per_map) * cw:(c % per_map + 1) * cw], q_feat], axis=0)
        for c in range(n_chains)]

    m_ref[...] = jnp.full_like(m_ref, -jnp.inf)
    l_ref[...] = jnp.zeros_like(l_ref)
    acc_ref[...] = jnp.zeros_like(acc_ref)

    def chain_mode(c, d):
        if d is None:
            return "full"
        q_lo = (c % per_map) * cw
        if q_lo + cw <= d * tk:
            return "skip"
        if q_lo >= (d + 1) * tk:
            return "full"
        return q_lo - d * tk

    def visible_keys(c, d):
        return cw if chain_mode(c, d) == 0 else tk

    def score_stage(slot, j, d=None):
        chains = [c for c in range(n_chains) if chain_mode(c, d) != "skip"]
        kb = k_ref[0, pl.ds(pl.multiple_of(j * tk, tk), tk), :]
        kb = jnp.concatenate([kb, kfeat_ref[...]], axis=1)

        def scores(t, st):
            if t < len(chains):
                c = chains[t]
                nk = visible_keys(c, d)
                s_ref[slot, c, :nk] = jnp.dot(kb[:nk], q_chain[c],
                                              preferred_element_type=jnp.float32)

        return len(chains), scores

    def scores_to(slot, j, d=None):
        n, scores = score_stage(slot, j, d)
        for t in range(n):
            scores(t, None)

    def consume(slot, j, d=None, ahead=None):
        chains = [c for c in range(n_chains) if chain_mode(c, d) != "skip"]
        vtb = vt_ref[:, pl.ds(pl.multiple_of(j * tk, tk), tk)]
        vtb = jnp.concatenate([vtb, jnp.ones((SUM_ROWS, tk), vtb.dtype)], axis=0)
        shift = -slope * (i * tq - j * tk).astype(jnp.float32)

        def column_max(t, st):
            if t >= len(chains):
                return
            c = chains[t]
            nk = visible_keys(c, d)
            s = s_ref[slot, c, :nk]
            mode = chain_mode(c, d)
            if mode != "full":
                s = s + mask_ref[:nk, mode:mode + cw]
                s_ref[slot, c, :nk] = s
            st["cmax"] = jnp.max(s, axis=0, keepdims=True) + shift

        def softmax_pv(t, st):
            if t >= len(chains):
                return
            c = chains[t]
            nk = visible_keys(c, d)
            lanes = slice(c * cw, (c + 1) * cw)
            m_prev = m_ref[:, lanes]
            m_new = jnp.maximum(m_prev, st.pop("cmax"))
            st["alpha"] = jnp.exp2(m_prev - m_new)
            p = jnp.exp2(s_ref[slot, c, :nk] - (m_new - shift))
            m_ref[:, lanes] = m_new
            st["pv"] = jnp.dot(vtb[:, :nk], p.astype(jnp.bfloat16),
                               preferred_element_type=jnp.float32)

        def accumulate(t, st):
            if t >= len(chains):
                return
            c = chains[t]
            lanes = slice(c * cw, (c + 1) * cw)
            alpha, pv = st.pop("alpha"), st.pop("pv")
            acc_ref[:, lanes] = alpha * acc_ref[:, lanes] + pv[:HEAD_W]
            l_ref[:, lanes] = alpha * l_ref[:, lanes] + pv[HEAD_W:HEAD_W + 1]

        n_ahead, ahead_stage = ahead if ahead is not None else (0, None)
        stages = (column_max, softmax_pv, accumulate)
        if ahead is not None:
            stages = (ahead_stage,) + stages
        _emit_pipelined(stages, max(len(chains), n_ahead))

    def step(slot, j):
        consume(slot, j, ahead=score_stage(1 - slot, j + 1))

    scores_to(0, 0)

    def pair(jj, carry):
        step(0, 2 * jj)
        step(1, 2 * jj + 1)
        return carry

    first_diag = n_diag * i
    lax.fori_loop(0, first_diag // 2, pair, 0)
    for d in range(n_diag):
        ahead = score_stage((d + 1) % 2, first_diag + d + 1, d + 1) if d + 1 < n_diag else None
        consume(d % 2, first_diag + d, d, ahead=ahead)

    o = acc_ref[...] / l_ref[...]
    a = o[:, :tq] - lam * o[:, tq:]
    ms = jnp.mean(a * a, axis=0, keepdims=True)
    a = a * lax.rsqrt(ms + NORM_EPS) * subln_ref[...] * (1.0 - LAMBDA_INIT)
    o_ref[0] = a.T.astype(o_ref.dtype)


def _sb_attn_kernel(q_ref, k_ref, vt_ref, o_ref, c_ref, acc_ref):
    tq, cw = SB_TQ, CHAIN_W
    n_chains = tq // cw
    i = pl.program_id(2)
    qt = q_ref[...]
    q_chain = [qt[:, c * cw:(c + 1) * cw] for c in range(n_chains)]

    krow = lax.broadcasted_iota(jnp.int32, (cw, cw), 0)
    qcol = lax.broadcasted_iota(jnp.int32, (cw, cw), 1)
    strict = krow < qcol
    lrow = lax.broadcasted_iota(jnp.int32, (cw + SUM_ROWS, cw), 0)
    lcol = lax.broadcasted_iota(jnp.int32, (cw + SUM_ROWS, cw), 1)
    later = jnp.where(jnp.logical_or(lcol > lrow, lrow >= cw), 1.0, 0.0).astype(jnp.bfloat16)

    c_ref[...] = jnp.zeros_like(c_ref)
    acc_ref[...] = jnp.zeros_like(acc_ref)

    def run_pieces(pieces):
        def scores(t, st):
            sub, c, _ = pieces[t]
            start = pl.multiple_of(sub * cw, cw)
            st["z"] = jnp.dot(k_ref[0, pl.ds(start, cw), :], q_chain[c],
                              preferred_element_type=jnp.float32)

        def suffix(t, st):
            _, _, triangular = pieces[t]
            z = st.pop("z")
            u = jnp.maximum(z, 0.0) + jnp.log2(1.0 + jnp.exp2(-jnp.abs(z)))
            st["log_sig"] = z - u
            if triangular:
                u = jnp.where(strict, u, 0.0)
            st["tail"] = jnp.dot(later, u.astype(jnp.bfloat16),
                                 preferred_element_type=jnp.float32)

        def weights_pv(t, st):
            sub, _, triangular = pieces[t]
            start = pl.multiple_of(sub * cw, cw)
            tail = st.pop("tail")
            st["usum"] = tail[cw:cw + 1]
            a = jnp.exp2(st.pop("log_sig") - tail[:cw])
            if triangular:
                a = jnp.where(strict, a, 0.0)
            st["pv"] = jnp.dot(vt_ref[:, pl.ds(start, cw)], a.astype(jnp.bfloat16),
                               preferred_element_type=jnp.float32)

        def accumulate(t, st):
            _, c, _ = pieces[t]
            lanes = slice(c * cw, (c + 1) * cw)
            carry = c_ref[:, lanes]
            acc_ref[:, lanes] += st.pop("pv") * jnp.exp2(-carry)
            c_ref[:, lanes] = carry + st.pop("usum")

        _emit_pipelined((scores, suffix, weights_pv, accumulate), len(pieces))

    diag = [n_chains * i + c for c in range(n_chains)]
    head = [(diag[c], c, True) for c in reversed(range(n_chains))]
    second = [(diag[c] - 1, c, False) for c in reversed(range(n_chains))]

    @pl.when(i == 0)
    def _():
        run_pieces(head + [p for p in second if p[1] > 0])

    @pl.when(i > 0)
    def _():
        run_pieces(head + second)

    def unfinished(c, depth):
        lanes = slice(c * cw, (c + 1) * cw)
        return jnp.logical_and(diag[c] - depth >= 0,
                               jnp.min(c_ref[:, lanes]) < SB_DONE_LOG2)

    def any_unfinished(depth):
        go = unfinished(0, depth)
        for c in range(1, n_chains):
            go = jnp.logical_or(go, unfinished(c, depth))
        return go

    def body(carry):
        depth, _ = carry
        for c in range(n_chains):
            @pl.when(unfinished(c, depth))
            def _():
                run_pieces([(diag[c] - depth, c, False)])
        return depth + 1, any_unfinished(depth + 1)

    lax.while_loop(lambda carry: carry[1], body, (jnp.int32(2), any_unfinished(2)))
    o_ref[0] = acc_ref[...].T.astype(o_ref.dtype)


def _mix_ffn2_kernel(h_ref, a_ref, b_ref, gate_ref, wa_ref, wb_ref, wout_ref,
                     norm2_ref, wg_ref, wu_ref, wd_ref, normf_ref, o_ref):
    half = h_ref.shape[0] // 2
    rows = [slice(0, half), slice(half, 2 * half)]
    yab = [(jnp.dot(a_ref[r, :], wa_ref[...], preferred_element_type=jnp.float32),
            jnp.dot(b_ref[r, :], wb_ref[...], preferred_element_type=jnp.float32)) for r in rows]
    h2 = []
    for r, (ya, yb) in zip(rows, yab):
        gate = gate_ref[r, :].astype(jnp.float32)
        y = (gate[:, :D_MODEL] * ya + gate[:, D_MODEL:] * yb).astype(jnp.bfloat16)
        h2.append(h_ref[r, :] + jnp.dot(y, wout_ref[...], preferred_element_type=jnp.float32))
    h2 = jnp.concatenate(h2, axis=0)
    h3 = _swiglu_half_step(h2, norm2_ref[...], wg_ref, wu_ref, wd_ref)
    o_ref[...] = _rms(h3, normf_ref[...])


def _const_spec(shape):
    return pl.BlockSpec(shape, lambda *_: (0,) * len(shape), pipeline_mode=pl.Buffered(1))


def _tc_params(n_axes):
    return pltpu.CompilerParams(dimension_semantics=("arbitrary",) * n_axes,
                                vmem_limit_bytes=VMEM_LIMIT_BYTES)


def kernel(x, ffn1_norm, ffn1_w_gate, ffn1_w_up, ffn1_w_down, mix_norm, w_in, b_gate, lambda_q1, lambda_k1, lambda_q2, lambda_k2, diff_subln, w_branch_diff, w_branch_sb, w_out, ffn2_norm, ffn2_w_gate, ffn2_w_up, ffn2_w_down, final_norm):
    B, S, D = x.shape
    T = B * S
    f32, bf16 = jnp.float32, jnp.bfloat16
    xt = x.reshape(T, D)
    row = lambda v: v.reshape(1, -1).astype(f32)

    tok_spec = pl.BlockSpec((FFN_TM, D), lambda t: (t, 0))
    hbm_spec = pl.BlockSpec(memory_space=pl.ANY)
    ffn_weight_scratch = [pltpu.VMEM((D, D_FF), bf16), pltpu.VMEM((D, D_FF), bf16),
                          pltpu.VMEM((D_FF, D), bf16)]
    ffn_stage_scratch = [pltpu.VMEM((STAGE_SLOTS, STAGE_WIDE_ROWS, D_FF), f32),
                         pltpu.VMEM((STAGE_SLOTS, STAGE_TALL_ROWS, D), f32),
                         pltpu.SemaphoreType.DMA((STAGE_SLOTS,))]
    h1 = pl.pallas_call(
        _ffn1_kernel,
        grid=(T // FFN_TM,),
        in_specs=[tok_spec, _const_spec((1, D)), hbm_spec, hbm_spec, hbm_spec],
        out_specs=tok_spec,
        out_shape=jax.ShapeDtypeStruct((T, D), f32),
        scratch_shapes=ffn_weight_scratch + ffn_stage_scratch,
        compiler_params=_tc_params(1),
        name="ffn1",
    )(xt, row(ffn1_norm[0]), ffn1_w_gate[0], ffn1_w_up[0], ffn1_w_down[0])

    rowscale = jnp.ones((QVT_W,), f32)
    rowscale = rowscale.at[0:ATT_W].set(DA_QK_DIM ** -0.5 * LOG2E)
    rowscale = rowscale.at[ATT_W:2 * ATT_W].set(HEAD_W ** -0.5 * LOG2E)
    k, qvt, gates = pl.pallas_call(
        _in_proj_kernel,
        grid=(T // PROJ_TM,),
        in_specs=[pl.BlockSpec((PROJ_TM, D), lambda t: (t, 0)), _const_spec((1, D)), hbm_spec,
                  _const_spec((QVT_W, 1)), _const_spec((1, GATE_W))],
        out_specs=[pl.BlockSpec((PROJ_TM, K_W), lambda t: (t, 0)),
                   pl.BlockSpec((QVT_W, PROJ_TM), lambda t: (0, t)),
                   pl.BlockSpec((PROJ_TM, GATE_W), lambda t: (t, 0))],
        out_shape=[jax.ShapeDtypeStruct((T, K_W), bf16),
                   jax.ShapeDtypeStruct((QVT_W, T), bf16),
                   jax.ShapeDtypeStruct((T, GATE_W), bf16)],
        scratch_shapes=[pltpu.VMEM((D, K_W), bf16), pltpu.VMEM((QVT_W, D), bf16),
                        pltpu.VMEM((D, GATE_W), bf16),
                        pltpu.VMEM((STAGE_SLOTS, STAGE_IN_ROWS, GATE_W), f32),
                        pltpu.SemaphoreType.DMA((STAGE_SLOTS,))],
        compiler_params=_tc_params(1),
        name="in_proj",
    )(h1, row(mix_norm[0]), w_in[0], rowscale.reshape(-1, 1), row(b_gate[0]))
    k3 = k.reshape(B, S, K_W)

    def q_spec(slab, tq):
        return pl.BlockSpec((HEAD_W, tq),
                            lambda b, h, i: (slab * N_HEADS + h, b * (S // tq) + i))

    def k_spec(slab):
        return pl.BlockSpec((1, S, HEAD_W), lambda b, h, i: (b, 0, slab * N_HEADS + h))

    def vt_spec(slab):
        return pl.BlockSpec((HEAD_W, S), lambda b, h, i: (slab * N_HEADS + h, b))

    def att_out_spec(tq):
        return pl.BlockSpec((1, tq, HEAD_W), lambda b, h, i: (b, i, h))

    att_out_shape = jax.ShapeDtypeStruct((B, S, ATT_W), bf16)
    smem_spec = pl.BlockSpec(memory_space=pltpu.SMEM)

    lam = (jnp.exp(jnp.sum(lambda_q1[0].astype(f32) * lambda_k1[0].astype(f32)))
           - jnp.exp(jnp.sum(lambda_q2[0].astype(f32) * lambda_k2[0].astype(f32)))
           + LAMBDA_INIT).reshape(1)
    slopes = jnp.exp2(-8.0 * jnp.arange(1, N_HEADS + 1, dtype=f32) / N_HEADS)

    mix_weights = [w_branch_diff[0], w_branch_sb[0], w_out[0],
                   ffn2_w_gate[0], ffn2_w_up[0], ffn2_w_down[0]]
    nq_diff = S // ATT_TQ
    n_steps = B * N_HEADS * nq_diff

    def row_block_spec(w):
        rows, cols = w.shape
        n_blocks = max(n for n in range(1, n_steps + 1)
                       if n_steps % n == 0 and rows % (16 * n) == 0)
        return pl.BlockSpec(
            (rows // n_blocks, cols),
            lambda b, h, i: ((((b * N_HEADS + h) * nq_diff + i) * n_blocks) // n_steps, 0))

    side_specs = [row_block_spec(w) for w in mix_weights]
    a, *mix_weights_bf16 = pl.pallas_call(
        _diff_attn_kernel,
        grid=(B, N_HEADS, nq_diff),
        in_specs=[smem_spec, smem_spec, q_spec(0, ATT_TQ), k_spec(0), vt_spec(2),
                  _const_spec((HEAD_W, 1))] + side_specs,
        out_specs=[att_out_spec(ATT_TQ)] + side_specs,
        out_shape=[att_out_shape] + [jax.ShapeDtypeStruct(w.shape, bf16) for w in mix_weights],
        scratch_shapes=[pltpu.VMEM((ATT_TK, ATT_TK), f32), pltpu.VMEM((ATT_TK, HEAD_W), bf16),
                        pltpu.VMEM((2, 2 * ATT_TQ // CHAIN_W, ATT_TK, CHAIN_W), f32),
                        pltpu.VMEM((1, 2 * ATT_TQ), f32), pltpu.VMEM((1, 2 * ATT_TQ), f32),
                        pltpu.VMEM((HEAD_W, 2 * ATT_TQ), f32)],
        compiler_params=_tc_params(3),
        name="diff_attn",
    )(slopes, lam, qvt, k3, qvt, diff_subln[0].reshape(-1, 1).astype(f32), *mix_weights)

    b = pl.pallas_call(
        _sb_attn_kernel,
        grid=(B, N_HEADS, S // SB_TQ),
        in_specs=[q_spec(1, SB_TQ), k_spec(1), vt_spec(3)],
        out_specs=att_out_spec(SB_TQ),
        out_shape=att_out_shape,
        scratch_shapes=[pltpu.VMEM((1, SB_TQ), f32), pltpu.VMEM((HEAD_W, SB_TQ), f32)],
        compiler_params=_tc_params(3),
        name="sb_attn",
    )(qvt, k3, qvt)

    out = pl.pallas_call(
        _mix_ffn2_kernel,
        grid=(T // FFN_TM,),
        in_specs=[tok_spec,
                  pl.BlockSpec((FFN_TM, ATT_W), lambda t: (t, 0)),
                  pl.BlockSpec((FFN_TM, ATT_W), lambda t: (t, 0)),
                  pl.BlockSpec((FFN_TM, GATE_W), lambda t: (t, 0)),
                  _const_spec((ATT_W, D)), _const_spec((ATT_W, D)),
                  _const_spec((D, D)), _const_spec((1, D)), _const_spec((D, D_FF)),
                  _const_spec((D, D_FF)), _const_spec((D_FF, D)), _const_spec((1, D))],
        out_specs=tok_spec,
        out_shape=jax.ShapeDtypeStruct((T, D), f32),
        compiler_params=_tc_params(1),
        name="mix_ffn2",
    )(h1, a.reshape(T, -1), b.reshape(T, -1), gates, *mix_weights_bf16[:3],
      row(ffn2_norm[0]), *mix_weights_bf16[3:], row(final_norm))
    return out.reshape(B, S, D)
```

```python
import math

import jax
import jax.numpy as jnp
from jax import lax
from jax.experimental import pallas as pl
from jax.experimental.pallas import tpu as pltpu

D_MODEL = 1024
D_FF = 2816
N_HEADS = 4
HEAD_W = 128
DA_QK_DIM = 64
ATT_W = N_HEADS * HEAD_W
K_W = 2 * ATT_W
QVT_W = 4 * ATT_W
GATE_W = 2 * D_MODEL
NORM_EPS = 1e-5
LAMBDA_INIT = 0.8 - 0.6 * math.exp(-0.3 * 0)
LOG2E = 1.0 / math.log(2.0)

VMEM_LIMIT_BYTES = 56 * 1024 * 1024

STAGE_SLOTS = 3
STAGE_WIDE_ROWS = 128
STAGE_TALL_ROWS = 352

FFN_TM = 512
PROJ_TM = 512
ATT_TQ = 2048
ATT_TK = 512
SB_TQ = 4096
CHAIN_W = 256
SB_DONE_LOG2 = 160.0
SUM_ROWS = 16
SLOPE_TERMS = 3
BF16_EXACT_INT = 256

_NT = (((1,), (1,)), ((), ()))


def _rms(x, g):
    ms = jnp.mean(x * x, axis=-1, keepdims=True)
    return x * lax.rsqrt(ms + NORM_EPS) * g


def _swiglu_half_step(x, norm_g, wg_ref, wu_ref, wd_ref):
    halves = jnp.split(x, 2, axis=0)
    xn = [_rms(h, norm_g).astype(jnp.bfloat16) for h in halves]
    gu = [(jnp.dot(n, wg_ref[...], preferred_element_type=jnp.float32),
           jnp.dot(n, wu_ref[...], preferred_element_type=jnp.float32)) for n in xn]
    out = []
    for h, (g, u) in zip(halves, gu):
        hact = (g * jax.nn.sigmoid(g) * u).astype(jnp.bfloat16)
        out.append(h + 0.5 * jnp.dot(hact, wd_ref[...], preferred_element_type=jnp.float32))
    return jnp.concatenate(out, axis=0)


def _stage_weight(src_hbm, dst_ref, stage_ref, sem_ref, *, rows, col0=0, ncols=None,
                  dst_row0=0, dst_col0=0, transpose=False):
    n_rows = src_hbm.shape[0]
    ncols = src_hbm.shape[1] - col0 if ncols is None else ncols
    n_slots = stage_ref.shape[0]
    assert n_rows % rows == 0 and rows <= stage_ref.shape[1] and ncols <= stage_ref.shape[2]
    n_slabs = n_rows // rows

    def slab_copy(c):
        return pltpu.make_async_copy(
            src_hbm.at[pl.ds(c * rows, rows), pl.ds(col0, ncols)],
            stage_ref.at[c % n_slots, pl.ds(0, rows), pl.ds(0, ncols)],
            sem_ref.at[c % n_slots])

    for c in range(min(n_slots - 1, n_slabs)):
        slab_copy(c).start()
    for c in range(n_slabs):
        if c + n_slots - 1 < n_slabs:
            slab_copy(c + n_slots - 1).start()
        slab_copy(c).wait()
        slab = stage_ref[c % n_slots, :rows, :ncols]
        if transpose:
            dst_ref[dst_row0:dst_row0 + ncols,
                    dst_col0 + c * rows:dst_col0 + (c + 1) * rows] = slab.T.astype(dst_ref.dtype)
        else:
            dst_ref[dst_row0 + c * rows:dst_row0 + (c + 1) * rows,
                    dst_col0:dst_col0 + ncols] = slab.astype(dst_ref.dtype)


def _stage_ffn_weights(wg_hbm, wu_hbm, wd_hbm, wg_ref, wu_ref, wd_ref, stage_wide, stage_tall, sem):
    _stage_weight(wg_hbm, wg_ref, stage_wide, sem, rows=STAGE_WIDE_ROWS)
    _stage_weight(wu_hbm, wu_ref, stage_wide, sem, rows=STAGE_WIDE_ROWS)
    _stage_weight(wd_hbm, wd_ref, stage_tall, sem, rows=STAGE_TALL_ROWS)


def _ffn1_kernel(x_ref, norm_ref, wg_hbm, wu_hbm, wd_hbm, win32_ref, o_ref, win16_ref,
                 wg_ref, wu_ref, wd_ref, stage_wide, stage_tall, sem):
    @pl.when(pl.program_id(0) == 0)
    def _():
        _stage_ffn_weights(wg_hbm, wu_hbm, wd_hbm, wg_ref, wu_ref, wd_ref,
                           stage_wide, stage_tall, sem)

    win16_ref[...] = win32_ref[...].astype(win16_ref.dtype)
    o_ref[...] = _swiglu_half_step(x_ref[...], norm_ref[...], wg_ref, wu_ref, wd_ref)


def _in_proj_kernel(h_ref, norm_ref, win_ref, rowscale_ref, bgate_ref,
                    k_ref, qvt_ref, gate_ref, wqvt_ref):
    w = ATT_W

    @pl.when(pl.program_id(0) == 0)
    def _():
        for piece, src_block in enumerate((0, 3, 2, 5)):
            cols = win_ref[:, src_block * w:(src_block + 1) * w].astype(jnp.float32)
            wqvt_ref[piece * w:(piece + 1) * w, :] = cols.T.astype(wqvt_ref.dtype)

    half = h_ref.shape[0] // 2
    rows = [slice(0, half), slice(half, 2 * half)]
    n = [_rms(h_ref[r, :], norm_ref[...]).astype(jnp.bfloat16) for r in rows]
    g = [jnp.dot(nh, win_ref[:, 6 * w:], preferred_element_type=jnp.float32) for nh in n]
    for r, gh in zip(rows, g):
        gate_ref[r, :] = jax.nn.sigmoid(gh + bgate_ref[...]).astype(jnp.bfloat16)
    for r, nh in zip(rows, n):
        for piece, src_block in enumerate((1, 4)):
            k_ref[r, piece * w:(piece + 1) * w] = jnp.dot(
                nh, win_ref[:, src_block * w:(src_block + 1) * w],
                preferred_element_type=jnp.float32).astype(jnp.bfloat16)
    for r, nh in zip(rows, n):
        qvt = lax.dot_general(wqvt_ref[...], nh, _NT, preferred_element_type=jnp.float32)
        qvt_ref[:, r] = (qvt * rowscale_ref[...]).astype(jnp.bfloat16)


def _emit_pipelined(stages, n):
    state = [dict() for _ in range(n)]
    for step in range(n + len(stages) - 1):
        for s, stage in enumerate(stages):
            t = step - s
            if 0 <= t < n:
                stage(t, state[t])


def _diff_attn_kernel(slope_ref, lam_ref, q_ref, k_ref, vt_ref, subln_ref, *refs):
    tq, tk, cw = ATT_TQ, ATT_TK, CHAIN_W
    per_map = tq // cw
    n_chains = 2 * per_map
    n_diag = tq // tk
    assert n_diag % 2 == 0 and tk == 2 * cw and tk <= 2 * BF16_EXACT_INT
    h = pl.program_id(1)
    i = pl.program_id(2)
    slope = slope_ref[h] * LOG2E
    lam = lam_ref[0]

    n_side = (len(refs) - 7) // 2
    w32_refs, o_ref, w16_refs = refs[:n_side], refs[n_side], refs[n_side + 1:2 * n_side + 1]
    mask_ref, kfeat_ref, s_ref, m_ref, l_ref, acc_ref = refs[2 * n_side + 1:]

    @pl.when(jnp.logical_and(pl.program_id(0) == 0, jnp.logical_and(h == 0, i == 0)))
    def _():
        krow = lax.broadcasted_iota(jnp.int32, (tk, tk), 0)
        qcol = lax.broadcasted_iota(jnp.int32, (tk, tk), 1)
        mask_ref[...] = jnp.where(qcol >= krow, 0.0, -jnp.inf)
        kpos = lax.broadcasted_iota(jnp.int32, (tk, HEAD_W), 0)
        klane = lax.broadcasted_iota(jnp.int32, (tk, HEAD_W), 1)
        k_hi = jnp.where(kpos >= BF16_EXACT_INT, BF16_EXACT_INT, 0)
        kfeat_ref[...] = jnp.where(klane < SLOPE_TERMS, k_hi,
                                   jnp.where(klane < 2 * SLOPE_TERMS, kpos - k_hi, 0)
                                   ).astype(jnp.float32).astype(jnp.bfloat16)

    qt = q_ref[...]
    chan = lax.broadcasted_iota(jnp.int32, (HEAD_W, tq), 0)
    zero = jnp.zeros_like(qt)
    q_maps = (jnp.where(chan < DA_QK_DIM, qt, zero), jnp.where(chan >= DA_QK_DIM, qt, zero))

    sl = jnp.full((HEAD_W, cw), slope, jnp.float32)
    hi = sl.astype(jnp.bfloat16).astype(jnp.float32)
    mid = (sl - hi).astype(jnp.bfloat16).astype(jnp.float32)
    lo = sl - hi - mid
    frow = lax.broadcasted_iota(jnp.int32, (HEAD_W, cw), 0)
    part = frow % SLOPE_TERMS
    q_feat = jnp.where(frow < 2 * SLOPE_TERMS,
                       jnp.where(part == 0, hi, jnp.where(part == 1, mid, lo)),
                       0.0).astype(jnp.bfloat16)
    q_chain = [jnp.concatenate(
        [q_maps[c // per_map][:, (c % per_map) * cw:(c % per_map + 1) * cw], q_feat], axis=0)
        for c in range(n_chains)]

    m_ref[...] = jnp.full_like(m_ref, -jnp.inf)
    l_ref[...] = jnp.zeros_like(l_ref)
    acc_ref[...] = jnp.zeros_like(acc_ref)

    def chain_mode(c, d):
        if d is None:
            return "full"
        q_lo = (c % per_map) * cw
        if q_lo + cw <= d * tk:
            return "skip"
        if q_lo >= (d + 1) * tk:
            return "full"
        return q_lo - d * tk

    def visible_keys(c, d):
        return cw if chain_mode(c, d) == 0 else tk

    def score_stage(slot, j, d=None):
        chains = [c for c in range(n_chains) if chain_mode(c, d) != "skip"]
        kb = k_ref[0, pl.ds(pl.multiple_of(j * tk, tk), tk), :]
        kb = jnp.concatenate([kb, kfeat_ref[...]], axis=1)

        def scores(t, st):
            if t < len(chains):
                c = chains[t]
                nk = visible_keys(c, d)
                s_ref[slot, c, :nk] = jnp.dot(kb[:nk], q_chain[c],
                                              preferred_element_type=jnp.float32)

        return len(chains), scores

    def scores_to(slot, j, d=None):
        n, scores = score_stage(slot, j, d)
        for t in range(n):
            scores(t, None)

    def consume(slot, j, d=None, ahead=None):
        chains = [c for c in range(n_chains) if chain_mode(c, d) != "skip"]
        vtb = vt_ref[:, pl.ds(pl.multiple_of(j * tk, tk), tk)]
        vtb = jnp.concatenate([vtb, jnp.ones((SUM_ROWS, tk), vtb.dtype)], axis=0)
        shift = -slope * (i * tq - j * tk).astype(jnp.float32)

        def column_max(t, st):
            if t >= len(chains):
                return
            c = chains[t]
            nk = visible_keys(c, d)
            s = s_ref[slot, c, :nk]
            mode = chain_mode(c, d)
            if mode != "full":
                s = s + mask_ref[:nk, mode:mode + cw]
                s_ref[slot, c, :nk] = s
            st["cmax"] = jnp.max(s, axis=0, keepdims=True) + shift

        def softmax_pv(t, st):
            if t >= len(chains):
                return
            c = chains[t]
            nk = visible_keys(c, d)
            lanes = slice(c * cw, (c + 1) * cw)
            m_prev = m_ref[:, lanes]
            m_new = jnp.maximum(m_prev, st.pop("cmax"))
            st["alpha"] = jnp.exp2(m_prev - m_new)
            p = jnp.exp2(s_ref[slot, c, :nk] - (m_new - shift))
            m_ref[:, lanes] = m_new
            st["pv"] = jnp.dot(vtb[:, :nk], p.astype(jnp.bfloat16),
                               preferred_element_type=jnp.float32)

        def accumulate(t, st):
            if t >= len(chains):
                return
            c = chains[t]
            lanes = slice(c * cw, (c + 1) * cw)
            alpha, pv = st.pop("alpha"), st.pop("pv")
            acc_ref[:, lanes] = alpha * acc_ref[:, lanes] + pv[:HEAD_W]
            l_ref[:, lanes] = alpha * l_ref[:, lanes] + pv[HEAD_W:HEAD_W + 1]

        n_ahead, ahead_stage = ahead if ahead is not None else (0, None)
        stages = (column_max, softmax_pv, accumulate)
        if ahead is not None:
            stages = (ahead_stage,) + stages
        _emit_pipelined(stages, max(len(chains), n_ahead))

    def step(slot, j):
        consume(slot, j, ahead=score_stage(1 - slot, j + 1))

    scores_to(0, 0)
    for w32, w16 in zip(w32_refs, w16_refs):
        w16[...] = w32[...].astype(w16.dtype)

    def pair(jj, carry):
        step(0, 2 * jj)
        step(1, 2 * jj + 1)
        return carry

    first_diag = n_diag * i
    lax.fori_loop(0, first_diag // 2, pair, 0)
    for d in range(n_diag):
        ahead = score_stage((d + 1) % 2, first_diag + d + 1, d + 1) if d + 1 < n_diag else None
        consume(d % 2, first_diag + d, d, ahead=ahead)

    o = acc_ref[...] / l_ref[...]
    a = o[:, :tq] - lam * o[:, tq:]
    ms = jnp.mean(a * a, axis=0, keepdims=True)
    a = a * lax.rsqrt(ms + NORM_EPS) * subln_ref[...] * (1.0 - LAMBDA_INIT)
    o_ref[0] = a.T.astype(o_ref.dtype)


def _sb_attn_kernel(q_ref, k_ref, vt_ref, o_ref, c_ref, acc_ref):
    tq, cw = SB_TQ, CHAIN_W
    n_chains = tq // cw
    i = pl.program_id(2)
    qt = q_ref[...]
    q_chain = [qt[:, c * cw:(c + 1) * cw] for c in range(n_chains)]

    krow = lax.broadcasted_iota(jnp.int32, (cw, cw), 0)
    qcol = lax.broadcasted_iota(jnp.int32, (cw, cw), 1)
    strict = krow < qcol
    lrow = lax.broadcasted_iota(jnp.int32, (cw + SUM_ROWS, cw), 0)
    lcol = lax.broadcasted_iota(jnp.int32, (cw + SUM_ROWS, cw), 1)
    later = jnp.where(jnp.logical_or(lcol > lrow, lrow >= cw), 1.0, 0.0).astype(jnp.bfloat16)

    c_ref[...] = jnp.zeros_like(c_ref)
    acc_ref[...] = jnp.zeros_like(acc_ref)

    def run_pieces(pieces):
        def scores(t, st):
            sub, c, _ = pieces[t]
            start = pl.multiple_of(sub * cw, cw)
            st["z"] = jnp.dot(k_ref[0, pl.ds(start, cw), :], q_chain[c],
                              preferred_element_type=jnp.float32)

        def suffix(t, st):
            _, _, triangular = pieces[t]
            z = st.pop("z")
            u = jnp.maximum(z, 0.0) + jnp.log2(1.0 + jnp.exp2(-jnp.abs(z)))
            st["log_sig"] = z - u
            if triangular:
                u = jnp.where(strict, u, 0.0)
            st["tail"] = jnp.dot(later, u.astype(jnp.bfloat16),
                                 preferred_element_type=jnp.float32)

        def weights_pv(t, st):
            sub, _, triangular = pieces[t]
            start = pl.multiple_of(sub * cw, cw)
            tail = st.pop("tail")
            st["usum"] = tail[cw:cw + 1]
            a = jnp.exp2(st.pop("log_sig") - tail[:cw])
            if triangular:
                a = jnp.where(strict, a, 0.0)
            st["pv"] = jnp.dot(vt_ref[:, pl.ds(start, cw)], a.astype(jnp.bfloat16),
                               preferred_element_type=jnp.float32)

        def accumulate(t, st):
            _, c, _ = pieces[t]
            lanes = slice(c * cw, (c + 1) * cw)
            carry = c_ref[:, lanes]
            acc_ref[:, lanes] += st.pop("pv") * jnp.exp2(-carry)
            c_ref[:, lanes] = carry + st.pop("usum")

        _emit_pipelined((scores, suffix, weights_pv, accumulate), len(pieces))

    diag = [n_chains * i + c for c in range(n_chains)]
    head = [(diag[c], c, True) for c in reversed(range(n_chains))]
    second = [(diag[c] - 1, c, False) for c in reversed(range(n_chains))]

    @pl.when(i == 0)
    def _():
        run_pieces(head + [p for p in second if p[1] > 0])

    @pl.when(i > 0)
    def _():
        run_pieces(head + second)

    def unfinished(c, depth):
        lanes = slice(c * cw, (c + 1) * cw)
        return jnp.logical_and(diag[c] - depth >= 0,
                               jnp.min(c_ref[:, lanes]) < SB_DONE_LOG2)

    def any_unfinished(depth):
        go = unfinished(0, depth)
        for c in range(1, n_chains):
            go = jnp.logical_or(go, unfinished(c, depth))
        return go

    def body(carry):
        depth, _ = carry
        for c in range(n_chains):
            @pl.when(unfinished(c, depth))
            def _():
                run_pieces([(diag[c] - depth, c, False)])
        return depth + 1, any_unfinished(depth + 1)

    lax.while_loop(lambda carry: carry[1], body, (jnp.int32(2), any_unfinished(2)))
    o_ref[0] = acc_ref[...].T.astype(o_ref.dtype)


def _mix_ffn2_kernel(h_ref, a_ref, b_ref, gate_ref, wa_ref, wb_ref, wout_ref,
                     norm2_ref, wg_ref, wu_ref, wd_ref, normf_ref, o_ref):
    half = h_ref.shape[0] // 2
    rows = [slice(0, half), slice(half, 2 * half)]
    yab = [(jnp.dot(a_ref[r, :], wa_ref[...], preferred_element_type=jnp.float32),
            jnp.dot(b_ref[r, :], wb_ref[...], preferred_element_type=jnp.float32)) for r in rows]
    h2 = []
    for r, (ya, yb) in zip(rows, yab):
        gate = gate_ref[r, :].astype(jnp.float32)
        y = (gate[:, :D_MODEL] * ya + gate[:, D_MODEL:] * yb).astype(jnp.bfloat16)
        h2.append(h_ref[r, :] + jnp.dot(y, wout_ref[...], preferred_element_type=jnp.float32))
    h2 = jnp.concatenate(h2, axis=0)
    h3 = _swiglu_half_step(h2, norm2_ref[...], wg_ref, wu_ref, wd_ref)
    o_ref[...] = _rms(h3, normf_ref[...])


def _const_spec(shape):
    return pl.BlockSpec(shape, lambda *_: (0,) * len(shape), pipeline_mode=pl.Buffered(1))


def _tc_params(n_axes):
    return pltpu.CompilerParams(dimension_semantics=("arbitrary",) * n_axes,
                                vmem_limit_bytes=VMEM_LIMIT_BYTES)


def kernel(x, ffn1_norm, ffn1_w_gate, ffn1_w_up, ffn1_w_down, mix_norm, w_in, b_gate, lambda_q1, lambda_k1, lambda_q2, lambda_k2, diff_subln, w_branch_diff, w_branch_sb, w_out, ffn2_norm, ffn2_w_gate, ffn2_w_up, ffn2_w_down, final_norm):
    B, S, D = x.shape
    T = B * S
    f32, bf16 = jnp.float32, jnp.bfloat16
    xt = x.reshape(T, D)
    row = lambda v: v.reshape(1, -1).astype(f32)

    tok_spec = pl.BlockSpec((FFN_TM, D), lambda t: (t, 0))
    hbm_spec = pl.BlockSpec(memory_space=pl.ANY)
    ffn_weight_scratch = [pltpu.VMEM((D, D_FF), bf16), pltpu.VMEM((D, D_FF), bf16),
                          pltpu.VMEM((D_FF, D), bf16)]
    ffn_stage_scratch = [pltpu.VMEM((STAGE_SLOTS, STAGE_WIDE_ROWS, D_FF), f32),
                         pltpu.VMEM((STAGE_SLOTS, STAGE_TALL_ROWS, D), f32),
                         pltpu.SemaphoreType.DMA((STAGE_SLOTS,))]

    def row_block_spec(w, n_steps, step_of):
        n_rows, n_cols = w.shape
        n_blocks = max(n for n in range(1, n_steps + 1)
                       if n_steps % n == 0 and n_rows % (16 * n) == 0)
        return pl.BlockSpec((n_rows // n_blocks, n_cols),
                            lambda *idx: ((step_of(*idx) * n_blocks) // n_steps, 0))

    win_side_spec = row_block_spec(w_in[0], T // FFN_TM, lambda t: t)
    h1, w_in_bf16 = pl.pallas_call(
        _ffn1_kernel,
        grid=(T // FFN_TM,),
        in_specs=[tok_spec, _const_spec((1, D)), hbm_spec, hbm_spec, hbm_spec, win_side_spec],
        out_specs=[tok_spec, win_side_spec],
        out_shape=[jax.ShapeDtypeStruct((T, D), f32), jax.ShapeDtypeStruct(w_in[0].shape, bf16)],
        scratch_shapes=ffn_weight_scratch + ffn_stage_scratch,
        compiler_params=_tc_params(1),
        name="ffn1",
    )(xt, row(ffn1_norm[0]), ffn1_w_gate[0], ffn1_w_up[0], ffn1_w_down[0], w_in[0])

    rowscale = jnp.ones((QVT_W,), f32)
    rowscale = rowscale.at[0:ATT_W].set(DA_QK_DIM ** -0.5 * LOG2E)
    rowscale = rowscale.at[ATT_W:2 * ATT_W].set(HEAD_W ** -0.5 * LOG2E)
    k, qvt, gates = pl.pallas_call(
        _in_proj_kernel,
        grid=(T // PROJ_TM,),
        in_specs=[pl.BlockSpec((PROJ_TM, D), lambda t: (t, 0)), _const_spec((1, D)),
                  _const_spec(w_in[0].shape), _const_spec((QVT_W, 1)), _const_spec((1, GATE_W))],
        out_specs=[pl.BlockSpec((PROJ_TM, K_W), lambda t: (t, 0)),
                   pl.BlockSpec((QVT_W, PROJ_TM), lambda t: (0, t)),
                   pl.BlockSpec((PROJ_TM, GATE_W), lambda t: (t, 0))],
        out_shape=[jax.ShapeDtypeStruct((T, K_W), bf16),
                   jax.ShapeDtypeStruct((QVT_W, T), bf16),
                   jax.ShapeDtypeStruct((T, GATE_W), bf16)],
        scratch_shapes=[pltpu.VMEM((QVT_W, D), bf16)],
        compiler_params=_tc_params(1),
        name="in_proj",
    )(h1, row(mix_norm[0]), w_in_bf16, rowscale.reshape(-1, 1), row(b_gate[0]))
    k3 = k.reshape(B, S, K_W)

    def q_spec(slab, tq):
        return pl.BlockSpec((HEAD_W, tq),
                            lambda b, h, i: (slab * N_HEADS + h, b * (S // tq) + i))

    def k_spec(slab):
        return pl.BlockSpec((1, S, HEAD_W), lambda b, h, i: (b, 0, slab * N_HEADS + h))

    def vt_spec(slab):
        return pl.BlockSpec((HEAD_W, S), lambda b, h, i: (slab * N_HEADS + h, b))

    def att_out_spec(tq):
        return pl.BlockSpec((1, tq, HEAD_W), lambda b, h, i: (b, i, h))

    att_out_shape = jax.ShapeDtypeStruct((B, S, ATT_W), bf16)
    smem_spec = pl.BlockSpec(memory_space=pltpu.SMEM)

    lam = (jnp.exp(jnp.sum(lambda_q1[0].astype(f32) * lambda_k1[0].astype(f32)))
           - jnp.exp(jnp.sum(lambda_q2[0].astype(f32) * lambda_k2[0].astype(f32)))
           + LAMBDA_INIT).reshape(1)
    slopes = jnp.exp2(-8.0 * jnp.arange(1, N_HEADS + 1, dtype=f32) / N_HEADS)

    mix_weights = [w_branch_diff[0], w_branch_sb[0], w_out[0],
                   ffn2_w_gate[0], ffn2_w_up[0], ffn2_w_down[0]]
    nq_diff = S // ATT_TQ
    side_specs = [row_block_spec(w, B * N_HEADS * nq_diff,
                                 lambda b, h, i: (b * N_HEADS + h) * nq_diff + i)
                  for w in mix_weights]
    a, *mix_weights_bf16 = pl.pallas_call(
        _diff_attn_kernel,
        grid=(B, N_HEADS, nq_diff),
        in_specs=[smem_spec, smem_spec, q_spec(0, ATT_TQ), k_spec(0), vt_spec(2),
                  _const_spec((HEAD_W, 1))] + side_specs,
        out_specs=[att_out_spec(ATT_TQ)] + side_specs,
        out_shape=[att_out_shape] + [jax.ShapeDtypeStruct(w.shape, bf16) for w in mix_weights],
        scratch_shapes=[pltpu.VMEM((ATT_TK, ATT_TK), f32), pltpu.VMEM((ATT_TK, HEAD_W), bf16),
                        pltpu.VMEM((2, 2 * ATT_TQ // CHAIN_W, ATT_TK, CHAIN_W), f32),
                        pltpu.VMEM((1, 2 * ATT_TQ), f32), pltpu.VMEM((1, 2 * ATT_TQ), f32),
                        pltpu.VMEM((HEAD_W, 2 * ATT_TQ), f32)],
        compiler_params=_tc_params(3),
        name="diff_attn",
    )(slopes, lam, qvt, k3, qvt, diff_subln[0].reshape(-1, 1).astype(f32), *mix_weights)

    b = pl.pallas_call(
        _sb_attn_kernel,
        grid=(B, N_HEADS, S // SB_TQ),
        in_specs=[q_spec(1, SB_TQ), k_spec(1), vt_spec(3)],
        out_specs=att_out_spec(SB_TQ),
        out_shape=att_out_shape,
        scratch_shapes=[pltpu.VMEM((1, SB_TQ), f32), pltpu.VMEM((HEAD_W, SB_TQ), f32)],
        compiler_params=_tc_params(3),
        name="sb_attn",
    )(qvt, k3, qvt)

    out = pl.pallas_call(
        _mix_ffn2_kernel,
        grid=(T // FFN_TM,),
        in_specs=[tok_spec,
                  pl.BlockSpec((FFN_TM, ATT_W), lambda t: (t, 0)),
                  pl.BlockSpec((FFN_TM, ATT_W), lambda t: (t, 0)),
                  pl.BlockSpec((FFN_TM, GATE_W), lambda t: (t, 0)),
                  _const_spec((ATT_W, D)), _const_spec((ATT_W, D)),
                  _const_spec((D, D)), _const_spec((1, D)), _const_spec((D, D_FF)),
                  _const_spec((D, D_FF)), _const_spec((D_FF, D)), _const_spec((1, D))],
        out_specs=tok_spec,
        out_shape=jax.ShapeDtypeStruct((T, D), f32),
        compiler_params=_tc_params(1),
        name="mix_ffn2",
    )(h1, a.reshape(T, -1), b.reshape(T, -1), gates, *mix_weights_bf16[:3],
      row(ffn2_norm[0]), *mix_weights_bf16[3:], row(final_norm))
    return out.reshape(B, S, D)
```

```python
import math

import jax
import jax.numpy as jnp
from jax import lax
from jax.experimental import pallas as pl
from jax.experimental.pallas import tpu as pltpu

D_MODEL = 1024
D_FF = 2816
N_HEADS = 4
HEAD_W = 128
DA_QK_DIM = 64
ATT_W = N_HEADS * HEAD_W
K_W = 2 * ATT_W
QVT_W = 4 * ATT_W
GATE_W = 2 * D_MODEL
NORM_EPS = 1e-5
LAMBDA_INIT = 0.8 - 0.6 * math.exp(-0.3 * 0)
LOG2E = 1.0 / math.log(2.0)

VMEM_LIMIT_BYTES = 56 * 1024 * 1024

STAGE_SLOTS = 3
STAGE_WIDE_ROWS = 128
STAGE_TALL_ROWS = 352

FFN_TM = 512
PROJ_TM = 512
ATT_TQ = 2048
ATT_TK = 512
SB_TQ = 4096
CHAIN_W = 256
SB_DONE_LOG2 = 160.0
SUM_ROWS = 16
SLOPE_TERMS = 3
BF16_EXACT_INT = 256

_NT = (((1,), (1,)), ((), ()))


def _rms(x, g):
    ms = jnp.mean(x * x, axis=-1, keepdims=True)
    return x * lax.rsqrt(ms + NORM_EPS) * g


def _swiglu_half_step(x, norm_g, wg_ref, wu_ref, wd_ref):
    halves = jnp.split(x, 2, axis=0)
    xn = [_rms(h, norm_g).astype(jnp.bfloat16) for h in halves]
    gu = [(jnp.dot(n, wg_ref[...], preferred_element_type=jnp.float32),
           jnp.dot(n, wu_ref[...], preferred_element_type=jnp.float32)) for n in xn]
    out = []
    for h, (g, u) in zip(halves, gu):
        hact = (g * jax.nn.sigmoid(g) * u).astype(jnp.bfloat16)
        out.append(h + 0.5 * jnp.dot(hact, wd_ref[...], preferred_element_type=jnp.float32))
    return jnp.concatenate(out, axis=0)


def _stage_weight(src_hbm, dst_ref, stage_ref, sem_ref):
    n_slots, rows, _ = stage_ref.shape
    n_rows = src_hbm.shape[0]
    assert n_rows % rows == 0 and src_hbm.shape[1] == stage_ref.shape[2]
    n_slabs = n_rows // rows

    def slab_copy(c):
        return pltpu.make_async_copy(src_hbm.at[pl.ds(c * rows, rows)],
                                     stage_ref.at[c % n_slots], sem_ref.at[c % n_slots])

    for c in range(min(n_slots - 1, n_slabs)):
        slab_copy(c).start()
    for c in range(n_slabs):
        if c + n_slots - 1 < n_slabs:
            slab_copy(c + n_slots - 1).start()
        slab_copy(c).wait()
        dst_ref[c * rows:(c + 1) * rows, :] = stage_ref[c % n_slots].astype(dst_ref.dtype)


def _ffn1_kernel(x_ref, norm_ref, wg_hbm, wu_hbm, wd_hbm, win32_ref, o_ref, win16_ref,
                 wg_ref, wu_ref, wd_ref, stage_wide, stage_tall, sem):
    @pl.when(pl.program_id(0) == 0)
    def _():
        _stage_weight(wg_hbm, wg_ref, stage_wide, sem)
        _stage_weight(wu_hbm, wu_ref, stage_wide, sem)
        _stage_weight(wd_hbm, wd_ref, stage_tall, sem)

    win16_ref[...] = win32_ref[...].astype(win16_ref.dtype)
    o_ref[...] = _swiglu_half_step(x_ref[...], norm_ref[...], wg_ref, wu_ref, wd_ref)


def _in_proj_kernel(h_ref, norm_ref, win_ref, rowscale_ref, bgate_ref,
                    k_ref, qvt_ref, gate_ref, wqvt_ref):
    w = ATT_W

    @pl.when(pl.program_id(0) == 0)
    def _():
        for piece, src_block in enumerate((0, 3, 2, 5)):
            cols = win_ref[:, src_block * w:(src_block + 1) * w].astype(jnp.float32)
            wqvt_ref[piece * w:(piece + 1) * w, :] = cols.T.astype(wqvt_ref.dtype)

    half = h_ref.shape[0] // 2
    rows = [slice(0, half), slice(half, 2 * half)]
    n = [_rms(h_ref[r, :], norm_ref[...]).astype(jnp.bfloat16) for r in rows]
    g = [jnp.dot(nh, win_ref[:, 6 * w:], preferred_element_type=jnp.float32) for nh in n]
    for r, gh in zip(rows, g):
        gate_ref[r, :] = jax.nn.sigmoid(gh + bgate_ref[...]).astype(jnp.bfloat16)
    for r, nh in zip(rows, n):
        for piece, src_block in enumerate((1, 4)):
            k_ref[r, piece * w:(piece + 1) * w] = jnp.dot(
                nh, win_ref[:, src_block * w:(src_block + 1) * w],
                preferred_element_type=jnp.float32).astype(jnp.bfloat16)
    for r, nh in zip(rows, n):
        qvt = lax.dot_general(wqvt_ref[...], nh, _NT, preferred_element_type=jnp.float32)
        qvt_ref[:, r] = (qvt * rowscale_ref[...]).astype(jnp.bfloat16)


def _emit_pipelined(stages, n):
    state = [dict() for _ in range(n)]
    for step in range(n + len(stages) - 1):
        for s, stage in enumerate(stages):
            t = step - s
            if 0 <= t < n:
                stage(t, state[t])


def _diff_attn_kernel(slope_ref, lam_ref, q_ref, k_ref, vt_ref, subln_ref, *refs):
    tq, tk, cw = ATT_TQ, ATT_TK, CHAIN_W
    per_map = tq // cw
    n_chains = 2 * per_map
    n_diag = tq // tk
    assert n_diag % 2 == 0 and tk == 2 * cw and tk <= 2 * BF16_EXACT_INT
    h = pl.program_id(1)
    i = pl.program_id(2)
    slope = slope_ref[h] * LOG2E
    lam = lam_ref[0]

    n_side = (len(refs) - 7) // 2
    w32_refs, o_ref, w16_refs = refs[:n_side], refs[n_side], refs[n_side + 1:2 * n_side + 1]
    mask_ref, kfeat_ref, s_ref, m_ref, l_ref, acc_ref = refs[2 * n_side + 1:]

    @pl.when(jnp.logical_and(pl.program_id(0) == 0, jnp.logical_and(h == 0, i == 0)))
    def _():
        krow = lax.broadcasted_iota(jnp.int32, (tk, tk), 0)
        qcol = lax.broadcasted_iota(jnp.int32, (tk, tk), 1)
        mask_ref[...] = jnp.where(qcol >= krow, 0.0, -jnp.inf)
        kpos = lax.broadcasted_iota(jnp.int32, (tk, HEAD_W), 0)
        klane = lax.broadcasted_iota(jnp.int32, (tk, HEAD_W), 1)
        k_hi = jnp.where(kpos >= BF16_EXACT_INT, BF16_EXACT_INT, 0)
        kfeat_ref[...] = jnp.where(klane < SLOPE_TERMS, k_hi,
                                   jnp.where(klane < 2 * SLOPE_TERMS, kpos - k_hi, 0)
                                   ).astype(jnp.float32).astype(jnp.bfloat16)

    qt = q_ref[...]
    chan = lax.broadcasted_iota(jnp.int32, (HEAD_W, tq), 0)
    zero = jnp.zeros_like(qt)
    q_maps = (jnp.where(chan < DA_QK_DIM, qt, zero), jnp.where(chan >= DA_QK_DIM, qt, zero))

    sl = jnp.full((HEAD_W, cw), slope, jnp.float32)
    hi = sl.astype(jnp.bfloat16).astype(jnp.float32)
    mid = (sl - hi).astype(jnp.bfloat16).astype(jnp.float32)
    lo = sl - hi - mid
    frow = lax.broadcasted_iota(jnp.int32, (HEAD_W, cw), 0)
    part = frow % SLOPE_TERMS
    q_feat = jnp.where(frow < 2 * SLOPE_TERMS,
                       jnp.where(part == 0, hi, jnp.where(part == 1, mid, lo)),
                       0.0).astype(jnp.bfloat16)
    q_chain = [jnp.concatenate(
        [q_maps[c // per_map][:, (c % per_map) * cw:(c % per_map + 1) * cw], q_feat], axis=0)
        for c in range(n_chains)]

    m_ref[...] = jnp.full_like(m_ref, -jnp.inf)
    l_ref[...] = jnp.zeros_like(l_ref)
    acc_ref[...] = jnp.zeros_like(acc_ref)

    def chain_mode(c, d):
        if d is None:
            return "full"
        q_lo = (c % per_map) * cw
        if q_lo + cw <= d * tk:
            return "skip"
        if q_lo >= (d + 1) * tk:
            return "full"
        return q_lo - d * tk

    def visible_keys(c, d):
        return cw if chain_mode(c, d) == 0 else tk

    def score_stage(slot, j, d=None):
        chains = [c for c in range(n_chains) if chain_mode(c, d) != "skip"]
        kb = k_ref[0, pl.ds(pl.multiple_of(j * tk, tk), tk), :]
        kb = jnp.concatenate([kb, kfeat_ref[...]], axis=1)

        def scores(t, st):
            if t < len(chains):
                c = chains[t]
                nk = visible_keys(c, d)
                s_ref[slot, c, :nk] = jnp.dot(kb[:nk], q_chain[c],
                                              preferred_element_type=jnp.float32)

        return len(chains), scores

    def scores_to(slot, j, d=None):
        n, scores = score_stage(slot, j, d)
        for t in range(n):
            scores(t, None)

    def consume(slot, j, d=None, ahead=None):
        chains = [c for c in range(n_chains) if chain_mode(c, d) != "skip"]
        vtb = vt_ref[:, pl.ds(pl.multiple_of(j * tk, tk), tk)]
        vtb = jnp.concatenate([vtb, jnp.ones((SUM_ROWS, tk), vtb.dtype)], axis=0)
        shift = -slope * (i * tq - j * tk).astype(jnp.float32)

        def column_max(t, st):
            if t >= len(chains):
                return
            c = chains[t]
            nk = visible_keys(c, d)
            s = s_ref[slot, c, :nk]
            mode = chain_mode(c, d)
            if mode != "full":
                s = s + mask_ref[:nk, mode:mode + cw]
                s_ref[slot, c, :nk] = s
            st["cmax"] = jnp.max(s, axis=0, keepdims=True) + shift

        def softmax_pv(t, st):
            if t >= len(chains):
                return
            c = chains[t]
            nk = visible_keys(c, d)
            lanes = slice(c * cw, (c + 1) * cw)
            m_prev = m_ref[:, lanes]
            m_new = jnp.maximum(m_prev, st.pop("cmax"))
            st["alpha"] = jnp.exp2(m_prev - m_new)
            p = jnp.exp2(s_ref[slot, c, :nk] - (m_new - shift))
            m_ref[:, lanes] = m_new
            st["pv"] = jnp.dot(vtb[:, :nk], p.astype(jnp.bfloat16),
                               preferred_element_type=jnp.float32)

        def accumulate(t, st):
            if t >= len(chains):
                return
            c = chains[t]
            lanes = slice(c * cw, (c + 1) * cw)
            alpha, pv = st.pop("alpha"), st.pop("pv")
            acc_ref[:, lanes] = alpha * acc_ref[:, lanes] + pv[:HEAD_W]
            l_ref[:, lanes] = alpha * l_ref[:, lanes] + pv[HEAD_W:HEAD_W + 1]

        n_ahead, ahead_stage = ahead if ahead is not None else (0, None)
        stages = (column_max, softmax_pv, accumulate)
        if ahead is not None:
            stages = (ahead_stage,) + stages
        _emit_pipelined(stages, max(len(chains), n_ahead))

    def step(slot, j):
        consume(slot, j, ahead=score_stage(1 - slot, j + 1))

    scores_to(0, 0)
    for w32, w16 in zip(w32_refs, w16_refs):
        w16[...] = w32[...].astype(w16.dtype)

    def pair(jj, carry):
        step(0, 2 * jj)
        step(1, 2 * jj + 1)
        return carry

    first_diag = n_diag * i
    lax.fori_loop(0, first_diag // 2, pair, 0)
    for d in range(n_diag):
        ahead = score_stage((d + 1) % 2, first_diag + d + 1, d + 1) if d + 1 < n_diag else None
        consume(d % 2, first_diag + d, d, ahead=ahead)

    o = acc_ref[...] / l_ref[...]
    a = o[:, :tq] - lam * o[:, tq:]
    ms = jnp.mean(a * a, axis=0, keepdims=True)
    a = a * lax.rsqrt(ms + NORM_EPS) * subln_ref[...] * (1.0 - LAMBDA_INIT)
    o_ref[0] = a.T.astype(o_ref.dtype)


def _sb_attn_kernel(q_ref, k_ref, vt_ref, o_ref, c_ref, acc_ref):
    tq, cw = SB_TQ, CHAIN_W
    n_chains = tq // cw
    i = pl.program_id(2)
    qt = q_ref[...]
    q_chain = [qt[:, c * cw:(c + 1) * cw] for c in range(n_chains)]

    krow = lax.broadcasted_iota(jnp.int32, (cw, cw), 0)
    qcol = lax.broadcasted_iota(jnp.int32, (cw, cw), 1)
    strict = krow < qcol
    lrow = lax.broadcasted_iota(jnp.int32, (cw + SUM_ROWS, cw), 0)
    lcol = lax.broadcasted_iota(jnp.int32, (cw + SUM_ROWS, cw), 1)
    later = jnp.where(jnp.logical_or(lcol > lrow, lrow >= cw), 1.0, 0.0).astype(jnp.bfloat16)

    c_ref[...] = jnp.zeros_like(c_ref)
    acc_ref[...] = jnp.zeros_like(acc_ref)

    def run_pieces(pieces):
        def scores(t, st):
            sub, c, _ = pieces[t]
            start = pl.multiple_of(sub * cw, cw)
            st["z"] = jnp.dot(k_ref[0, pl.ds(start, cw), :], q_chain[c],
                              preferred_element_type=jnp.float32)

        def suffix(t, st):
            _, _, triangular = pieces[t]
            z = st.pop("z")
            u = jnp.maximum(z, 0.0) + jnp.log2(1.0 + jnp.exp2(-jnp.abs(z)))
            st["log_sig"] = z - u
            if triangular:
                u = jnp.where(strict, u, 0.0)
            st["tail"] = jnp.dot(later, u.astype(jnp.bfloat16),
                                 preferred_element_type=jnp.float32)

        def weights_pv(t, st):
            sub, _, triangular = pieces[t]
            start = pl.multiple_of(sub * cw, cw)
            tail = st.pop("tail")
            st["usum"] = tail[cw:cw + 1]
            a = jnp.exp2(st.pop("log_sig") - tail[:cw])
            if triangular:
                a = jnp.where(strict, a, 0.0)
            st["pv"] = jnp.dot(vt_ref[:, pl.ds(start, cw)], a.astype(jnp.bfloat16),
                               preferred_element_type=jnp.float32)

        def accumulate(t, st):
            _, c, _ = pieces[t]
            lanes = slice(c * cw, (c + 1) * cw)
            carry = c_ref[:, lanes]
            acc_ref[:, lanes] += st.pop("pv") * jnp.exp2(-carry)
            c_ref[:, lanes] = carry + st.pop("usum")

        _emit_pipelined((scores, suffix, weights_pv, accumulate), len(pieces))

    diag = [n_chains * i + c for c in range(n_chains)]
    head = [(diag[c], c, True) for c in reversed(range(n_chains))]
    second = [(diag[c] - 1, c, False) for c in reversed(range(n_chains))]

    @pl.when(i == 0)
    def _():
        run_pieces(head + [p for p in second if p[1] > 0])

    @pl.when(i > 0)
    def _():
        run_pieces(head + second)

    def unfinished(c, depth):
        lanes = slice(c * cw, (c + 1) * cw)
        return jnp.logical_and(diag[c] - depth >= 0,
                               jnp.min(c_ref[:, lanes]) < SB_DONE_LOG2)

    def any_unfinished(depth):
        go = unfinished(0, depth)
        for c in range(1, n_chains):
            go = jnp.logical_or(go, unfinished(c, depth))
        return go

    def body(carry):
        depth, _ = carry
        for c in range(n_chains):
            @pl.when(unfinished(c, depth))
            def _():
                run_pieces([(diag[c] - depth, c, False)])
        return depth + 1, any_unfinished(depth + 1)

    lax.while_loop(lambda carry: carry[1], body, (jnp.int32(2), any_unfinished(2)))
    o_ref[0] = acc_ref[...].T.astype(o_ref.dtype)


def _mix_ffn2_kernel(h_ref, a_ref, b_ref, gate_ref, wa_ref, wb_ref, wout_ref,
                     norm2_ref, wg_ref, wu_ref, wd_ref, normf_ref, o_ref):
    half = h_ref.shape[0] // 2
    rows = [slice(0, half), slice(half, 2 * half)]
    yab = [(jnp.dot(a_ref[r, :], wa_ref[...], preferred_element_type=jnp.float32),
            jnp.dot(b_ref[r, :], wb_ref[...], preferred_element_type=jnp.float32)) for r in rows]
    h2 = []
    for r, (ya, yb) in zip(rows, yab):
        gate = gate_ref[r, :].astype(jnp.float32)
        y = (gate[:, :D_MODEL] * ya + gate[:, D_MODEL:] * yb).astype(jnp.bfloat16)
        h2.append(h_ref[r, :] + jnp.dot(y, wout_ref[...], preferred_element_type=jnp.float32))
    h2 = jnp.concatenate(h2, axis=0)
    h3 = _swiglu_half_step(h2, norm2_ref[...], wg_ref, wu_ref, wd_ref)
    o_ref[...] = _rms(h3, normf_ref[...])


def _const_spec(shape):
    return pl.BlockSpec(shape, lambda *_: (0,) * len(shape), pipeline_mode=pl.Buffered(1))


def _tc_params(n_axes):
    return pltpu.CompilerParams(dimension_semantics=("arbitrary",) * n_axes,
                                vmem_limit_bytes=VMEM_LIMIT_BYTES)


def kernel(x, ffn1_norm, ffn1_w_gate, ffn1_w_up, ffn1_w_down, mix_norm, w_in, b_gate, lambda_q1, lambda_k1, lambda_q2, lambda_k2, diff_subln, w_branch_diff, w_branch_sb, w_out, ffn2_norm, ffn2_w_gate, ffn2_w_up, ffn2_w_down, final_norm):
    B, S, D = x.shape
    T = B * S
    f32, bf16 = jnp.float32, jnp.bfloat16
    xt = x.reshape(T, D)
    row = lambda v: v.reshape(1, -1).astype(f32)

    tok_spec = pl.BlockSpec((FFN_TM, D), lambda t: (t, 0))
    hbm_spec = pl.BlockSpec(memory_space=pl.ANY)
    ffn_weight_scratch = [pltpu.VMEM((D, D_FF), bf16), pltpu.VMEM((D, D_FF), bf16),
                          pltpu.VMEM((D_FF, D), bf16)]
    ffn_stage_scratch = [pltpu.VMEM((STAGE_SLOTS, STAGE_WIDE_ROWS, D_FF), f32),
                         pltpu.VMEM((STAGE_SLOTS, STAGE_TALL_ROWS, D), f32),
                         pltpu.SemaphoreType.DMA((STAGE_SLOTS,))]

    def row_block_spec(w, n_steps, step_of):
        n_rows, n_cols = w.shape
        n_blocks = max(n for n in range(1, n_steps + 1)
                       if n_steps % n == 0 and n_rows % (16 * n) == 0)
        return pl.BlockSpec((n_rows // n_blocks, n_cols),
                            lambda *idx: ((step_of(*idx) * n_blocks) // n_steps, 0))

    win_side_spec = row_block_spec(w_in[0], T // FFN_TM, lambda t: t)
    h1, w_in_bf16 = pl.pallas_call(
        _ffn1_kernel,
        grid=(T // FFN_TM,),
        in_specs=[tok_spec, _const_spec((1, D)), hbm_spec, hbm_spec, hbm_spec, win_side_spec],
        out_specs=[tok_spec, win_side_spec],
        out_shape=[jax.ShapeDtypeStruct((T, D), f32), jax.ShapeDtypeStruct(w_in[0].shape, bf16)],
        scratch_shapes=ffn_weight_scratch + ffn_stage_scratch,
        compiler_params=_tc_params(1),
        name="ffn1",
    )(xt, row(ffn1_norm[0]), ffn1_w_gate[0], ffn1_w_up[0], ffn1_w_down[0], w_in[0])

    rowscale = jnp.ones((QVT_W,), f32)
    rowscale = rowscale.at[0:ATT_W].set(DA_QK_DIM ** -0.5 * LOG2E)
    rowscale = rowscale.at[ATT_W:2 * ATT_W].set(HEAD_W ** -0.5 * LOG2E)
    k, qvt, gates = pl.pallas_call(
        _in_proj_kernel,
        grid=(T // PROJ_TM,),
        in_specs=[pl.BlockSpec((PROJ_TM, D), lambda t: (t, 0)), _const_spec((1, D)),
                  _const_spec(w_in[0].shape), _const_spec((QVT_W, 1)), _const_spec((1, GATE_W))],
        out_specs=[pl.BlockSpec((PROJ_TM, K_W), lambda t: (t, 0)),
                   pl.BlockSpec((QVT_W, PROJ_TM), lambda t: (0, t)),
                   pl.BlockSpec((PROJ_TM, GATE_W), lambda t: (t, 0))],
        out_shape=[jax.ShapeDtypeStruct((T, K_W), bf16),
                   jax.ShapeDtypeStruct((QVT_W, T), bf16),
                   jax.ShapeDtypeStruct((T, GATE_W), bf16)],
        scratch_shapes=[pltpu.VMEM((QVT_W, D), bf16)],
        compiler_params=_tc_params(1),
        name="in_proj",
    )(h1, row(mix_norm[0]), w_in_bf16, rowscale.reshape(-1, 1), row(b_gate[0]))
    k3 = k.reshape(B, S, K_W)

    def q_spec(slab, tq):
        return pl.BlockSpec((HEAD_W, tq),
                            lambda b, h, i: (slab * N_HEADS + h, b * (S // tq) + i))

    def k_spec(slab):
        return pl.BlockSpec((1, S, HEAD_W), lambda b, h, i: (b, 0, slab * N_HEADS + h))

    def vt_spec(slab):
        return pl.BlockSpec((HEAD_W, S), lambda b, h, i: (slab * N_HEADS + h, b))

    def att_out_spec(tq):
        return pl.BlockSpec((1, tq, HEAD_W), lambda b, h, i: (b, i, h))

    att_out_shape = jax.ShapeDtypeStruct((B, S, ATT_W), bf16)
    smem_spec = pl.BlockSpec(memory_space=pltpu.SMEM)

    lam = (jnp.exp(jnp.sum(lambda_q1[0].astype(f32) * lambda_k1[0].astype(f32)))
           - jnp.exp(jnp.sum(lambda_q2[0].astype(f32) * lambda_k2[0].astype(f32)))
           + LAMBDA_INIT).reshape(1)
    slopes = jnp.exp2(-8.0 * jnp.arange(1, N_HEADS + 1, dtype=f32) / N_HEADS)

    mix_weights = [w_branch_diff[0], w_branch_sb[0], w_out[0],
                   ffn2_w_gate[0], ffn2_w_up[0], ffn2_w_down[0]]
    nq_diff = S // ATT_TQ
    side_specs = [row_block_spec(w, B * N_HEADS * nq_diff,
                                 lambda b, h, i: (b * N_HEADS + h) * nq_diff + i)
                  for w in mix_weights]
    a, *mix_weights_bf16 = pl.pallas_call(
        _diff_attn_kernel,
        grid=(B, N_HEADS, nq_diff),
        in_specs=[smem_spec, smem_spec, q_spec(0, ATT_TQ), k_spec(0), vt_spec(2),
                  _const_spec((HEAD_W, 1))] + side_specs,
        out_specs=[att_out_spec(ATT_TQ)] + side_specs,
        out_shape=[att_out_shape] + [jax.ShapeDtypeStruct(w.shape, bf16) for w in mix_weights],
        scratch_shapes=[pltpu.VMEM((ATT_TK, ATT_TK), f32), pltpu.VMEM((ATT_TK, HEAD_W), bf16),
                        pltpu.VMEM((2, 2 * ATT_TQ // CHAIN_W, ATT_TK, CHAIN_W), f32),
                        pltpu.VMEM((1, 2 * ATT_TQ), f32), pltpu.VMEM((1, 2 * ATT_TQ), f32),
                        pltpu.VMEM((HEAD_W, 2 * ATT_TQ), f32)],
        compiler_params=_tc_params(3),
        name="diff_attn",
    )(slopes, lam, qvt, k3, qvt, diff_subln[0].reshape(-1, 1).astype(f32), *mix_weights)

    b = pl.pallas_call(
        _sb_attn_kernel,
        grid=(B, N_HEADS, S // SB_TQ),
        in_specs=[q_spec(1, SB_TQ), k_spec(1), vt_spec(3)],
        out_specs=att_out_spec(SB_TQ),
        out_shape=att_out_shape,
        scratch_shapes=[pltpu.VMEM((1, SB_TQ), f32), pltpu.VMEM((HEAD_W, SB_TQ), f32)],
        compiler_params=_tc_params(3),
        name="sb_attn",
    )(qvt, k3, qvt)

    out = pl.pallas_call(
        _mix_ffn2_kernel,
        grid=(T // FFN_TM,),
        in_specs=[tok_spec,
                  pl.BlockSpec((FFN_TM, ATT_W), lambda t: (t, 0)),
                  pl.BlockSpec((FFN_TM, ATT_W), lambda t: (t, 0)),
                  pl.BlockSpec((FFN_TM, GATE_W), lambda t: (t, 0)),
                  _const_spec((ATT_W, D)), _const_spec((ATT_W, D)),
                  _const_spec((D, D)), _const_spec((1, D)), _const_spec((D, D_FF)),
                  _const_spec((D, D_FF)), _const_spec((D_FF, D)), _const_spec((1, D))],
        out_specs=tok_spec,
        out_shape=jax.ShapeDtypeStruct((T, D), f32),
        compiler_params=_tc_params(1),
        name="mix_ffn2",
    )(h1, a.reshape(T, -1), b.reshape(T, -1), gates, *mix_weights_bf16[:3],
      row(ffn2_norm[0]), *mix_weights_bf16[3:], row(final_norm))
    return out.reshape(B, S, D)
```

```python
import math

import jax
import jax.numpy as jnp
from jax import lax
from jax.experimental import pallas as pl
from jax.experimental.pallas import tpu as pltpu

D_MODEL = 1024
D_FF = 2816
N_HEADS = 4
HEAD_W = 128
DA_QK_DIM = 64
ATT_W = N_HEADS * HEAD_W
K_W = 2 * ATT_W
QVT_W = 4 * ATT_W
GATE_W = 2 * D_MODEL
NORM_EPS = 1e-5
LAMBDA_INIT = 0.8 - 0.6 * math.exp(-0.3 * 0)
LOG2E = 1.0 / math.log(2.0)

VMEM_LIMIT_BYTES = 56 * 1024 * 1024

STAGE_SLOTS = 3
STAGE_WIDE_ROWS = 128
STAGE_TALL_ROWS = 352

FFN_TM = 512
PROJ_TM = 512
ATT_TQ = 2048
ATT_TK = 512
SB_TQ = 4096
CHAIN_W = 256
SB_DONE_LOG2 = 160.0
SUM_ROWS = 16
SLOPE_TERMS = 3
BF16_EXACT_INT = 256

_NT = (((1,), (1,)), ((), ()))


def _rms(x, g):
    ms = jnp.mean(x * x, axis=-1, keepdims=True)
    return x * lax.rsqrt(ms + NORM_EPS) * g


def _swiglu_half_step(x, norm_g, wg_ref, wu_ref, wd_ref):
    halves = jnp.split(x, 2, axis=0)
    xn = [_rms(h, norm_g).astype(jnp.bfloat16) for h in halves]
    gu = [(jnp.dot(n, wg_ref[...], preferred_element_type=jnp.float32),
           jnp.dot(n, wu_ref[...], preferred_element_type=jnp.float32)) for n in xn]
    out = []
    for h, (g, u) in zip(halves, gu):
        hact = (g * jax.nn.sigmoid(g) * u).astype(jnp.bfloat16)
        out.append(h + 0.5 * jnp.dot(hact, wd_ref[...], preferred_element_type=jnp.float32))
    return jnp.concatenate(out, axis=0)


def _stage_weight(src_hbm, dst_ref, stage_ref, sem_ref):
    n_slots, rows, _ = stage_ref.shape
    n_rows = src_hbm.shape[0]
    assert n_rows % rows == 0 and src_hbm.shape[1] == stage_ref.shape[2]
    n_slabs = n_rows // rows

    def slab_copy(c):
        return pltpu.make_async_copy(src_hbm.at[pl.ds(c * rows, rows)],
                                     stage_ref.at[c % n_slots], sem_ref.at[c % n_slots])

    for c in range(min(n_slots - 1, n_slabs)):
        slab_copy(c).start()
    for c in range(n_slabs):
        if c + n_slots - 1 < n_slabs:
            slab_copy(c + n_slots - 1).start()
        slab_copy(c).wait()
        dst_ref[c * rows:(c + 1) * rows, :] = stage_ref[c % n_slots].astype(dst_ref.dtype)


def _ffn1_kernel(x_ref, norm_ref, wg_hbm, wu_hbm, wd_hbm, win32_ref, o_ref, win16_ref,
                 wg_ref, wu_ref, wd_ref, stage_wide, stage_tall, sem):
    @pl.when(pl.program_id(0) == 0)
    def _():
        _stage_weight(wg_hbm, wg_ref, stage_wide, sem)
        _stage_weight(wu_hbm, wu_ref, stage_wide, sem)
        _stage_weight(wd_hbm, wd_ref, stage_tall, sem)

    win16_ref[...] = win32_ref[...].astype(win16_ref.dtype)
    o_ref[...] = _swiglu_half_step(x_ref[...], norm_ref[...], wg_ref, wu_ref, wd_ref)


def _in_proj_kernel(h_ref, norm_ref, win_ref, rowscale_ref, bgate_ref,
                    k_ref, qvt_ref, gate_ref, wqvt_ref):
    w = ATT_W

    @pl.when(pl.program_id(0) == 0)
    def _():
        for piece, src_block in enumerate((0, 3, 2, 5)):
            cols = win_ref[:, src_block * w:(src_block + 1) * w].astype(jnp.float32)
            wqvt_ref[piece * w:(piece + 1) * w, :] = cols.T.astype(wqvt_ref.dtype)

    half = h_ref.shape[0] // 2
    rows = [slice(0, half), slice(half, 2 * half)]
    n = [_rms(h_ref[r, :], norm_ref[...]).astype(jnp.bfloat16) for r in rows]
    g = [jnp.dot(nh, win_ref[:, 6 * w:], preferred_element_type=jnp.float32) for nh in n]
    for r, gh in zip(rows, g):
        gate_ref[r, :] = jax.nn.sigmoid(gh + bgate_ref[...]).astype(jnp.bfloat16)
    for r, nh in zip(rows, n):
        for piece, src_block in enumerate((1, 4)):
            k_ref[r, piece * w:(piece + 1) * w] = jnp.dot(
                nh, win_ref[:, src_block * w:(src_block + 1) * w],
                preferred_element_type=jnp.float32).astype(jnp.bfloat16)
    for r, nh in zip(rows, n):
        qvt = lax.dot_general(wqvt_ref[...], nh, _NT, preferred_element_type=jnp.float32)
        qvt_ref[:, r] = (qvt * rowscale_ref[...]).astype(jnp.bfloat16)


def _emit_pipelined(stages, n):
    state = [dict() for _ in range(n)]
    for step in range(n + len(stages) - 1):
        for s, stage in enumerate(stages):
            t = step - s
            if 0 <= t < n:
                stage(t, state[t])


def _diff_attn_kernel(slope_ref, lam_ref, q_ref, k_ref, vt_ref, subln_ref, *refs):
    tq, tk, cw = ATT_TQ, ATT_TK, CHAIN_W
    per_map = tq // cw
    n_chains = 2 * per_map
    n_diag = tq // tk
    assert n_diag % 2 == 0 and tk == 2 * cw and tk <= 2 * BF16_EXACT_INT
    h = pl.program_id(1)
    i = pl.program_id(2)
    slope = slope_ref[h] * LOG2E
    lam = lam_ref[0]

    n_side = (len(refs) - 7) // 2
    w32_refs, o_ref, w16_refs = refs[:n_side], refs[n_side], refs[n_side + 1:2 * n_side + 1]
    mask_ref, kfeat_ref, s_ref, m_ref, l_ref, acc_ref = refs[2 * n_side + 1:]

    @pl.when(jnp.logical_and(pl.program_id(0) == 0, jnp.logical_and(h == 0, i == 0)))
    def _():
        krow = lax.broadcasted_iota(jnp.int32, (tk, tk), 0)
        qcol = lax.broadcasted_iota(jnp.int32, (tk, tk), 1)
        mask_ref[...] = jnp.where(qcol >= krow, 0.0, -jnp.inf)
        kpos = lax.broadcasted_iota(jnp.int32, (tk, HEAD_W), 0)
        klane = lax.broadcasted_iota(jnp.int32, (tk, HEAD_W), 1)
        k_hi = jnp.where(kpos >= BF16_EXACT_INT, BF16_EXACT_INT, 0)
        kfeat_ref[...] = jnp.where(klane < SLOPE_TERMS, k_hi,
                                   jnp.where(klane < 2 * SLOPE_TERMS, kpos - k_hi, 0)
                                   ).astype(jnp.float32).astype(jnp.bfloat16)

    qt = q_ref[...]
    chan = lax.broadcasted_iota(jnp.int32, (HEAD_W, tq), 0)
    zero = jnp.zeros_like(qt)
    q_maps = (jnp.where(chan < DA_QK_DIM, qt, zero), jnp.where(chan >= DA_QK_DIM, qt, zero))

    sl = jnp.full((HEAD_W, cw), slope, jnp.float32)
    hi = sl.astype(jnp.bfloat16).astype(jnp.float32)
    mid = (sl - hi).astype(jnp.bfloat16).astype(jnp.float32)
    lo = sl - hi - mid
    frow = lax.broadcasted_iota(jnp.int32, (HEAD_W, cw), 0)
    part = frow % SLOPE_TERMS
    q_feat = jnp.where(frow < 2 * SLOPE_TERMS,
                       jnp.where(part == 0, hi, jnp.where(part == 1, mid, lo)),
                       0.0).astype(jnp.bfloat16)
    q_chain = [jnp.concatenate(
        [q_maps[c // per_map][:, (c % per_map) * cw:(c % per_map + 1) * cw], q_feat], axis=0)
        for c in range(n_chains)]

    m_ref[...] = jnp.full_like(m_ref, -jnp.inf)
    l_ref[...] = jnp.zeros_like(l_ref)
    acc_ref[...] = jnp.zeros_like(acc_ref)

    def chain_mode(c, d):
        if d is None:
            return "full"
        q_lo = (c % per_map) * cw
        if q_lo + cw <= d * tk:
            return "skip"
        if q_lo >= (d + 1) * tk:
            return "full"
        return q_lo - d * tk

    def visible_keys(c, d):
        return cw if chain_mode(c, d) == 0 else tk

    def score_stage(slot, j, d=None):
        chains = [c for c in range(n_chains) if chain_mode(c, d) != "skip"]
        kb = k_ref[0, pl.ds(pl.multiple_of(j * tk, tk), tk), :]
        kb = jnp.concatenate([kb, kfeat_ref[...]], axis=1)

        def scores(t, st):
            if t < len(chains):
                c = chains[t]
                nk = visible_keys(c, d)
                s_ref[slot, c, :nk] = jnp.dot(kb[:nk], q_chain[c],
                                              preferred_element_type=jnp.float32)

        return len(chains), scores

    def scores_to(slot, j, d=None):
        n, scores = score_stage(slot, j, d)
        for t in range(n):
            scores(t, None)

    def consume(slot, j, d=None, ahead=None):
        chains = [c for c in range(n_chains) if chain_mode(c, d) != "skip"]
        vtb = vt_ref[:, pl.ds(pl.multiple_of(j * tk, tk), tk)]
        vtb = jnp.concatenate([vtb, jnp.ones((SUM_ROWS, tk), vtb.dtype)], axis=0)
        shift = -slope * (i * tq - j * tk).astype(jnp.float32)

        def column_max(t, st):
            if t >= len(chains):
                return
            c = chains[t]
            nk = visible_keys(c, d)
            s = s_ref[slot, c, :nk]
            mode = chain_mode(c, d)
            if mode != "full":
                s = s + mask_ref[:nk, mode:mode + cw]
                s_ref[slot, c, :nk] = s
            st["cmax"] = jnp.max(s, axis=0, keepdims=True) + shift

        def softmax_pv(t, st):
            if t >= len(chains):
                return
            c = chains[t]
            nk = visible_keys(c, d)
            lanes = slice(c * cw, (c + 1) * cw)
            m_prev = m_ref[:, lanes]
            m_new = jnp.maximum(m_prev, st.pop("cmax"))
            st["alpha"] = jnp.exp2(m_prev - m_new)
            p = jnp.exp2(s_ref[slot, c, :nk] - (m_new - shift))
            m_ref[:, lanes] = m_new
            st["pv"] = jnp.dot(vtb[:, :nk], p.astype(jnp.bfloat16),
                               preferred_element_type=jnp.float32)

        def accumulate(t, st):
            if t >= len(chains):
                return
            c = chains[t]
            lanes = slice(c * cw, (c + 1) * cw)
            alpha, pv = st.pop("alpha"), st.pop("pv")
            acc_ref[:, lanes] = alpha * acc_ref[:, lanes] + pv[:HEAD_W]
            l_ref[:, lanes] = alpha * l_ref[:, lanes] + pv[HEAD_W:HEAD_W + 1]

        n_ahead, ahead_stage = ahead if ahead is not None else (0, None)
        stages = (column_max, softmax_pv, accumulate)
        if ahead is not None:
            stages = (ahead_stage,) + stages
        _emit_pipelined(stages, max(len(chains), n_ahead))

    def step(slot, j):
        consume(slot, j, ahead=score_stage(1 - slot, j + 1))

    scores_to(0, 0)
    for w32, w16 in zip(w32_refs, w16_refs):
        w16[...] = w32[...].astype(w16.dtype)

    def pair(jj, carry):
        step(0, 2 * jj)
        step(1, 2 * jj + 1)
        return carry

    first_diag = n_diag * i
    lax.fori_loop(0, first_diag // 2, pair, 0)
    for d in range(n_diag):
        ahead = score_stage((d + 1) % 2, first_diag + d + 1, d + 1) if d + 1 < n_diag else None
        consume(d % 2, first_diag + d, d, ahead=ahead)

    o = acc_ref[...] / l_ref[...]
    a = o[:, :tq] - lam * o[:, tq:]
    ms = jnp.mean(a * a, axis=0, keepdims=True)
    a = a * lax.rsqrt(ms + NORM_EPS) * subln_ref[...] * (1.0 - LAMBDA_INIT)
    o_ref[0] = a.T.astype(o_ref.dtype)


def _sb_attn_kernel(q_ref, k_ref, vt_ref, o_ref, c_ref, acc_ref):
    tq, cw = SB_TQ, CHAIN_W
    n_chains = tq // cw
    i = pl.program_id(2)
    qt = q_ref[...]
    q_chain = [qt[:, c * cw:(c + 1) * cw] for c in range(n_chains)]

    krow = lax.broadcasted_iota(jnp.int32, (cw, cw), 0)
    qcol = lax.broadcasted_iota(jnp.int32, (cw, cw), 1)
    strict = krow < qcol
    lrow = lax.broadcasted_iota(jnp.int32, (cw + SUM_ROWS, cw), 0)
    lcol = lax.broadcasted_iota(jnp.int32, (cw + SUM_ROWS, cw), 1)
    later = jnp.where(jnp.logical_or(lcol > lrow, lrow >= cw), 1.0, 0.0).astype(jnp.bfloat16)

    c_ref[...] = jnp.zeros_like(c_ref)
    acc_ref[...] = jnp.zeros_like(acc_ref)

    def run_pieces(pieces):
        def scores(t, st):
            sub, c, _ = pieces[t]
            start = pl.multiple_of(sub * cw, cw)
            st["z"] = jnp.dot(k_ref[0, pl.ds(start, cw), :], q_chain[c],
                              preferred_element_type=jnp.float32)

        def suffix(t, st):
            _, _, triangular = pieces[t]
            z = st.pop("z")
            u = jnp.maximum(z, 0.0) + jnp.log2(1.0 + jnp.exp2(-jnp.abs(z)))
            st["log_sig"] = z - u
            if triangular:
                u = jnp.where(strict, u, 0.0)
            st["tail"] = jnp.dot(later, u.astype(jnp.bfloat16),
                                 preferred_element_type=jnp.float32)

        def weights_pv(t, st):
            sub, _, triangular = pieces[t]
            start = pl.multiple_of(sub * cw, cw)
            tail = st.pop("tail")
            st["usum"] = tail[cw:cw + 1]
            a = jnp.exp2(st.pop("log_sig") - tail[:cw])
            if triangular:
                a = jnp.where(strict, a, 0.0)
            st["pv"] = jnp.dot(vt_ref[:, pl.ds(start, cw)], a.astype(jnp.bfloat16),
                               preferred_element_type=jnp.float32)

        def accumulate(t, st):
            _, c, _ = pieces[t]
            lanes = slice(c * cw, (c + 1) * cw)
            carry = c_ref[:, lanes]
            acc_ref[:, lanes] += st.pop("pv") * jnp.exp2(-carry)
            c_ref[:, lanes] = carry + st.pop("usum")

        _emit_pipelined((scores, suffix, weights_pv, accumulate), len(pieces))

    diag = [n_chains * i + c for c in range(n_chains)]
    head = [(diag[c], c, True) for c in reversed(range(n_chains))]
    second = [(diag[c] - 1, c, False) for c in reversed(range(n_chains))]

    @pl.when(i == 0)
    def _():
        run_pieces(head + [p for p in second if p[1] > 0])

    @pl.when(i > 0)
    def _():
        run_pieces(head + second)

    def unfinished(c, depth):
        lanes = slice(c * cw, (c + 1) * cw)
        return jnp.logical_and(diag[c] - depth >= 0,
                               jnp.min(c_ref[:, lanes]) < SB_DONE_LOG2)

    def any_unfinished(depth):
        go = unfinished(0, depth)
        for c in range(1, n_chains):
            go = jnp.logical_or(go, unfinished(c, depth))
        return go

    def body(carry):
        depth, _ = carry
        for c in range(n_chains):
            @pl.when(unfinished(c, depth))
            def _():
                run_pieces([(diag[c] - depth, c, False)])
        return depth + 1, any_unfinished(depth + 1)

    lax.while_loop(lambda carry: carry[1], body, (jnp.int32(2), any_unfinished(2)))
    o_ref[0] = acc_ref[...].T.astype(o_ref.dtype)


def _mix_ffn2_kernel(h_ref, a_ref, b_ref, gate_ref, wa_ref, wb_ref, wout_ref,
                     norm2_ref, wg_ref, wu_ref, wd_ref, normf_ref, o_ref):
    half = h_ref.shape[0] // 2
    rows = [slice(0, half), slice(half, 2 * half)]
    yab = [(jnp.dot(a_ref[r, :], wa_ref[...], preferred_element_type=jnp.float32),
            jnp.dot(b_ref[r, :], wb_ref[...], preferred_element_type=jnp.float32)) for r in rows]
    h2 = []
    for r, (ya, yb) in zip(rows, yab):
        gate = gate_ref[r, :].astype(jnp.float32)
        y = (gate[:, :D_MODEL] * ya + gate[:, D_MODEL:] * yb).astype(jnp.bfloat16)
        h2.append(h_ref[r, :] + jnp.dot(y, wout_ref[...], preferred_element_type=jnp.float32))
    h2 = jnp.concatenate(h2, axis=0)
    h3 = _swiglu_half_step(h2, norm2_ref[...], wg_ref, wu_ref, wd_ref)
    o_ref[...] = _rms(h3, normf_ref[...])


def _const_spec(shape):
    return pl.BlockSpec(shape, lambda *_: (0,) * len(shape), pipeline_mode=pl.Buffered(1))


def _tc_params(n_axes):
    return pltpu.CompilerParams(dimension_semantics=("arbitrary",) * n_axes,
                                vmem_limit_bytes=VMEM_LIMIT_BYTES)


def kernel(x, ffn1_norm, ffn1_w_gate, ffn1_w_up, ffn1_w_down, mix_norm, w_in, b_gate, lambda_q1, lambda_k1, lambda_q2, lambda_k2, diff_subln, w_branch_diff, w_branch_sb, w_out, ffn2_norm, ffn2_w_gate, ffn2_w_up, ffn2_w_down, final_norm):
    B, S, D = x.shape
    T = B * S
    assert D == D_MODEL and w_in.shape[1:] == (D_MODEL, 6 * ATT_W + GATE_W)
    assert S % ATT_TQ == 0 and S % SB_TQ == 0 and T % FFN_TM == 0 and T % PROJ_TM == 0
    f32, bf16 = jnp.float32, jnp.bfloat16
    xt = x.reshape(T, D)
    row = lambda v: v.reshape(1, -1).astype(f32)

    tok_spec = pl.BlockSpec((FFN_TM, D), lambda t: (t, 0))
    hbm_spec = pl.BlockSpec(memory_space=pl.ANY)
    ffn_weight_scratch = [pltpu.VMEM((D, D_FF), bf16), pltpu.VMEM((D, D_FF), bf16),
                          pltpu.VMEM((D_FF, D), bf16)]
    ffn_stage_scratch = [pltpu.VMEM((STAGE_SLOTS, STAGE_WIDE_ROWS, D_FF), f32),
                         pltpu.VMEM((STAGE_SLOTS, STAGE_TALL_ROWS, D), f32),
                         pltpu.SemaphoreType.DMA((STAGE_SLOTS,))]

    def row_block_spec(w, n_steps, step_of):
        n_rows, n_cols = w.shape
        n_blocks = max(n for n in range(1, n_steps + 1)
                       if n_steps % n == 0 and n_rows % (16 * n) == 0)
        return pl.BlockSpec((n_rows // n_blocks, n_cols),
                            lambda *idx: ((step_of(*idx) * n_blocks) // n_steps, 0))

    win_side_spec = row_block_spec(w_in[0], T // FFN_TM, lambda t: t)
    h1, w_in_bf16 = pl.pallas_call(
        _ffn1_kernel,
        grid=(T // FFN_TM,),
        in_specs=[tok_spec, _const_spec((1, D)), hbm_spec, hbm_spec, hbm_spec, win_side_spec],
        out_specs=[tok_spec, win_side_spec],
        out_shape=[jax.ShapeDtypeStruct((T, D), f32), jax.ShapeDtypeStruct(w_in[0].shape, bf16)],
        scratch_shapes=ffn_weight_scratch + ffn_stage_scratch,
        compiler_params=_tc_params(1),
        name="ffn1",
    )(xt, row(ffn1_norm[0]), ffn1_w_gate[0], ffn1_w_up[0], ffn1_w_down[0], w_in[0])

    rowscale = jnp.ones((QVT_W,), f32)
    rowscale = rowscale.at[0:ATT_W].set(DA_QK_DIM ** -0.5 * LOG2E)
    rowscale = rowscale.at[ATT_W:2 * ATT_W].set(HEAD_W ** -0.5 * LOG2E)
    k, qvt, gates = pl.pallas_call(
        _in_proj_kernel,
        grid=(T // PROJ_TM,),
        in_specs=[pl.BlockSpec((PROJ_TM, D), lambda t: (t, 0)), _const_spec((1, D)),
                  _const_spec(w_in[0].shape), _const_spec((QVT_W, 1)), _const_spec((1, GATE_W))],
        out_specs=[pl.BlockSpec((PROJ_TM, K_W), lambda t: (t, 0)),
                   pl.BlockSpec((QVT_W, PROJ_TM), lambda t: (0, t)),
                   pl.BlockSpec((PROJ_TM, GATE_W), lambda t: (t, 0))],
        out_shape=[jax.ShapeDtypeStruct((T, K_W), bf16),
                   jax.ShapeDtypeStruct((QVT_W, T), bf16),
                   jax.ShapeDtypeStruct((T, GATE_W), bf16)],
        scratch_shapes=[pltpu.VMEM((QVT_W, D), bf16)],
        compiler_params=_tc_params(1),
        name="in_proj",
    )(h1, row(mix_norm[0]), w_in_bf16, rowscale.reshape(-1, 1), row(b_gate[0]))
    k3 = k.reshape(B, S, K_W)

    def q_spec(slab, tq):
        return pl.BlockSpec((HEAD_W, tq),
                            lambda b, h, i: (slab * N_HEADS + h, b * (S // tq) + i))

    def k_spec(slab):
        return pl.BlockSpec((1, S, HEAD_W), lambda b, h, i: (b, 0, slab * N_HEADS + h))

    def vt_spec(slab):
        return pl.BlockSpec((HEAD_W, S), lambda b, h, i: (slab * N_HEADS + h, b))

    def att_out_spec(tq):
        return pl.BlockSpec((1, tq, HEAD_W), lambda b, h, i: (b, i, h))

    att_out_shape = jax.ShapeDtypeStruct((B, S, ATT_W), bf16)
    smem_spec = pl.BlockSpec(memory_space=pltpu.SMEM)

    lam = (jnp.exp(jnp.sum(lambda_q1[0].astype(f32) * lambda_k1[0].astype(f32)))
           - jnp.exp(jnp.sum(lambda_q2[0].astype(f32) * lambda_k2[0].astype(f32)))
           + LAMBDA_INIT).reshape(1)
    slopes = jnp.exp2(-8.0 * jnp.arange(1, N_HEADS + 1, dtype=f32) / N_HEADS)

    mix_weights = [w_branch_diff[0], w_branch_sb[0], w_out[0],
                   ffn2_w_gate[0], ffn2_w_up[0], ffn2_w_down[0]]
    nq_diff = S // ATT_TQ
    side_specs = [row_block_spec(w, B * N_HEADS * nq_diff,
                                 lambda b, h, i: (b * N_HEADS + h) * nq_diff + i)
                  for w in mix_weights]
    a, *mix_weights_bf16 = pl.pallas_call(
        _diff_attn_kernel,
        grid=(B, N_HEADS, nq_diff),
        in_specs=[smem_spec, smem_spec, q_spec(0, ATT_TQ), k_spec(0), vt_spec(2),
                  _const_spec((HEAD_W, 1))] + side_specs,
        out_specs=[att_out_spec(ATT_TQ)] + side_specs,
        out_shape=[att_out_shape] + [jax.ShapeDtypeStruct(w.shape, bf16) for w in mix_weights],
        scratch_shapes=[pltpu.VMEM((ATT_TK, ATT_TK), f32), pltpu.VMEM((ATT_TK, HEAD_W), bf16),
                        pltpu.VMEM((2, 2 * ATT_TQ // CHAIN_W, ATT_TK, CHAIN_W), f32),
                        pltpu.VMEM((1, 2 * ATT_TQ), f32), pltpu.VMEM((1, 2 * ATT_TQ), f32),
                        pltpu.VMEM((HEAD_W, 2 * ATT_TQ), f32)],
        compiler_params=_tc_params(3),
        name="diff_attn",
    )(slopes, lam, qvt, k3, qvt, diff_subln[0].reshape(-1, 1).astype(f32), *mix_weights)

    b = pl.pallas_call(
        _sb_attn_kernel,
        grid=(B, N_HEADS, S // SB_TQ),
        in_specs=[q_spec(1, SB_TQ), k_spec(1), vt_spec(3)],
        out_specs=att_out_spec(SB_TQ),
        out_shape=att_out_shape,
        scratch_shapes=[pltpu.VMEM((1, SB_TQ), f32), pltpu.VMEM((HEAD_W, SB_TQ), f32)],
        compiler_params=_tc_params(3),
        name="sb_attn",
    )(qvt, k3, qvt)

    out = pl.pallas_call(
        _mix_ffn2_kernel,
        grid=(T // FFN_TM,),
        in_specs=[tok_spec,
                  pl.BlockSpec((FFN_TM, ATT_W), lambda t: (t, 0)),
                  pl.BlockSpec((FFN_TM, ATT_W), lambda t: (t, 0)),
                  pl.BlockSpec((FFN_TM, GATE_W), lambda t: (t, 0)),
                  _const_spec((ATT_W, D)), _const_spec((ATT_W, D)),
                  _const_spec((D, D)), _const_spec((1, D)), _const_spec((D, D_FF)),
                  _const_spec((D, D_FF)), _const_spec((D_FF, D)), _const_spec((1, D))],
        out_specs=tok_spec,
        out_shape=jax.ShapeDtypeStruct((T, D), f32),
        compiler_params=_tc_params(1),
        name="mix_ffn2",
    )(h1, a.reshape(T, -1), b.reshape(T, -1), gates, *mix_weights_bf16[:3],
      row(ffn2_norm[0]), *mix_weights_bf16[3:], row(final_norm))
    return out.reshape(B, S, D)
```

```python
import math

import jax
import jax.numpy as jnp
from jax import lax
from jax.experimental import pallas as pl
from jax.experimental.pallas import tpu as pltpu

D_MODEL = 1024
D_FF = 2816
N_HEADS = 4
HEAD_W = 128
DA_QK_DIM = 64
ATT_W = N_HEADS * HEAD_W
K_W = 2 * ATT_W
QVT_W = 4 * ATT_W
GATE_W = 2 * D_MODEL
NORM_EPS = 1e-5
LAMBDA_INIT = 0.8 - 0.6 * math.exp(-0.3 * 0)
LOG2E = 1.0 / math.log(2.0)

VMEM_LIMIT_BYTES = 56 * 1024 * 1024

STAGE_SLOTS = 3
STAGE_WIDE_ROWS = 128
STAGE_TALL_ROWS = 352

FFN_TM = 512
PROJ_TM = 512
ATT_TQ = 2048
ATT_TK = 512
SB_TQ = 4096
CHAIN_W = 256
SB_DONE_LOG2 = 160.0
SUM_ROWS = 16
SLOPE_TERMS = 3
BF16_EXACT_INT = 256

_NT = (((1,), (1,)), ((), ()))


def _rms(x, g):
    ms = jnp.mean(x * x, axis=-1, keepdims=True)
    return x * lax.rsqrt(ms + NORM_EPS) * g


def _swiglu_half_step(x, norm_g, wg_ref, wu_ref, wd_ref):
    halves = jnp.split(x, 2, axis=0)
    xn = [_rms(h, norm_g).astype(jnp.bfloat16) for h in halves]
    gu = [(jnp.dot(n, wg_ref[...], preferred_element_type=jnp.float32),
           jnp.dot(n, wu_ref[...], preferred_element_type=jnp.float32)) for n in xn]
    out = []
    for h, (g, u) in zip(halves, gu):
        hact = (g * jax.nn.sigmoid(g) * u).astype(jnp.bfloat16)
        out.append(h + 0.5 * jnp.dot(hact, wd_ref[...], preferred_element_type=jnp.float32))
    return jnp.concatenate(out, axis=0)


def _stage_weight(src_hbm, dst_ref, stage_ref, sem_ref):
    n_slots, rows, _ = stage_ref.shape
    n_rows = src_hbm.shape[0]
    assert n_rows % rows == 0 and src_hbm.shape[1] == stage_ref.shape[2]
    n_slabs = n_rows // rows

    def slab_copy(c):
        return pltpu.make_async_copy(src_hbm.at[pl.ds(c * rows, rows)],
                                     stage_ref.at[c % n_slots], sem_ref.at[c % n_slots])

    for c in range(min(n_slots - 1, n_slabs)):
        slab_copy(c).start()
    for c in range(n_slabs):
        if c + n_slots - 1 < n_slabs:
            slab_copy(c + n_slots - 1).start()
        slab_copy(c).wait()
        dst_ref[c * rows:(c + 1) * rows, :] = stage_ref[c % n_slots].astype(dst_ref.dtype)


def _ffn1_kernel(x_ref, norm_ref, wg_hbm, wu_hbm, wd_hbm, win32_ref, o_ref, win16_ref,
                 wg_ref, wu_ref, wd_ref, stage_wide, stage_tall, sem):
    @pl.when(pl.program_id(0) == 0)
    def _():
        _stage_weight(wg_hbm, wg_ref, stage_wide, sem)
        _stage_weight(wu_hbm, wu_ref, stage_wide, sem)
        _stage_weight(wd_hbm, wd_ref, stage_tall, sem)

    win16_ref[...] = win32_ref[...].astype(win16_ref.dtype)
    o_ref[...] = _swiglu_half_step(x_ref[...], norm_ref[...], wg_ref, wu_ref, wd_ref)


def _in_proj_kernel(h_ref, norm_ref, win_ref, rowscale_ref, bgate_ref,
                    k_ref, qvt_ref, gate_ref, wqvt_ref):
    w = ATT_W

    @pl.when(pl.program_id(0) == 0)
    def _():
        for piece, src_block in enumerate((0, 3, 2, 5)):
            cols = win_ref[:, src_block * w:(src_block + 1) * w].astype(jnp.float32)
            wqvt_ref[piece * w:(piece + 1) * w, :] = cols.T.astype(wqvt_ref.dtype)

    half = h_ref.shape[0] // 2
    rows = [slice(0, half), slice(half, 2 * half)]
    n = [_rms(h_ref[r, :], norm_ref[...]).astype(jnp.bfloat16) for r in rows]
    g = [jnp.dot(nh, win_ref[:, 6 * w:], preferred_element_type=jnp.float32) for nh in n]
    for r, gh in zip(rows, g):
        gate_ref[r, :] = jax.nn.sigmoid(gh + bgate_ref[...]).astype(jnp.bfloat16)
    for r, nh in zip(rows, n):
        for piece, src_block in enumerate((1, 4)):
            k_ref[r, piece * w:(piece + 1) * w] = jnp.dot(
                nh, win_ref[:, src_block * w:(src_block + 1) * w],
                preferred_element_type=jnp.float32).astype(jnp.bfloat16)
    for r, nh in zip(rows, n):
        qvt = lax.dot_general(wqvt_ref[...], nh, _NT, preferred_element_type=jnp.float32)
        qvt_ref[:, r] = (qvt * rowscale_ref[...]).astype(jnp.bfloat16)


def _emit_pipelined(stages, n):
    state = [dict() for _ in range(n)]
    for step in range(n + len(stages) - 1):
        for s, stage in enumerate(stages):
            t = step - s
            if 0 <= t < n:
                stage(t, state[t])


def _diff_attn_kernel(slope_ref, lam_ref, q0_ref, q1_ref, k0_ref, k1_ref, vt0_ref, vt1_ref,
                      subln_ref, *refs):
    tq, tk, cw = ATT_TQ, ATT_TK, CHAIN_W
    per_map = tq // cw
    n_chains = 2 * per_map
    n_diag = tq // tk
    assert n_diag % 2 == 0 and tk == 2 * cw and tk <= 2 * BF16_EXACT_INT
    hp = pl.program_id(1)
    i = pl.program_id(2)
    lam = lam_ref[0]

    n_side = (len(refs) - 7) // 2
    w32_refs, o_ref, w16_refs = refs[:n_side], refs[n_side], refs[n_side + 1:2 * n_side + 1]
    mask_ref, kfeat_ref, s_ref, m2_ref, l2_ref, acc2_ref = refs[2 * n_side + 1:]

    @pl.when(jnp.logical_and(pl.program_id(0) == 0, jnp.logical_and(hp == 0, i == 0)))
    def _():
        krow = lax.broadcasted_iota(jnp.int32, (tk, tk), 0)
        qcol = lax.broadcasted_iota(jnp.int32, (tk, tk), 1)
        mask_ref[...] = jnp.where(qcol >= krow, 0.0, -jnp.inf)
        kpos = lax.broadcasted_iota(jnp.int32, (tk, HEAD_W), 0)
        klane = lax.broadcasted_iota(jnp.int32, (tk, HEAD_W), 1)
        k_hi = jnp.where(kpos >= BF16_EXACT_INT, BF16_EXACT_INT, 0)
        kfeat_ref[...] = jnp.where(klane < SLOPE_TERMS, k_hi,
                                   jnp.where(klane < 2 * SLOPE_TERMS, kpos - k_hi, 0)
                                   ).astype(jnp.float32).astype(jnp.bfloat16)

    heads = [
        _diff_head(i, lam, slope_ref[2 * hp + hh] * LOG2E, q_ref, k_ref, vt_ref, subln_ref,
                   o_ref, hh * HEAD_W,
                   mask_ref, kfeat_ref, s_ref, m2_ref.at[hh], l2_ref.at[hh], acc2_ref.at[hh])
        for hh, (q_ref, k_ref, vt_ref) in enumerate(
            ((q0_ref, k0_ref, vt0_ref), (q1_ref, k1_ref, vt1_ref)))]

    def side_casts():
        for w32, w16 in zip(w32_refs, w16_refs):
            w16[...] = w32[...].astype(w16.dtype)

    heads[0]["run"](True, side_casts, heads[1]["score_stage"](0, 0))
    heads[1]["run"](False, None, None)


def _diff_head(i, lam, slope, q_ref, k_ref, vt_ref, subln_ref, o_ref, o_lane0,
               mask_ref, kfeat_ref, s_ref, m_ref, l_ref, acc_ref):
    tq, tk, cw = ATT_TQ, ATT_TK, CHAIN_W
    per_map = tq // cw
    n_chains = 2 * per_map
    n_diag = tq // tk

    qt = q_ref[...]
    chan = lax.broadcasted_iota(jnp.int32, (HEAD_W, tq), 0)
    zero = jnp.zeros_like(qt)
    q_maps = (jnp.where(chan < DA_QK_DIM, qt, zero), jnp.where(chan >= DA_QK_DIM, qt, zero))

    sl = jnp.full((HEAD_W, cw), slope, jnp.float32)
    hi = sl.astype(jnp.bfloat16).astype(jnp.float32)
    mid = (sl - hi).astype(jnp.bfloat16).astype(jnp.float32)
    lo = sl - hi - mid
    frow = lax.broadcasted_iota(jnp.int32, (HEAD_W, cw), 0)
    part = frow % SLOPE_TERMS
    q_feat = jnp.where(frow < 2 * SLOPE_TERMS,
                       jnp.where(part == 0, hi, jnp.where(part == 1, mid, lo)),
                       0.0).astype(jnp.bfloat16)
    q_chain = [jnp.concatenate(
        [q_maps[c // per_map][:, (c % per_map) * cw:(c % per_map + 1) * cw], q_feat], axis=0)
        for c in range(n_chains)]

    m_ref[...] = jnp.full_like(m_ref, -jnp.inf)
    l_ref[...] = jnp.zeros_like(l_ref)
    acc_ref[...] = jnp.zeros_like(acc_ref)

    def chain_mode(c, d):
        if d is None:
            return "full"
        q_lo = (c % per_map) * cw
        if q_lo + cw <= d * tk:
            return "skip"
        if q_lo >= (d + 1) * tk:
            return "full"
        return q_lo - d * tk

    def visible_keys(c, d):
        return cw if chain_mode(c, d) == 0 else tk

    def score_stage(slot, j, d=None):
        chains = [c for c in range(n_chains) if chain_mode(c, d) != "skip"]
        kb = k_ref[0, pl.ds(pl.multiple_of(j * tk, tk), tk), :]
        kb = jnp.concatenate([kb, kfeat_ref[...]], axis=1)

        def scores(t, st):
            if t < len(chains):
                c = chains[t]
                nk = visible_keys(c, d)
                s_ref[slot, c, :nk] = jnp.dot(kb[:nk], q_chain[c],
                                              preferred_element_type=jnp.float32)

        return len(chains), scores

    def scores_to(slot, j, d=None):
        n, scores = score_stage(slot, j, d)
        for t in range(n):
            scores(t, None)

    def consume(slot, j, d=None, ahead=None):
        chains = [c for c in range(n_chains) if chain_mode(c, d) != "skip"]
        vtb = vt_ref[:, pl.ds(pl.multiple_of(j * tk, tk), tk)]
        vtb = jnp.concatenate([vtb, jnp.ones((SUM_ROWS, tk), vtb.dtype)], axis=0)
        shift = -slope * (i * tq - j * tk).astype(jnp.float32)

        def column_max(t, st):
            if t >= len(chains):
                return
            c = chains[t]
            nk = visible_keys(c, d)
            s = s_ref[slot, c, :nk]
            mode = chain_mode(c, d)
            if mode != "full":
                s = s + mask_ref[:nk, mode:mode + cw]
                s_ref[slot, c, :nk] = s
            st["cmax"] = jnp.max(s, axis=0, keepdims=True) + shift

        def softmax_pv(t, st):
            if t >= len(chains):
                return
            c = chains[t]
            nk = visible_keys(c, d)
            lanes = slice(c * cw, (c + 1) * cw)
            m_prev = m_ref[:, lanes]
            m_new = jnp.maximum(m_prev, st.pop("cmax"))
            st["alpha"] = jnp.exp2(m_prev - m_new)
            p = jnp.exp2(s_ref[slot, c, :nk] - (m_new - shift))
            m_ref[:, lanes] = m_new
            st["pv"] = jnp.dot(vtb[:, :nk], p.astype(jnp.bfloat16),
                               preferred_element_type=jnp.float32)

        def accumulate(t, st):
            if t >= len(chains):
                return
            c = chains[t]
            lanes = slice(c * cw, (c + 1) * cw)
            alpha, pv = st.pop("alpha"), st.pop("pv")
            acc_ref[:, lanes] = alpha * acc_ref[:, lanes] + pv[:HEAD_W]
            l_ref[:, lanes] = alpha * l_ref[:, lanes] + pv[HEAD_W:HEAD_W + 1]

        n_ahead, ahead_stage = ahead if ahead is not None else (0, None)
        stages = (column_max, softmax_pv, accumulate)
        if ahead is not None:
            stages = (ahead_stage,) + stages
        _emit_pipelined(stages, max(len(chains), n_ahead))

    def step(slot, j):
        consume(slot, j, ahead=score_stage(1 - slot, j + 1))

    def pair(jj, carry):
        step(0, 2 * jj)
        step(1, 2 * jj + 1)
        return carry

    def run(own_prologue, after_prologue, ahead_last):
        if own_prologue:
            scores_to(0, 0)
        if after_prologue is not None:
            after_prologue()
        first_diag = n_diag * i
        lax.fori_loop(0, first_diag // 2, pair, 0)
        for d in range(n_diag):
            ahead = (score_stage((d + 1) % 2, first_diag + d + 1, d + 1)
                     if d + 1 < n_diag else ahead_last)
            consume(d % 2, first_diag + d, d, ahead=ahead)

        o = acc_ref[...] / l_ref[...]
        a = o[:, :tq] - lam * o[:, tq:]
        ms = jnp.mean(a * a, axis=0, keepdims=True)
        a = a * lax.rsqrt(ms + NORM_EPS) * subln_ref[...] * (1.0 - LAMBDA_INIT)
        o_ref[0, :, o_lane0:o_lane0 + HEAD_W] = a.T.astype(o_ref.dtype)

    return {"score_stage": score_stage, "run": run}


def _sb_attn_kernel(q0_ref, q1_ref, k0_ref, k1_ref, vt0_ref, vt1_ref, o_ref, c2_ref, acc2_ref):
    tq, cw = SB_TQ, CHAIN_W
    n_chains = tq // cw
    i = pl.program_id(2)
    k_refs, vt_refs = (k0_ref, k1_ref), (vt0_ref, vt1_ref)
    c_refs = (c2_ref.at[0], c2_ref.at[1])
    acc_refs = (acc2_ref.at[0], acc2_ref.at[1])
    q_chains = []
    for q_ref in (q0_ref, q1_ref):
        qt = q_ref[...]
        q_chains.append([qt[:, c * cw:(c + 1) * cw] for c in range(n_chains)])

    krow = lax.broadcasted_iota(jnp.int32, (cw, cw), 0)
    qcol = lax.broadcasted_iota(jnp.int32, (cw, cw), 1)
    strict = krow < qcol
    lrow = lax.broadcasted_iota(jnp.int32, (cw + SUM_ROWS, cw), 0)
    lcol = lax.broadcasted_iota(jnp.int32, (cw + SUM_ROWS, cw), 1)
    later = jnp.where(jnp.logical_or(lcol > lrow, lrow >= cw), 1.0, 0.0).astype(jnp.bfloat16)

    c2_ref[...] = jnp.zeros_like(c2_ref)
    acc2_ref[...] = jnp.zeros_like(acc2_ref)

    def run_pieces(pieces):
        def scores(t, st):
            hh, sub, c, _ = pieces[t]
            start = pl.multiple_of(sub * cw, cw)
            st["z"] = jnp.dot(k_refs[hh][0, pl.ds(start, cw), :], q_chains[hh][c],
                              preferred_element_type=jnp.float32)

        def suffix(t, st):
            _, _, _, triangular = pieces[t]
            z = st.pop("z")
            u = jnp.maximum(z, 0.0) + jnp.log2(1.0 + jnp.exp2(-jnp.abs(z)))
            st["log_sig"] = z - u
            if triangular:
                u = jnp.where(strict, u, 0.0)
            st["tail"] = jnp.dot(later, u.astype(jnp.bfloat16),
                                 preferred_element_type=jnp.float32)

        def weights_pv(t, st):
            hh, sub, _, triangular = pieces[t]
            start = pl.multiple_of(sub * cw, cw)
            tail = st.pop("tail")
            st["usum"] = tail[cw:cw + 1]
            a = jnp.exp2(st.pop("log_sig") - tail[:cw])
            if triangular:
                a = jnp.where(strict, a, 0.0)
            st["pv"] = jnp.dot(vt_refs[hh][:, pl.ds(start, cw)], a.astype(jnp.bfloat16),
                               preferred_element_type=jnp.float32)

        def accumulate(t, st):
            hh, _, c, _ = pieces[t]
            lanes = slice(c * cw, (c + 1) * cw)
            carry = c_refs[hh][:, lanes]
            acc_refs[hh][:, lanes] += st.pop("pv") * jnp.exp2(-carry)
            c_refs[hh][:, lanes] = carry + st.pop("usum")

        _emit_pipelined((scores, suffix, weights_pv, accumulate), len(pieces))

    diag = [n_chains * i + c for c in range(n_chains)]

    def head_pieces(first_block):
        pieces = []
        for hh in range(2):
            pieces += [(hh, diag[c], c, True) for c in reversed(range(n_chains))]
            pieces += [(hh, diag[c] - 1, c, False) for c in reversed(range(n_chains))
                       if not (first_block and c == 0)]
        return pieces

    @pl.when(i == 0)
    def _():
        run_pieces(head_pieces(True))

    @pl.when(i > 0)
    def _():
        run_pieces(head_pieces(False))

    def unfinished(hh, c, depth):
        lanes = slice(c * cw, (c + 1) * cw)
        return jnp.logical_and(diag[c] - depth >= 0,
                               jnp.min(c_refs[hh][:, lanes]) < SB_DONE_LOG2)

    def any_unfinished(depth):
        go = unfinished(0, 0, depth)
        for hh in range(2):
            for c in range(n_chains):
                go = jnp.logical_or(go, unfinished(hh, c, depth))
        return go

    def body(carry):
        depth, _ = carry
        for hh in range(2):
            for c in range(n_chains):
                @pl.when(unfinished(hh, c, depth))
                def _():
                    run_pieces([(hh, diag[c] - depth, c, False)])
        return depth + 1, any_unfinished(depth + 1)

    lax.while_loop(lambda carry: carry[1], body, (jnp.int32(2), any_unfinished(2)))
    for hh in range(2):
        o_ref[0, :, hh * HEAD_W:(hh + 1) * HEAD_W] = acc_refs[hh][...].T.astype(o_ref.dtype)


def _mix_ffn2_kernel(h_ref, a_ref, b_ref, gate_ref, wa_ref, wb_ref, wout_ref,
                     norm2_ref, wg_ref, wu_ref, wd_ref, normf_ref, o_ref):
    half = h_ref.shape[0] // 2
    rows = [slice(0, half), slice(half, 2 * half)]
    yab = [(jnp.dot(a_ref[r, :], wa_ref[...], preferred_element_type=jnp.float32),
            jnp.dot(b_ref[r, :], wb_ref[...], preferred_element_type=jnp.float32)) for r in rows]
    h2 = []
    for r, (ya, yb) in zip(rows, yab):
        gate = gate_ref[r, :].astype(jnp.float32)
        y = (gate[:, :D_MODEL] * ya + gate[:, D_MODEL:] * yb).astype(jnp.bfloat16)
        h2.append(h_ref[r, :] + jnp.dot(y, wout_ref[...], preferred_element_type=jnp.float32))
    h2 = jnp.concatenate(h2, axis=0)
    h3 = _swiglu_half_step(h2, norm2_ref[...], wg_ref, wu_ref, wd_ref)
    o_ref[...] = _rms(h3, normf_ref[...])


def _const_spec(shape):
    return pl.BlockSpec(shape, lambda *_: (0,) * len(shape), pipeline_mode=pl.Buffered(1))


def _tc_params(n_axes):
    return pltpu.CompilerParams(dimension_semantics=("arbitrary",) * n_axes,
                                vmem_limit_bytes=VMEM_LIMIT_BYTES)


def kernel(x, ffn1_norm, ffn1_w_gate, ffn1_w_up, ffn1_w_down, mix_norm, w_in, b_gate, lambda_q1, lambda_k1, lambda_q2, lambda_k2, diff_subln, w_branch_diff, w_branch_sb, w_out, ffn2_norm, ffn2_w_gate, ffn2_w_up, ffn2_w_down, final_norm):
    B, S, D = x.shape
    T = B * S
    assert D == D_MODEL and w_in.shape[1:] == (D_MODEL, 6 * ATT_W + GATE_W)
    assert S % ATT_TQ == 0 and S % SB_TQ == 0 and T % FFN_TM == 0 and T % PROJ_TM == 0
    f32, bf16 = jnp.float32, jnp.bfloat16
    xt = x.reshape(T, D)
    row = lambda v: v.reshape(1, -1).astype(f32)

    tok_spec = pl.BlockSpec((FFN_TM, D), lambda t: (t, 0))
    hbm_spec = pl.BlockSpec(memory_space=pl.ANY)
    ffn_weight_scratch = [pltpu.VMEM((D, D_FF), bf16), pltpu.VMEM((D, D_FF), bf16),
                          pltpu.VMEM((D_FF, D), bf16)]
    ffn_stage_scratch = [pltpu.VMEM((STAGE_SLOTS, STAGE_WIDE_ROWS, D_FF), f32),
                         pltpu.VMEM((STAGE_SLOTS, STAGE_TALL_ROWS, D), f32),
                         pltpu.SemaphoreType.DMA((STAGE_SLOTS,))]

    def row_block_spec(w, n_steps, step_of):
        n_rows, n_cols = w.shape
        n_blocks = max(n for n in range(1, n_steps + 1)
                       if n_steps % n == 0 and n_rows % (16 * n) == 0)
        return pl.BlockSpec((n_rows // n_blocks, n_cols),
                            lambda *idx: ((step_of(*idx) * n_blocks) // n_steps, 0))

    win_side_spec = row_block_spec(w_in[0], T // FFN_TM, lambda t: t)
    h1, w_in_bf16 = pl.pallas_call(
        _ffn1_kernel,
        grid=(T // FFN_TM,),
        in_specs=[tok_spec, _const_spec((1, D)), hbm_spec, hbm_spec, hbm_spec, win_side_spec],
        out_specs=[tok_spec, win_side_spec],
        out_shape=[jax.ShapeDtypeStruct((T, D), f32), jax.ShapeDtypeStruct(w_in[0].shape, bf16)],
        scratch_shapes=ffn_weight_scratch + ffn_stage_scratch,
        compiler_params=_tc_params(1),
        name="ffn1",
    )(xt, row(ffn1_norm[0]), ffn1_w_gate[0], ffn1_w_up[0], ffn1_w_down[0], w_in[0])

    rowscale = jnp.ones((QVT_W,), f32)
    rowscale = rowscale.at[0:ATT_W].set(DA_QK_DIM ** -0.5 * LOG2E)
    rowscale = rowscale.at[ATT_W:2 * ATT_W].set(HEAD_W ** -0.5 * LOG2E)
    k, qvt, gates = pl.pallas_call(
        _in_proj_kernel,
        grid=(T // PROJ_TM,),
        in_specs=[pl.BlockSpec((PROJ_TM, D), lambda t: (t, 0)), _const_spec((1, D)),
                  _const_spec(w_in[0].shape), _const_spec((QVT_W, 1)), _const_spec((1, GATE_W))],
        out_specs=[pl.BlockSpec((PROJ_TM, K_W), lambda t: (t, 0)),
                   pl.BlockSpec((QVT_W, PROJ_TM), lambda t: (0, t)),
                   pl.BlockSpec((PROJ_TM, GATE_W), lambda t: (t, 0))],
        out_shape=[jax.ShapeDtypeStruct((T, K_W), bf16),
                   jax.ShapeDtypeStruct((QVT_W, T), bf16),
                   jax.ShapeDtypeStruct((T, GATE_W), bf16)],
        scratch_shapes=[pltpu.VMEM((QVT_W, D), bf16)],
        compiler_params=_tc_params(1),
        name="in_proj",
    )(h1, row(mix_norm[0]), w_in_bf16, rowscale.reshape(-1, 1), row(b_gate[0]))
    k3 = k.reshape(B, S, K_W)

    nq_sb = S // SB_TQ

    def sb_q_spec(hh):
        return pl.BlockSpec((HEAD_W, SB_TQ),
                            lambda b, hp, i: (N_HEADS + 2 * hp + hh, b * nq_sb + i))

    def sb_k_spec(hh):
        return pl.BlockSpec((1, S, HEAD_W), lambda b, hp, i: (b, 0, N_HEADS + 2 * hp + hh))

    def sb_vt_spec(hh):
        return pl.BlockSpec((HEAD_W, S), lambda b, hp, i: (3 * N_HEADS + 2 * hp + hh, b))

    att_out_shape = jax.ShapeDtypeStruct((B, S, ATT_W), bf16)
    smem_spec = pl.BlockSpec(memory_space=pltpu.SMEM)

    lam = (jnp.exp(jnp.sum(lambda_q1[0].astype(f32) * lambda_k1[0].astype(f32)))
           - jnp.exp(jnp.sum(lambda_q2[0].astype(f32) * lambda_k2[0].astype(f32)))
           + LAMBDA_INIT).reshape(1)
    slopes = jnp.exp2(-8.0 * jnp.arange(1, N_HEADS + 1, dtype=f32) / N_HEADS)

    mix_weights = [w_branch_diff[0], w_branch_sb[0], w_out[0],
                   ffn2_w_gate[0], ffn2_w_up[0], ffn2_w_down[0]]
    nq_diff = S // ATT_TQ
    n_pairs = N_HEADS // 2
    side_specs = [row_block_spec(w, B * n_pairs * nq_diff,
                                 lambda b, hp, i: (b * n_pairs + hp) * nq_diff + i)
                  for w in mix_weights]

    def pair_q_spec(hh):
        return pl.BlockSpec((HEAD_W, ATT_TQ), lambda b, hp, i: (2 * hp + hh, b * nq_diff + i))

    def pair_k_spec(hh):
        return pl.BlockSpec((1, S, HEAD_W), lambda b, hp, i: (b, 0, 2 * hp + hh))

    def pair_vt_spec(hh):
        return pl.BlockSpec((HEAD_W, S), lambda b, hp, i: (2 * N_HEADS + 2 * hp + hh, b))

    a, *mix_weights_bf16 = pl.pallas_call(
        _diff_attn_kernel,
        grid=(B, n_pairs, nq_diff),
        in_specs=[smem_spec, smem_spec, pair_q_spec(0), pair_q_spec(1), pair_k_spec(0),
                  pair_k_spec(1), pair_vt_spec(0), pair_vt_spec(1),
                  _const_spec((HEAD_W, 1))] + side_specs,
        out_specs=[pl.BlockSpec((1, ATT_TQ, 2 * HEAD_W), lambda b, hp, i: (b, i, hp))]
        + side_specs,
        out_shape=[att_out_shape] + [jax.ShapeDtypeStruct(w.shape, bf16) for w in mix_weights],
        scratch_shapes=[pltpu.VMEM((ATT_TK, ATT_TK), f32), pltpu.VMEM((ATT_TK, HEAD_W), bf16),
                        pltpu.VMEM((2, 2 * ATT_TQ // CHAIN_W, ATT_TK, CHAIN_W), f32),
                        pltpu.VMEM((2, 1, 2 * ATT_TQ), f32), pltpu.VMEM((2, 1, 2 * ATT_TQ), f32),
                        pltpu.VMEM((2, HEAD_W, 2 * ATT_TQ), f32)],
        compiler_params=_tc_params(3),
        name="diff_attn",
    )(slopes, lam, qvt, qvt, k3, k3, qvt, qvt, diff_subln[0].reshape(-1, 1).astype(f32),
      *mix_weights)

    b = pl.pallas_call(
        _sb_attn_kernel,
        grid=(B, n_pairs, nq_sb),
        in_specs=[sb_q_spec(0), sb_q_spec(1), sb_k_spec(0), sb_k_spec(1),
                  sb_vt_spec(0), sb_vt_spec(1)],
        out_specs=pl.BlockSpec((1, SB_TQ, 2 * HEAD_W), lambda b, hp, i: (b, i, hp)),
        out_shape=att_out_shape,
        scratch_shapes=[pltpu.VMEM((2, 1, SB_TQ), f32), pltpu.VMEM((2, HEAD_W, SB_TQ), f32)],
        compiler_params=_tc_params(3),
        name="sb_attn",
    )(qvt, qvt, k3, k3, qvt, qvt)

    out = pl.pallas_call(
        _mix_ffn2_kernel,
        grid=(T // FFN_TM,),
        in_specs=[tok_spec,
                  pl.BlockSpec((FFN_TM, ATT_W), lambda t: (t, 0)),
                  pl.BlockSpec((FFN_TM, ATT_W), lambda t: (t, 0)),
                  pl.BlockSpec((FFN_TM, GATE_W), lambda t: (t, 0)),
                  _const_spec((ATT_W, D)), _const_spec((ATT_W, D)),
                  _const_spec((D, D)), _const_spec((1, D)), _const_spec((D, D_FF)),
                  _const_spec((D, D_FF)), _const_spec((D_FF, D)), _const_spec((1, D))],
        out_specs=tok_spec,
        out_shape=jax.ShapeDtypeStruct((T, D), f32),
        compiler_params=_tc_params(1),
        name="mix_ffn2",
    )(h1, a.reshape(T, -1), b.reshape(T, -1), gates, *mix_weights_bf16[:3],
      row(ffn2_norm[0]), *mix_weights_bf16[3:], row(final_norm))
    return out.reshape(B, S, D)
```

```python
import math

import jax
import jax.numpy as jnp
from jax import lax
from jax.experimental import pallas as pl
from jax.experimental.pallas import tpu as pltpu

D_MODEL = 1024
D_FF = 2816
N_HEADS = 4
HEAD_W = 128
DA_QK_DIM = 64
ATT_W = N_HEADS * HEAD_W
K_W = 2 * ATT_W
QVT_W = 4 * ATT_W
GATE_W = 2 * D_MODEL
NORM_EPS = 1e-5
LAMBDA_INIT = 0.8 - 0.6 * math.exp(-0.3 * 0)
LOG2E = 1.0 / math.log(2.0)

VMEM_LIMIT_BYTES = 56 * 1024 * 1024

STAGE_SLOTS = 3
STAGE_WIDE_ROWS = 128
STAGE_TALL_ROWS = 352

FFN_TM = 512
PROJ_TM = 512
ATT_TQ = 2048
ATT_TK = 512
SB_TQ = 4096
CHAIN_W = 256
SB_DONE_LOG2 = 160.0
SUM_ROWS = 16
SLOPE_TERMS = 3
BF16_EXACT_INT = 256

_NT = (((1,), (1,)), ((), ()))


def _rms(x, g):
    ms = jnp.mean(x * x, axis=-1, keepdims=True)
    return x * lax.rsqrt(ms + NORM_EPS) * g


def _swiglu_half_step(x, norm_g, wg_ref, wu_ref, wd_ref):
    halves = jnp.split(x, 2, axis=0)
    xn = [_rms(h, norm_g).astype(jnp.bfloat16) for h in halves]
    gu = [(jnp.dot(n, wg_ref[...], preferred_element_type=jnp.float32),
           jnp.dot(n, wu_ref[...], preferred_element_type=jnp.float32)) for n in xn]
    out = []
    for h, (g, u) in zip(halves, gu):
        hact = (g * jax.nn.sigmoid(g) * u).astype(jnp.bfloat16)
        out.append(h + 0.5 * jnp.dot(hact, wd_ref[...], preferred_element_type=jnp.float32))
    return jnp.concatenate(out, axis=0)


def _stage_weight(src_hbm, dst_ref, stage_ref, sem_ref):
    n_slots, rows, _ = stage_ref.shape
    n_rows = src_hbm.shape[0]
    assert n_rows % rows == 0 and src_hbm.shape[1] == stage_ref.shape[2]
    n_slabs = n_rows // rows

    def slab_copy(c):
        return pltpu.make_async_copy(src_hbm.at[pl.ds(c * rows, rows)],
                                     stage_ref.at[c % n_slots], sem_ref.at[c % n_slots])

    for c in range(min(n_slots - 1, n_slabs)):
        slab_copy(c).start()
    for c in range(n_slabs):
        if c + n_slots - 1 < n_slabs:
            slab_copy(c + n_slots - 1).start()
        slab_copy(c).wait()
        dst_ref[c * rows:(c + 1) * rows, :] = stage_ref[c % n_slots].astype(dst_ref.dtype)


def _ffn1_kernel(x_ref, norm_ref, wg_hbm, wu_hbm, wd_hbm, win32_ref, o_ref, win16_ref,
                 wg_ref, wu_ref, wd_ref, stage_wide, stage_tall, sem):
    @pl.when(pl.program_id(0) == 0)
    def _():
        _stage_weight(wg_hbm, wg_ref, stage_wide, sem)
        _stage_weight(wu_hbm, wu_ref, stage_wide, sem)
        _stage_weight(wd_hbm, wd_ref, stage_tall, sem)

    win16_ref[...] = win32_ref[...].astype(win16_ref.dtype)
    o_ref[...] = _swiglu_half_step(x_ref[...], norm_ref[...], wg_ref, wu_ref, wd_ref)


def _in_proj_kernel(h_ref, norm_ref, win_ref, rowscale_ref, bgate_ref,
                    k_ref, qvt_ref, gate_ref, wqvt_ref):
    w = ATT_W

    @pl.when(pl.program_id(0) == 0)
    def _():
        for piece, src_block in enumerate((0, 3, 2, 5)):
            cols = win_ref[:, src_block * w:(src_block + 1) * w].astype(jnp.float32)
            wqvt_ref[piece * w:(piece + 1) * w, :] = cols.T.astype(wqvt_ref.dtype)

    half = h_ref.shape[0] // 2
    rows = [slice(0, half), slice(half, 2 * half)]
    n = [_rms(h_ref[r, :], norm_ref[...]).astype(jnp.bfloat16) for r in rows]
    g = [jnp.dot(nh, win_ref[:, 6 * w:], preferred_element_type=jnp.float32) for nh in n]
    for r, gh in zip(rows, g):
        gate_ref[r, :] = jax.nn.sigmoid(gh + bgate_ref[...]).astype(jnp.bfloat16)
    for r, nh in zip(rows, n):
        for piece, src_block in enumerate((1, 4)):
            k_ref[r, piece * w:(piece + 1) * w] = jnp.dot(
                nh, win_ref[:, src_block * w:(src_block + 1) * w],
                preferred_element_type=jnp.float32).astype(jnp.bfloat16)
    for r, nh in zip(rows, n):
        qvt = lax.dot_general(wqvt_ref[...], nh, _NT, preferred_element_type=jnp.float32)
        qvt_ref[:, r] = (qvt * rowscale_ref[...]).astype(jnp.bfloat16)


def _emit_pipelined(stages, n):
    state = [dict() for _ in range(n)]
    for step in range(n + len(stages) - 1):
        for s, stage in enumerate(stages):
            t = step - s
            if 0 <= t < n:
                stage(t, state[t])


def _diff_attn_kernel(slope_ref, lam_ref, q0_ref, q1_ref, k0_ref, k1_ref, vt0_ref, vt1_ref,
                      subln_ref, *refs):
    tq, tk, cw = ATT_TQ, ATT_TK, CHAIN_W
    per_map = tq // cw
    n_chains = 2 * per_map
    n_diag = tq // tk
    assert n_diag % 2 == 0 and tk == 2 * cw and tk <= 2 * BF16_EXACT_INT
    hp = pl.program_id(1)
    i = pl.program_id(2)
    lam = lam_ref[0]

    n_side = (len(refs) - 7) // 2
    w32_refs, o_ref, w16_refs = refs[:n_side], refs[n_side], refs[n_side + 1:2 * n_side + 1]
    mask_ref, kfeat_ref, s_ref, m2_ref, l2_ref, acc2_ref = refs[2 * n_side + 1:]

    @pl.when(jnp.logical_and(pl.program_id(0) == 0, jnp.logical_and(hp == 0, i == 0)))
    def _():
        krow = lax.broadcasted_iota(jnp.int32, (tk, tk), 0)
        qcol = lax.broadcasted_iota(jnp.int32, (tk, tk), 1)
        mask_ref[...] = jnp.where(qcol >= krow, 0.0, -jnp.inf)
        kpos = lax.broadcasted_iota(jnp.int32, (tk, HEAD_W), 0)
        klane = lax.broadcasted_iota(jnp.int32, (tk, HEAD_W), 1)
        k_hi = jnp.where(kpos >= BF16_EXACT_INT, BF16_EXACT_INT, 0)
        kfeat_ref[...] = jnp.where(klane < SLOPE_TERMS, k_hi,
                                   jnp.where(klane < 2 * SLOPE_TERMS, kpos - k_hi, 0)
                                   ).astype(jnp.float32).astype(jnp.bfloat16)

    heads = [
        _diff_head(i, lam, slope_ref[2 * hp + hh] * LOG2E, q_ref, k_ref, vt_ref, subln_ref,
                   o_ref, hh * HEAD_W,
                   mask_ref, kfeat_ref, s_ref, m2_ref.at[hh], l2_ref.at[hh], acc2_ref.at[hh])
        for hh, (q_ref, k_ref, vt_ref) in enumerate(
            ((q0_ref, k0_ref, vt0_ref), (q1_ref, k1_ref, vt1_ref)))]

    def side_casts():
        for w32, w16 in zip(w32_refs, w16_refs):
            w16[...] = w32[...].astype(w16.dtype)

    heads[0]["run"](True, side_casts, heads[1]["score_stage"](0, 0))
    heads[1]["run"](False, None, None)


def _diff_head(i, lam, slope, q_ref, k_ref, vt_ref, subln_ref, o_ref, o_lane0,
               mask_ref, kfeat_ref, s_ref, m_ref, l_ref, acc_ref):
    tq, tk, cw = ATT_TQ, ATT_TK, CHAIN_W
    per_map = tq // cw
    n_chains = 2 * per_map
    n_diag = tq // tk

    qt = q_ref[...]
    chan = lax.broadcasted_iota(jnp.int32, (HEAD_W, tq), 0)
    zero = jnp.zeros_like(qt)
    q_maps = (jnp.where(chan < DA_QK_DIM, qt, zero), jnp.where(chan >= DA_QK_DIM, qt, zero))

    sl = jnp.full((HEAD_W, cw), slope, jnp.float32)
    hi = sl.astype(jnp.bfloat16).astype(jnp.float32)
    mid = (sl - hi).astype(jnp.bfloat16).astype(jnp.float32)
    lo = sl - hi - mid
    frow = lax.broadcasted_iota(jnp.int32, (HEAD_W, cw), 0)
    part = frow % SLOPE_TERMS
    q_feat = jnp.where(frow < 2 * SLOPE_TERMS,
                       jnp.where(part == 0, hi, jnp.where(part == 1, mid, lo)),
                       0.0).astype(jnp.bfloat16)
    q_chain = [jnp.concatenate(
        [q_maps[c // per_map][:, (c % per_map) * cw:(c % per_map + 1) * cw], q_feat], axis=0)
        for c in range(n_chains)]

    m_ref[...] = jnp.full_like(m_ref, -jnp.inf)
    l_ref[...] = jnp.zeros_like(l_ref)
    acc_ref[...] = jnp.zeros_like(acc_ref)

    def chain_mode(c, d):
        if d is None:
            return "full"
        q_lo = (c % per_map) * cw
        if q_lo + cw <= d * tk:
            return "skip"
        if q_lo >= (d + 1) * tk:
            return "full"
        return q_lo - d * tk

    def visible_keys(c, d):
        return cw if chain_mode(c, d) == 0 else tk

    def score_stage(slot, j, d=None):
        chains = [c for c in range(n_chains) if chain_mode(c, d) != "skip"]
        kb = k_ref[0, pl.ds(pl.multiple_of(j * tk, tk), tk), :]
        kb = jnp.concatenate([kb, kfeat_ref[...]], axis=1)

        def scores(t, st):
            if t < len(chains):
                c = chains[t]
                nk = visible_keys(c, d)
                s_ref[slot, c, :nk] = jnp.dot(kb[:nk], q_chain[c],
                                              preferred_element_type=jnp.float32)

        return len(chains), scores

    def scores_to(slot, j, d=None):
        n, scores = score_stage(slot, j, d)
        for t in range(n):
            scores(t, None)

    def consume(slot, j, d=None, ahead=None):
        chains = [c for c in range(n_chains) if chain_mode(c, d) != "skip"]
        vtb = vt_ref[:, pl.ds(pl.multiple_of(j * tk, tk), tk)]
        vtb = jnp.concatenate([vtb, jnp.ones((SUM_ROWS, tk), vtb.dtype)], axis=0)
        shift = -slope * (i * tq - j * tk).astype(jnp.float32)

        def column_max(t, st):
            if t >= len(chains):
                return
            c = chains[t]
            nk = visible_keys(c, d)
            s = s_ref[slot, c, :nk]
            mode = chain_mode(c, d)
            if mode != "full":
                s = s + mask_ref[:nk, mode:mode + cw]
                s_ref[slot, c, :nk] = s
            st["cmax"] = jnp.max(s, axis=0, keepdims=True) + shift

        def softmax_pv(t, st):
            if t >= len(chains):
                return
            c = chains[t]
            nk = visible_keys(c, d)
            lanes = slice(c * cw, (c + 1) * cw)
            m_prev = m_ref[:, lanes]
            m_new = jnp.maximum(m_prev, st.pop("cmax"))
            st["alpha"] = jnp.exp2(m_prev - m_new)
            p = jnp.exp2(s_ref[slot, c, :nk] - (m_new - shift))
            m_ref[:, lanes] = m_new
            st["pv"] = jnp.dot(vtb[:, :nk], p.astype(jnp.bfloat16),
                               preferred_element_type=jnp.float32)

        def accumulate(t, st):
            if t >= len(chains):
                return
            c = chains[t]
            lanes = slice(c * cw, (c + 1) * cw)
            alpha, pv = st.pop("alpha"), st.pop("pv")
            acc_ref[:, lanes] = alpha * acc_ref[:, lanes] + pv[:HEAD_W]
            l_ref[:, lanes] = alpha * l_ref[:, lanes] + pv[HEAD_W:HEAD_W + 1]

        n_ahead, ahead_stage = ahead if ahead is not None else (0, None)
        stages = (column_max, softmax_pv, accumulate)
        if ahead is not None:
            stages = (ahead_stage,) + stages
        _emit_pipelined(stages, max(len(chains), n_ahead))

    def step(slot, j):
        consume(slot, j, ahead=score_stage(1 - slot, j + 1))

    def pair(jj, carry):
        step(0, 2 * jj)
        step(1, 2 * jj + 1)
        return carry

    def run(own_prologue, after_prologue, ahead_last):
        if own_prologue:
            scores_to(0, 0)
        if after_prologue is not None:
            after_prologue()
        first_diag = n_diag * i
        lax.fori_loop(0, first_diag // 2, pair, 0)
        for d in range(n_diag):
            ahead = (score_stage((d + 1) % 2, first_diag + d + 1, d + 1)
                     if d + 1 < n_diag else ahead_last)
            consume(d % 2, first_diag + d, d, ahead=ahead)

        o = acc_ref[...] / l_ref[...]
        a = o[:, :tq] - lam * o[:, tq:]
        ms = jnp.mean(a * a, axis=0, keepdims=True)
        a = a * lax.rsqrt(ms + NORM_EPS) * subln_ref[...] * (1.0 - LAMBDA_INIT)
        o_ref[0, :, o_lane0:o_lane0 + HEAD_W] = a.T.astype(o_ref.dtype)

    return {"score_stage": score_stage, "run": run}


def _sb_attn_kernel(q_ref, k_ref, vt_ref, o_ref, c_ref, acc_ref):
    tq, cw = SB_TQ, CHAIN_W
    n_chains = tq // cw
    i = pl.program_id(2)
    qt = q_ref[...]
    q_chain = [qt[:, c * cw:(c + 1) * cw] for c in range(n_chains)]

    krow = lax.broadcasted_iota(jnp.int32, (cw, cw), 0)
    qcol = lax.broadcasted_iota(jnp.int32, (cw, cw), 1)
    strict = krow < qcol
    lrow = lax.broadcasted_iota(jnp.int32, (cw + SUM_ROWS, cw), 0)
    lcol = lax.broadcasted_iota(jnp.int32, (cw + SUM_ROWS, cw), 1)
    later = jnp.where(jnp.logical_or(lcol > lrow, lrow >= cw), 1.0, 0.0).astype(jnp.bfloat16)

    c_ref[...] = jnp.zeros_like(c_ref)
    acc_ref[...] = jnp.zeros_like(acc_ref)

    def run_pieces(pieces):
        def scores(t, st):
            sub, c, _ = pieces[t]
            start = pl.multiple_of(sub * cw, cw)
            st["z"] = jnp.dot(k_ref[0, pl.ds(start, cw), :], q_chain[c],
                              preferred_element_type=jnp.float32)

        def suffix(t, st):
            _, _, triangular = pieces[t]
            z = st.pop("z")
            u = jnp.maximum(z, 0.0) + jnp.log2(1.0 + jnp.exp2(-jnp.abs(z)))
            st["log_sig"] = z - u
            if triangular:
                u = jnp.where(strict, u, 0.0)
            st["tail"] = jnp.dot(later, u.astype(jnp.bfloat16),
                                 preferred_element_type=jnp.float32)

        def weights_pv(t, st):
            sub, _, triangular = pieces[t]
            start = pl.multiple_of(sub * cw, cw)
            tail = st.pop("tail")
            st["usum"] = tail[cw:cw + 1]
            a = jnp.exp2(st.pop("log_sig") - tail[:cw])
            if triangular:
                a = jnp.where(strict, a, 0.0)
            st["pv"] = jnp.dot(vt_ref[:, pl.ds(start, cw)], a.astype(jnp.bfloat16),
                               preferred_element_type=jnp.float32)

        def accumulate(t, st):
            _, c, _ = pieces[t]
            lanes = slice(c * cw, (c + 1) * cw)
            carry = c_ref[:, lanes]
            acc_ref[:, lanes] += st.pop("pv") * jnp.exp2(-carry)
            c_ref[:, lanes] = carry + st.pop("usum")

        _emit_pipelined((scores, suffix, weights_pv, accumulate), len(pieces))

    diag = [n_chains * i + c for c in range(n_chains)]
    head = [(diag[c], c, True) for c in reversed(range(n_chains))]
    second = [(diag[c] - 1, c, False) for c in reversed(range(n_chains))]

    @pl.when(i == 0)
    def _():
        run_pieces(head + [p for p in second if p[1] > 0])

    @pl.when(i > 0)
    def _():
        run_pieces(head + second)

    def unfinished(c, depth):
        lanes = slice(c * cw, (c + 1) * cw)
        return jnp.logical_and(diag[c] - depth >= 0,
                               jnp.min(c_ref[:, lanes]) < SB_DONE_LOG2)

    def any_unfinished(depth):
        go = unfinished(0, depth)
        for c in range(1, n_chains):
            go = jnp.logical_or(go, unfinished(c, depth))
        return go

    def body(carry):
        depth, _ = carry
        for c in range(n_chains):
            @pl.when(unfinished(c, depth))
            def _():
                run_pieces([(diag[c] - depth, c, False)])
        return depth + 1, any_unfinished(depth + 1)

    lane_chain = lax.broadcasted_iota(jnp.int32, (1, tq), 1) // cw
    pending = jnp.where(n_chains * i + lane_chain >= 2, c_ref[...], SB_DONE_LOG2)
    lax.while_loop(lambda carry: carry[1], body,
                   (jnp.int32(2), jnp.min(pending) < SB_DONE_LOG2))
    o_ref[0] = acc_ref[...].T.astype(o_ref.dtype)


def _mix_ffn2_kernel(h_ref, a_ref, b_ref, gate_ref, wa_ref, wb_ref, wout_ref,
                     norm2_ref, wg_ref, wu_ref, wd_ref, normf_ref, o_ref):
    half = h_ref.shape[0] // 2
    rows = [slice(0, half), slice(half, 2 * half)]
    yab = [(jnp.dot(a_ref[r, :], wa_ref[...], preferred_element_type=jnp.float32),
            jnp.dot(b_ref[r, :], wb_ref[...], preferred_element_type=jnp.float32)) for r in rows]
    h2 = []
    for r, (ya, yb) in zip(rows, yab):
        gate = gate_ref[r, :].astype(jnp.float32)
        y = (gate[:, :D_MODEL] * ya + gate[:, D_MODEL:] * yb).astype(jnp.bfloat16)
        h2.append(h_ref[r, :] + jnp.dot(y, wout_ref[...], preferred_element_type=jnp.float32))
    h2 = jnp.concatenate(h2, axis=0)
    h3 = _swiglu_half_step(h2, norm2_ref[...], wg_ref, wu_ref, wd_ref)
    o_ref[...] = _rms(h3, normf_ref[...])


def _const_spec(shape):
    return pl.BlockSpec(shape, lambda *_: (0,) * len(shape), pipeline_mode=pl.Buffered(1))


def _tc_params(n_axes):
    return pltpu.CompilerParams(dimension_semantics=("arbitrary",) * n_axes,
                                vmem_limit_bytes=VMEM_LIMIT_BYTES)


def kernel(x, ffn1_norm, ffn1_w_gate, ffn1_w_up, ffn1_w_down, mix_norm, w_in, b_gate, lambda_q1, lambda_k1, lambda_q2, lambda_k2, diff_subln, w_branch_diff, w_branch_sb, w_out, ffn2_norm, ffn2_w_gate, ffn2_w_up, ffn2_w_down, final_norm):
    B, S, D = x.shape
    T = B * S
    assert D == D_MODEL and w_in.shape[1:] == (D_MODEL, 6 * ATT_W + GATE_W)
    assert S % ATT_TQ == 0 and S % SB_TQ == 0 and T % FFN_TM == 0 and T % PROJ_TM == 0
    f32, bf16 = jnp.float32, jnp.bfloat16
    xt = x.reshape(T, D)
    row = lambda v: v.reshape(1, -1).astype(f32)

    tok_spec = pl.BlockSpec((FFN_TM, D), lambda t: (t, 0))
    hbm_spec = pl.BlockSpec(memory_space=pl.ANY)
    ffn_weight_scratch = [pltpu.VMEM((D, D_FF), bf16), pltpu.VMEM((D, D_FF), bf16),
                          pltpu.VMEM((D_FF, D), bf16)]
    ffn_stage_scratch = [pltpu.VMEM((STAGE_SLOTS, STAGE_WIDE_ROWS, D_FF), f32),
                         pltpu.VMEM((STAGE_SLOTS, STAGE_TALL_ROWS, D), f32),
                         pltpu.SemaphoreType.DMA((STAGE_SLOTS,))]

    def row_block_spec(w, n_steps, step_of):
        n_rows, n_cols = w.shape
        n_blocks = max(n for n in range(1, n_steps + 1)
                       if n_steps % n == 0 and n_rows % (16 * n) == 0)
        return pl.BlockSpec((n_rows // n_blocks, n_cols),
                            lambda *idx: ((step_of(*idx) * n_blocks) // n_steps, 0))

    win_side_spec = row_block_spec(w_in[0], T // FFN_TM, lambda t: t)
    h1, w_in_bf16 = pl.pallas_call(
        _ffn1_kernel,
        grid=(T // FFN_TM,),
        in_specs=[tok_spec, _const_spec((1, D)), hbm_spec, hbm_spec, hbm_spec, win_side_spec],
        out_specs=[tok_spec, win_side_spec],
        out_shape=[jax.ShapeDtypeStruct((T, D), f32), jax.ShapeDtypeStruct(w_in[0].shape, bf16)],
        scratch_shapes=ffn_weight_scratch + ffn_stage_scratch,
        compiler_params=_tc_params(1),
        name="ffn1",
    )(xt, row(ffn1_norm[0]), ffn1_w_gate[0], ffn1_w_up[0], ffn1_w_down[0], w_in[0])

    rowscale = jnp.ones((QVT_W,), f32)
    rowscale = rowscale.at[0:ATT_W].set(DA_QK_DIM ** -0.5 * LOG2E)
    rowscale = rowscale.at[ATT_W:2 * ATT_W].set(HEAD_W ** -0.5 * LOG2E)
    k, qvt, gates = pl.pallas_call(
        _in_proj_kernel,
        grid=(T // PROJ_TM,),
        in_specs=[pl.BlockSpec((PROJ_TM, D), lambda t: (t, 0)), _const_spec((1, D)),
                  _const_spec(w_in[0].shape), _const_spec((QVT_W, 1)), _const_spec((1, GATE_W))],
        out_specs=[pl.BlockSpec((PROJ_TM, K_W), lambda t: (t, 0)),
                   pl.BlockSpec((QVT_W, PROJ_TM), lambda t: (0, t)),
                   pl.BlockSpec((PROJ_TM, GATE_W), lambda t: (t, 0))],
        out_shape=[jax.ShapeDtypeStruct((T, K_W), bf16),
                   jax.ShapeDtypeStruct((QVT_W, T), bf16),
                   jax.ShapeDtypeStruct((T, GATE_W), bf16)],
        scratch_shapes=[pltpu.VMEM((QVT_W, D), bf16)],
        compiler_params=_tc_params(1),
        name="in_proj",
    )(h1, row(mix_norm[0]), w_in_bf16, rowscale.reshape(-1, 1), row(b_gate[0]))
    k3 = k.reshape(B, S, K_W)

    def q_spec(slab, tq):
        return pl.BlockSpec((HEAD_W, tq),
                            lambda b, h, i: (slab * N_HEADS + h, b * (S // tq) + i))

    def k_spec(slab):
        return pl.BlockSpec((1, S, HEAD_W), lambda b, h, i: (b, 0, slab * N_HEADS + h))

    def vt_spec(slab):
        return pl.BlockSpec((HEAD_W, S), lambda b, h, i: (slab * N_HEADS + h, b))

    def att_out_spec(tq):
        return pl.BlockSpec((1, tq, HEAD_W), lambda b, h, i: (b, i, h))

    att_out_shape = jax.ShapeDtypeStruct((B, S, ATT_W), bf16)
    smem_spec = pl.BlockSpec(memory_space=pltpu.SMEM)

    lam = (jnp.exp(jnp.sum(lambda_q1[0].astype(f32) * lambda_k1[0].astype(f32)))
           - jnp.exp(jnp.sum(lambda_q2[0].astype(f32) * lambda_k2[0].astype(f32)))
           + LAMBDA_INIT).reshape(1)
    slopes = jnp.exp2(-8.0 * jnp.arange(1, N_HEADS + 1, dtype=f32) / N_HEADS)

    mix_weights = [w_branch_diff[0], w_branch_sb[0], w_out[0],
                   ffn2_w_gate[0], ffn2_w_up[0], ffn2_w_down[0]]
    nq_diff = S // ATT_TQ
    n_pairs = N_HEADS // 2
    side_specs = [row_block_spec(w, B * n_pairs * nq_diff,
                                 lambda b, hp, i: (b * n_pairs + hp) * nq_diff + i)
                  for w in mix_weights]

    def pair_q_spec(hh):
        return pl.BlockSpec((HEAD_W, ATT_TQ), lambda b, hp, i: (2 * hp + hh, b * nq_diff + i))

    def pair_k_spec(hh):
        return pl.BlockSpec((1, S, HEAD_W), lambda b, hp, i: (b, 0, 2 * hp + hh))

    def pair_vt_spec(hh):
        return pl.BlockSpec((HEAD_W, S), lambda b, hp, i: (2 * N_HEADS + 2 * hp + hh, b))

    a, *mix_weights_bf16 = pl.pallas_call(
        _diff_attn_kernel,
        grid=(B, n_pairs, nq_diff),
        in_specs=[smem_spec, smem_spec, pair_q_spec(0), pair_q_spec(1), pair_k_spec(0),
                  pair_k_spec(1), pair_vt_spec(0), pair_vt_spec(1),
                  _const_spec((HEAD_W, 1))] + side_specs,
        out_specs=[pl.BlockSpec((1, ATT_TQ, 2 * HEAD_W), lambda b, hp, i: (b, i, hp))]
        + side_specs,
        out_shape=[att_out_shape] + [jax.ShapeDtypeStruct(w.shape, bf16) for w in mix_weights],
        scratch_shapes=[pltpu.VMEM((ATT_TK, ATT_TK), f32), pltpu.VMEM((ATT_TK, HEAD_W), bf16),
                        pltpu.VMEM((2, 2 * ATT_TQ // CHAIN_W, ATT_TK, CHAIN_W), f32),
                        pltpu.VMEM((2, 1, 2 * ATT_TQ), f32), pltpu.VMEM((2, 1, 2 * ATT_TQ), f32),
                        pltpu.VMEM((2, HEAD_W, 2 * ATT_TQ), f32)],
        compiler_params=_tc_params(3),
        name="diff_attn",
    )(slopes, lam, qvt, qvt, k3, k3, qvt, qvt, diff_subln[0].reshape(-1, 1).astype(f32),
      *mix_weights)

    b = pl.pallas_call(
        _sb_attn_kernel,
        grid=(B, N_HEADS, S // SB_TQ),
        in_specs=[q_spec(1, SB_TQ), k_spec(1), vt_spec(3)],
        out_specs=att_out_spec(SB_TQ),
        out_shape=att_out_shape,
        scratch_shapes=[pltpu.VMEM((1, SB_TQ), f32), pltpu.VMEM((HEAD_W, SB_TQ), f32)],
        compiler_params=_tc_params(3),
        name="sb_attn",
    )(qvt, k3, qvt)

    out = pl.pallas_call(
        _mix_ffn2_kernel,
        grid=(T // FFN_TM,),
        in_specs=[tok_spec,
                  pl.BlockSpec((FFN_TM, ATT_W), lambda t: (t, 0)),
                  pl.BlockSpec((FFN_TM, ATT_W), lambda t: (t, 0)),
                  pl.BlockSpec((FFN_TM, GATE_W), lambda t: (t, 0)),
                  _const_spec((ATT_W, D)), _const_spec((ATT_W, D)),
                  _const_spec((D, D)), _const_spec((1, D)), _const_spec((D, D_FF)),
                  _const_spec((D, D_FF)), _const_spec((D_FF, D)), _const_spec((1, D))],
        out_specs=tok_spec,
        out_shape=jax.ShapeDtypeStruct((T, D), f32),
        compiler_params=_tc_params(1),
        name="mix_ffn2",
    )(h1, a.reshape(T, -1), b.reshape(T, -1), gates, *mix_weights_bf16[:3],
      row(ffn2_norm[0]), *mix_weights_bf16[3:], row(final_norm))
    return out.reshape(B, S, D)
```
